```python
import jax, jax.numpy as jnp
from jax import lax
import numpy as np

D_MODEL = 2048
BATCH = 4
SEQ = 2048
DEPTH = 2
DEC_BATCH = 128
DEC_SEQ = 4
PAST_LEN = 8192
PAGE_SIZE = 128

N_MIXERS = 2
N_LAYERS_A = (DEPTH + 1) // N_MIXERS
N_LAYERS_B = DEPTH // N_MIXERS
HEAD_DIM = 64
ROPE_THETA = 10000.0
BLOCK = 128
A_HEADS = D_MODEL // HEAD_DIM
A_KV_HEADS = max(1, A_HEADS // 8)
A_WINDOW = 128
A_QKV = (A_HEADS + 2 * A_KV_HEADS) * HEAD_DIM
B_GROUPS = ((128, 1), (512, 4), (2048, 16))
N_B_GROUPS = len(B_GROUPS)
B_HEADS = D_MODEL // (2 * HEAD_DIM)
B_GROUP_WIDTH = B_HEADS * HEAD_DIM
B_QKV = 3 * N_B_GROUPS * B_GROUP_WIDTH
PEER_HEADS = 8
PEER_N_KEYS = 128
PEER_N_EXPERTS = PEER_N_KEYS * PEER_N_KEYS
PEER_TOPK = 16
PEER_KEY_DIM = 256
PEER_TOKEN_BLOCK = 128
DN_ALPHA = (2.0 * DEPTH) ** 0.25
DN_BETA = (8.0 * DEPTH) ** -0.25
LN_EPS = 1e-5

kernel_name = "hybrid_swa_sink_dilated_peer_step"


def layer_norm(x, g, b):
    xf = x.astype(jnp.float32)
    mu = xf.mean(-1, keepdims=True)
    var = jnp.square(xf - mu).mean(-1, keepdims=True)
    y = (xf - mu) * lax.rsqrt(var + LN_EPS)
    return (y * g.astype(jnp.float32) + b.astype(jnp.float32)).astype(x.dtype)


def rope(x, pos):
    half = HEAD_DIM // 2
    inv = ROPE_THETA ** (-jnp.arange(half, dtype=jnp.float32) / half)
    ang = pos.astype(jnp.float32)[:, None] * inv[None, :]
    ang = ang.reshape(ang.shape[0], *([1] * (x.ndim - 3)), half)
    cos, sin = jnp.cos(ang), jnp.sin(ang)
    xf = x.astype(jnp.float32)
    x1, x2 = xf[..., :half], xf[..., half:]
    return jnp.concatenate([x1 * cos - x2 * sin, x2 * cos + x1 * sin], -1).astype(x.dtype)


def sink_softmax(s, valid, sink):
    s = jnp.where(valid, s, -jnp.inf)
    m = s.max(-1, keepdims=True)
    if sink is not None:
        m = jnp.maximum(m, sink)
    p = jnp.exp(s - m)
    den = p.sum(-1, keepdims=True)
    if sink is not None:
        den = den + jnp.exp(sink - m)
    return p / den, (m + jnp.log(den))[..., 0]


def banded_attention(q, k, v, window, sink):
    n, s_len, h, hd = q.shape
    kvh = k.shape[2]
    g = h // kvh
    s_pad = -(-s_len // BLOCK) * BLOCK
    pad = [(0, 0), (0, s_pad - s_len), (0, 0), (0, 0)]
    q, k, v = jnp.pad(q, pad), jnp.pad(k, pad), jnp.pad(v, pad)
    nb = s_pad // BLOCK
    qb = q.reshape(n, nb, BLOCK, kvh, g, hd)

    def with_prev(t):
        tb = t.reshape(n, nb, BLOCK, kvh, hd)
        prev = jnp.pad(tb, [(0, 0), (1, 0), (0, 0), (0, 0), (0, 0)])[:, :-1]
        return jnp.concatenate([prev, tb], axis=2)

    kb, vb = with_prev(k), with_prev(v)
    s = jnp.einsum('nbqkgd,nbskd->nbkgqs', qb, kb, preferred_element_type=jnp.float32) * (hd ** -0.5)
    qi = jnp.arange(BLOCK)[:, None]
    kj = jnp.arange(2 * BLOCK)[None, :]
    dist = qi + BLOCK - kj
    band = (dist >= 0) & (dist <= window)
    key_ok = (jnp.arange(nb)[:, None, None] > 0) | (kj[None] >= BLOCK)
    valid = (band[None] & key_ok)[None, :, None, None]
    sink_b = None if sink is None else sink.astype(jnp.float32).reshape(1, 1, kvh, g, 1, 1)
    p, lse = sink_softmax(s, valid, sink_b)
    o = jnp.einsum('nbkgqs,nbskd->nbqkgd', p.astype(v.dtype), vb)
    o = o.reshape(n, s_pad, h, hd)[:, :s_len]
    lse = jnp.moveaxis(lse, -1, 2).reshape(n, s_pad, h)[:, :s_len]
    return o, lse


def a_qkv(x, pos, w_qkv, b_qkv):
    lead = x.shape[:2]
    qkv = x @ w_qkv + b_qkv
    nq = A_HEADS * HEAD_DIM
    nk = A_KV_HEADS * HEAD_DIM
    q = qkv[..., :nq].reshape(*lead, A_HEADS, HEAD_DIM)
    k = qkv[..., nq:nq + nk].reshape(*lead, A_KV_HEADS, HEAD_DIM)
    v = qkv[..., nq + nk:].reshape(*lead, A_KV_HEADS, HEAD_DIM)
    return rope(q, pos), rope(k, pos), v


def a_window_step(q, k_new, v_new, cache_kv, sinks):
    db, t_len = q.shape[:2]
    length = cache_kv.shape[1]
    g = A_HEADS // A_KV_HEADS
    k = jnp.concatenate([cache_kv[:, :, 0], k_new], 1)
    v = jnp.concatenate([cache_kv[:, :, 1], v_new], 1)
    qg = q.reshape(db, t_len, A_KV_HEADS, g, HEAD_DIM)
    s = jnp.einsum('btkgd,bskd->bkgts', qg, k, preferred_element_type=jnp.float32) * (HEAD_DIM ** -0.5)
    dist = (length + jnp.arange(t_len))[:, None] - jnp.arange(length + t_len)[None, :]
    valid = (dist >= 0) & (dist <= A_WINDOW)
    p, _ = sink_softmax(s, valid, sinks.astype(jnp.float32).reshape(1, A_KV_HEADS, g, 1, 1))
    o = jnp.einsum('bkgts,bskd->btkgd', p.astype(v.dtype), v)
    return o.reshape(db, t_len, A_HEADS, HEAD_DIM)


def mixer_a(xp, xs, cache_kv, pos_p, pos_s, w_qkv, b_qkv, sinks, w_o, b_o):
    qp, kp, vp = a_qkv(xp, pos_p, w_qkv, b_qkv)
    op, _ = banded_attention(qp, kp, vp, A_WINDOW, sinks)
    qs, ks, vs = a_qkv(xs, pos_s, w_qkv, b_qkv)
    os_ = a_window_step(qs, ks, vs, cache_kv, sinks)
    yp = op.reshape(*xp.shape[:2], -1) @ w_o + b_o
    ys = os_.reshape(*xs.shape[:2], -1) @ w_o + b_o
    keep = min(A_WINDOW, xp.shape[1])
    kv_p = jnp.stack([kp[:, -keep:], vp[:, -keep:]], 2)
    kv_s = jnp.stack([ks, vs], 2)
    return yp, ys, kv_p, kv_s


def b_qkv(x, pos, w_qkv):
    lead = x.shape[:2]
    qkv = (x @ w_qkv).reshape(*lead, 3, N_B_GROUPS, B_HEADS, HEAD_DIM)
    return rope(qkv[:, :, 0], pos), rope(qkv[:, :, 1], pos), qkv[:, :, 2]


def dilated_prompt(q, k, v, window, dilation):
    n, s_len, h, hd = q.shape
    sub = s_len // dilation

    def to_res(t):
        return t.reshape(n, sub, dilation, h, hd).transpose(0, 2, 1, 3, 4).reshape(n * dilation, sub, h, hd)

    o, lse = banded_attention(to_res(q), to_res(k), to_res(v), window // dilation, None)
    o = o.reshape(n, dilation, sub, h, hd).transpose(0, 2, 1, 3, 4).reshape(n, s_len, h, hd)
    lse = lse.reshape(n, dilation, sub, h).transpose(0, 2, 1, 3).reshape(n, s_len, h)
    return o, lse


def dilated_step(q, k_new, v_new, cache_kv, window, dilation):
    t_len = q.shape[1]
    length = cache_kv.shape[1]
    k = jnp.concatenate([cache_kv[:, :, 0], k_new], 1)
    v = jnp.concatenate([cache_kv[:, :, 1], v_new], 1)
    n_k = window // dilation + 1
    idx = (length + jnp.arange(t_len))[:, None] - dilation * jnp.arange(n_k)[None, :]
    valid = idx >= 0
    idx_c = jnp.maximum(idx, 0)
    kg = k[:, idx_c]
    vg = v[:, idx_c]
    s = jnp.einsum('bthd,btkhd->bthk', q, kg, preferred_element_type=jnp.float32) * (HEAD_DIM ** -0.5)
    p, lse = sink_softmax(s, valid[None, :, None, :], None)
    o = jnp.einsum('bthk,btkhd->bthd', p.astype(vg.dtype), vg)
    return o, lse


def merge_by_denominator(outs, lses):
    w = jax.nn.softmax(jnp.stack(lses, 0), axis=0)
    o = jnp.stack(outs, 0).astype(jnp.float32)
    return (w[..., None] * o).sum(0).astype(outs[0].dtype)


def mixer_b(xp, xs, caches, pos_p, pos_s, w_qkv, w_o):
    qp, kp, vp = b_qkv(xp, pos_p, w_qkv)
    qs, ks, vs = b_qkv(xs, pos_s, w_qkv)
    outs_p, lses_p, outs_s, lses_s, kv_p, kv_s = [], [], [], [], [], []
    for g, (window, dil) in enumerate(B_GROUPS):
        o, l = dilated_prompt(qp[:, :, g], kp[:, :, g], vp[:, :, g], window, dil)
        outs_p.append(o)
        lses_p.append(l)
        o, l = dilated_step(qs[:, :, g], ks[:, :, g], vs[:, :, g], caches[g], window, dil)
        outs_s.append(o)
        lses_s.append(l)
        keep = min(window, xp.shape[1])
        kv_p.append(jnp.stack([kp[:, -keep:, g], vp[:, -keep:, g]], 2))
        kv_s.append(jnp.stack([ks[:, :, g], vs[:, :, g]], 2))
    yp = merge_by_denominator(outs_p, lses_p).reshape(*xp.shape[:2], -1) @ w_o
    ys = merge_by_denominator(outs_s, lses_s).reshape(*xs.shape[:2], -1) @ w_o
    return yp, ys, kv_p, kv_s


def peer_ffn(x, w_query, sub_keys, u_tab, v_tab):
    lead = x.shape[:2]
    xt = x.reshape(-1, D_MODEL)
    n = xt.shape[0]
    q = (xt @ w_query).reshape(n, PEER_HEADS, 2, PEER_KEY_DIM // 2)
    sc = jnp.einsum('nhcd,hckd->nhck', q, sub_keys, preferred_element_type=jnp.float32)
    top_s, top_i = lax.top_k(sc, PEER_TOPK)
    cand_s = top_s[:, :, 0, :, None] + top_s[:, :, 1, None, :]
    cand_i = top_i[:, :, 0, :, None] * PEER_N_KEYS + top_i[:, :, 1, None, :]
    best_s, best_j = lax.top_k(cand_s.reshape(n, PEER_HEADS, PEER_TOPK * PEER_TOPK), PEER_TOPK)
    expert = jnp.take_along_axis(cand_i.reshape(n, PEER_HEADS, -1), best_j, axis=-1)
    gate = jax.nn.softmax(best_s, axis=-1)
    n_sel = PEER_HEADS * PEER_TOPK
    n_pad = -(-n // PEER_TOKEN_BLOCK) * PEER_TOKEN_BLOCK
    nblk = n_pad // PEER_TOKEN_BLOCK
    xb = jnp.pad(xt, [(0, n_pad - n), (0, 0)]).reshape(nblk, PEER_TOKEN_BLOCK, D_MODEL)
    eb = jnp.pad(expert.reshape(n, n_sel), [(0, n_pad - n), (0, 0)]).reshape(nblk, PEER_TOKEN_BLOCK, n_sel)
    gb = jnp.pad(gate.reshape(n, n_sel), [(0, n_pad - n), (0, 0)]).reshape(nblk, PEER_TOKEN_BLOCK, n_sel)

    def block(args):
        xc, ec, gc = args
        u = jnp.take(u_tab, ec, axis=0)
        a = jnp.einsum('cd,ced->ce', xc, u, preferred_element_type=jnp.float32)
        hg = (gc * jax.nn.gelu(a, approximate=False)).astype(xc.dtype)
        vv = jnp.take(v_tab, ec, axis=0)
        return jnp.einsum('ce,ced->cd', hg, vv)

    out = lax.map(block, (xb, eb, gb)).reshape(n_pad, D_MODEL)[:n]
    return out.reshape(*lead, D_MODEL)


def setup_inputs(seed: int = 0) -> dict:
    key = jax.random.key(seed)
    ks = jax.random.split(key, 24)
    f32 = jnp.float32

    def nrm(k, shape, scale):
        return jax.random.normal(k, shape, f32) * scale

    return {
        "x_prompt": nrm(ks[0], (BATCH, SEQ, D_MODEL), 1.0),
        "x_sample": nrm(ks[1], (DEC_BATCH, DEC_SEQ, D_MODEL), 1.0),
        "cache_a_kv": nrm(ks[2], (N_LAYERS_A, DEC_BATCH, min(A_WINDOW, PAST_LEN), 2, A_KV_HEADS, HEAD_DIM), 1.0),
        "cache_b1_kv": nrm(ks[3], (N_LAYERS_B, DEC_BATCH, min(B_GROUPS[0][0], PAST_LEN), 2, B_HEADS, HEAD_DIM), 1.0),
        "cache_b2_kv": nrm(ks[4], (N_LAYERS_B, DEC_BATCH, min(B_GROUPS[1][0], PAST_LEN), 2, B_HEADS, HEAD_DIM), 1.0),
        "cache_b3_kv": nrm(ks[5], (N_LAYERS_B, DEC_BATCH, min(B_GROUPS[2][0], PAST_LEN), 2, B_HEADS, HEAD_DIM), 1.0),
        "w_qkv_a": nrm(ks[6], (N_LAYERS_A, D_MODEL, A_QKV), D_MODEL ** -0.5),
        "b_qkv_a": nrm(ks[7], (N_LAYERS_A, A_QKV), 0.02),
        "sinks_a": nrm(ks[8], (N_LAYERS_A, A_HEADS), 1.0),
        "w_o_a": nrm(ks[9], (N_LAYERS_A, A_HEADS * HEAD_DIM, D_MODEL), DN_BETA * (A_HEADS * HEAD_DIM) ** -0.5),
        "b_o_a": nrm(ks[10], (N_LAYERS_A, D_MODEL), 0.02),
        "w_qkv_b": nrm(ks[11], (N_LAYERS_B, D_MODEL, B_QKV), D_MODEL ** -0.5),
        "w_o_b": nrm(ks[12], (N_LAYERS_B, B_GROUP_WIDTH, D_MODEL), DN_BETA * B_GROUP_WIDTH ** -0.5),
        "ln_mix_g": 1.0 + nrm(ks[13], (DEPTH, D_MODEL), 0.05),
        "ln_mix_b": nrm(ks[14], (DEPTH, D_MODEL), 0.02),
        "ln_ffn_g": 1.0 + nrm(ks[15], (DEPTH, D_MODEL), 0.05),
        "ln_ffn_b": nrm(ks[16], (DEPTH, D_MODEL), 0.02),
        "peer_w_query": nrm(ks[17], (DEPTH, D_MODEL, PEER_HEADS * PEER_KEY_DIM), D_MODEL ** -0.5),
        "peer_sub_keys": nrm(ks[18], (DEPTH, PEER_HEADS, 2, PEER_N_KEYS, PEER_KEY_DIM // 2), (PEER_KEY_DIM // 2) ** -0.5),
        "peer_u": nrm(ks[19], (DEPTH, PEER_N_EXPERTS, D_MODEL), D_MODEL ** -0.5),
        "peer_v": nrm(ks[20], (DEPTH, PEER_N_EXPERTS, D_MODEL), DN_BETA),
    }


def reference(x_prompt, x_sample, cache_a_kv, cache_b1_kv, cache_b2_kv, cache_b3_kv,
              w_qkv_a, b_qkv_a, sinks_a, w_o_a, b_o_a, w_qkv_b, w_o_b,
              ln_mix_g, ln_mix_b, ln_ffn_g, ln_ffn_b,
              peer_w_query, peer_sub_keys, peer_u, peer_v):
    xp, xs = x_prompt, x_sample
    pos_p = jnp.arange(xp.shape[1], dtype=jnp.int32)
    pos_s = PAST_LEN + jnp.arange(xs.shape[1], dtype=jnp.int32)
    a_kv_p, a_kv_s = [], []
    b_kv_p = [[] for _ in B_GROUPS]
    b_kv_s = [[] for _ in B_GROUPS]
    for layer in range(DEPTH):
        i = layer // N_MIXERS
        if layer % N_MIXERS == 0:
            mp, ms, kvp, kvs = mixer_a(xp, xs, cache_a_kv[i], pos_p, pos_s,
                                       w_qkv_a[i], b_qkv_a[i], sinks_a[i], w_o_a[i], b_o_a[i])
            a_kv_p.append(kvp)
            a_kv_s.append(kvs)
        else:
            mp, ms, kvps, kvss = mixer_b(xp, xs, (cache_b1_kv[i], cache_b2_kv[i], cache_b3_kv[i]),
                                         pos_p, pos_s, w_qkv_b[i], w_o_b[i])
            for g in range(N_B_GROUPS):
                b_kv_p[g].append(kvps[g])
                b_kv_s[g].append(kvss[g])
        xp = layer_norm(DN_ALPHA * xp + mp, ln_mix_g[layer], ln_mix_b[layer])
        xs = layer_norm(DN_ALPHA * xs + ms, ln_mix_g[layer], ln_mix_b[layer])
        fp = peer_ffn(xp, peer_w_query[layer], peer_sub_keys[layer], peer_u[layer], peer_v[layer])
        fs = peer_ffn(xs, peer_w_query[layer], peer_sub_keys[layer], peer_u[layer], peer_v[layer])
        xp = layer_norm(DN_ALPHA * xp + fp, ln_ffn_g[layer], ln_ffn_b[layer])
        xs = layer_norm(DN_ALPHA * xs + fs, ln_ffn_g[layer], ln_ffn_b[layer])
    return (xp, xs,
            jnp.stack(a_kv_p), jnp.stack(a_kv_s),
            jnp.stack(b_kv_p[0]), jnp.stack(b_kv_s[0]),
            jnp.stack(b_kv_p[1]), jnp.stack(b_kv_s[1]),
            jnp.stack(b_kv_p[2]), jnp.stack(b_kv_s[2]))
```

```python
import functools
import math

import jax
import jax.numpy as jnp
from jax import lax
from jax.experimental import pallas as pl
from jax.experimental.pallas import tpu as pltpu

D_MODEL = 2048
SEQ = 2048
DEPTH = 2
DEC_SEQ = 4
PAST_LEN = 8192
HEAD_DIM = 64
ROPE_THETA = 10000.0
BLOCK = 128
A_HEADS = D_MODEL // HEAD_DIM
A_KV_HEADS = A_HEADS // 8
A_WINDOW = 128
A_QKV = (A_HEADS + 2 * A_KV_HEADS) * HEAD_DIM
B_GROUPS = ((128, 1), (512, 4), (2048, 16))
N_B_GROUPS = len(B_GROUPS)
B_HEADS = D_MODEL // (2 * HEAD_DIM)
B_GROUP_WIDTH = B_HEADS * HEAD_DIM
B_QKV = 3 * N_B_GROUPS * B_GROUP_WIDTH
PEER_HEADS = 8
PEER_N_KEYS = 128
PEER_N_EXPERTS = PEER_N_KEYS * PEER_N_KEYS
PEER_TOPK = 16
PEER_KEY_DIM = 256
DN_ALPHA = (2.0 * DEPTH) ** 0.25
LN_EPS = 1e-5

LANES = 128
TOKEN_TILE = 512
VMEM_LIMIT = 56 * 1024 * 1024

_F32 = jnp.float32
_BF16 = jnp.bfloat16
_NEG_INF = float("-inf")


def _cparams(*sem):
    return pltpu.CompilerParams(dimension_semantics=sem, vmem_limit_bytes=VMEM_LIMIT)


def _qkv_rope_kernel(x_ref, w_ref, b_ref, cos_ref, sin_ref, o_ref, *, n_rope_tiles):
    j = pl.program_id(1)
    tm, tn = o_ref.shape
    acc = jnp.dot(x_ref[...].astype(_BF16), w_ref[...], preferred_element_type=_F32) + b_ref[...]

    @pl.when(j < n_rope_tiles)
    def _():
        c = cos_ref[...]
        s = sin_ref[...]
        lane = lax.broadcasted_iota(jnp.int32, (tm, LANES), 1)
        first_half = (lane % HEAD_DIM) < (HEAD_DIM // 2)
        for g in range(tn // LANES):
            blk = acc[:, g * LANES:(g + 1) * LANES]
            partner = jnp.where(first_half, pltpu.roll(blk, LANES - HEAD_DIM // 2, 1),
                                pltpu.roll(blk, HEAD_DIM // 2, 1))
            o_ref[:, g * LANES:(g + 1) * LANES] = blk * c + partner * s

    @pl.when(j >= n_rope_tiles)
    def _():
        o_ref[...] = acc


def _qkv_rope(x, w_bf16, bias, cos, sin, n_rope_cols, tn=512):
    n, k = x.shape
    ncols = w_bf16.shape[1]
    tm = TOKEN_TILE
    return pl.pallas_call(
        functools.partial(_qkv_rope_kernel, n_rope_tiles=n_rope_cols // tn),
        grid=(n // tm, ncols // tn),
        in_specs=[
            pl.BlockSpec((tm, k), lambda i, j: (i, 0)),
            pl.BlockSpec((k, tn), lambda i, j: (0, j)),
            pl.BlockSpec((1, tn), lambda i, j: (0, j)),
            pl.BlockSpec((tm, LANES), lambda i, j: (i, 0)),
            pl.BlockSpec((tm, LANES), lambda i, j: (i, 0)),
        ],
        out_specs=pl.BlockSpec((tm, tn), lambda i, j: (i, j)),
        out_shape=jax.ShapeDtypeStruct((n, ncols), _F32),
        compiler_params=_cparams("parallel", "arbitrary"),
        name="qkv_rope",
    )(x, w_bf16, bias, cos, sin)


def _band_attn_kernel(sink_ref, q_ref, kp_ref, kc_ref, vp_ref, vc_ref, *out_refs,
                      n_pairs, shared_kv, has_sink, with_lse):
    o_ref = out_refs[0]
    g = pl.program_id(1)
    j = pl.program_id(2)
    qi = lax.broadcasted_iota(jnp.int32, (BLOCK, 2 * BLOCK), 0)
    kj = lax.broadcasted_iota(jnp.int32, (BLOCK, 2 * BLOCK), 1)
    first_key = jnp.maximum(qi, jnp.where(j > 0, 0, BLOCK))
    valid = (kj >= first_key) & (kj <= qi + BLOCK)
    lane = lax.broadcasted_iota(jnp.int32, (BLOCK, LANES), 1)
    low = lane < HEAD_DIM

    def kv_pair(prev_ref, cur_ref, p):
        if shared_kv:
            prev, cur = prev_ref[...], cur_ref[...]
            keep = (lane // HEAD_DIM) == (g % 2)
            prev = jnp.where(keep, prev, pltpu.roll(prev, HEAD_DIM, 1))
            cur = jnp.where(keep, cur, pltpu.roll(cur, HEAD_DIM, 1))
        else:
            prev = prev_ref[:, p * LANES:(p + 1) * LANES]
            cur = cur_ref[:, p * LANES:(p + 1) * LANES]
        return jnp.concatenate([prev, cur], axis=0).astype(_BF16)

    for p in range(n_pairs):
        if p == 0 or not shared_kv:
            kcat = kv_pair(kp_ref, kc_ref, p)
            vcat = kv_pair(vp_ref, vc_ref, p)
        q2 = q_ref[:, p * LANES:(p + 1) * LANES]
        outs, lses = [], []
        for h in range(2):
            qm = jnp.where(low if h == 0 else ~low, q2, 0.0).astype(_BF16)
            s = lax.dot_general(qm, kcat, (((1,), (1,)), ((), ())), preferred_element_type=_F32)
            s = jnp.where(valid, s, _NEG_INF)
            m = jnp.max(s, axis=-1, keepdims=True)
            if has_sink:
                sink = sink_ref[g * (2 * n_pairs) + 2 * p + h]
                m = jnp.maximum(m, sink)
            e = jnp.exp(s - m)
            den = jnp.sum(e, axis=-1, keepdims=True)
            if has_sink:
                den = den + jnp.exp(sink - m)
            pn = (e / den).astype(_BF16)
            outs.append(jnp.dot(pn, vcat, preferred_element_type=_F32))
            lses.append(m + jnp.log(den))
        o_ref[:, p * LANES:(p + 1) * LANES] = jnp.where(low, outs[0], outs[1])
        if with_lse:
            out_refs[1][:, p * LANES:(p + 1) * LANES] = jnp.where(low, lses[0], lses[1])


def _band_attention(qkv, sinks, *, n_batch, dil, q_blk0, k_blk0, v_blk0, q_width, kv_width,
                    n_groups, shared_kv, row_width, out_width, with_lse):
    rows = qkv.shape[0]
    nb = SEQ // dil // BLOCK
    view = qkv.reshape(rows // dil, dil * row_width)
    q_per = row_width // q_width
    kv_per = row_width // kv_width

    if shared_kv:
        q_map = lambda b, g, j, s: (b * nb + j, q_blk0 + g)
        kc_map = lambda b, g, j, s: (b * nb + j, k_blk0 + g // 2)
        kp_map = lambda b, g, j, s: (b * nb + jnp.maximum(j - 1, 0), k_blk0 + g // 2)
        vc_map = lambda b, g, j, s: (b * nb + j, v_blk0 + g // 2)
        vp_map = lambda b, g, j, s: (b * nb + jnp.maximum(j - 1, 0), v_blk0 + g // 2)
        o_map = lambda b, g, j, s: (b * nb + j, g)
    else:
        q_map = lambda b, r, j, s: (b * nb + j, r * q_per + q_blk0)
        kc_map = lambda b, r, j, s: (b * nb + j, r * kv_per + k_blk0)
        kp_map = lambda b, r, j, s: (b * nb + jnp.maximum(j - 1, 0), r * kv_per + k_blk0)
        vc_map = lambda b, r, j, s: (b * nb + j, r * kv_per + v_blk0)
        vp_map = lambda b, r, j, s: (b * nb + jnp.maximum(j - 1, 0), r * kv_per + v_blk0)
        o_map = lambda b, r, j, s: (b * nb + j, r)

    out_rows = n_batch * SEQ // dil
    o_sds = jax.ShapeDtypeStruct((out_rows, dil * out_width), _F32)
    o_spec = pl.BlockSpec((BLOCK, q_width), o_map)
    outs = pl.pallas_call(
        functools.partial(_band_attn_kernel, n_pairs=q_width // LANES, shared_kv=shared_kv,
                          has_sink=shared_kv, with_lse=with_lse),
        grid_spec=pltpu.PrefetchScalarGridSpec(
            num_scalar_prefetch=1,
            grid=(n_batch, n_groups, nb),
            in_specs=[
                pl.BlockSpec((BLOCK, q_width), q_map),
                pl.BlockSpec((BLOCK, kv_width), kp_map),
                pl.BlockSpec((BLOCK, kv_width), kc_map),
                pl.BlockSpec((BLOCK, kv_width), vp_map),
                pl.BlockSpec((BLOCK, kv_width), vc_map),
            ],
            out_specs=[o_spec, o_spec] if with_lse else [o_spec],
        ),
        out_shape=[o_sds, o_sds] if with_lse else [o_sds],
        compiler_params=_cparams("parallel", "parallel", "arbitrary"),
        name="band_attn_d%d" % dil,
    )(sinks, view, view, view, view, view)
    return [o.reshape(n_batch * SEQ, out_width) for o in outs]


def _layer_norm_rows(z, g, b):
    mu = jnp.mean(z, axis=-1, keepdims=True)
    zc = z - mu
    var = jnp.mean(zc * zc, axis=-1, keepdims=True)
    return zc * lax.rsqrt(var + LN_EPS) * g + b


def _proj_ln_kernel(*refs, n_merge, has_bias):
    pos = 0
    if n_merge:
        o_refs = refs[:n_merge]
        l_refs = refs[n_merge:2 * n_merge]
        pos = 2 * n_merge
        ls = [r[...] for r in l_refs]
        m = functools.reduce(jnp.maximum, ls)
        es = [jnp.exp(l - m) for l in ls]
        den = functools.reduce(jnp.add, es)
        o = functools.reduce(jnp.add, [(e / den) * r[...] for e, r in zip(es, o_refs)])
    else:
        o = refs[0][...]
        pos = 1
    w_ref = refs[pos]
    pos += 1
    y = jnp.dot(o.astype(_BF16), w_ref[...], preferred_element_type=_F32)
    if has_bias:
        y = y + refs[pos][...]
        pos += 1
    x_ref, g_ref, b_ref, out_ref = refs[pos:pos + 4]
    out_ref[...] = _layer_norm_rows(DN_ALPHA * x_ref[...] + y, g_ref[...], b_ref[...])


def _proj_ln(o_list, lse_list, w_bf16, bias, x, ln_g, ln_b, tm=256):
    n = x.shape[0]
    k = w_bf16.shape[0]
    row = lambda i: (i, 0)
    fixed = lambda i: (0, 0)
    args = list(o_list) + list(lse_list) + [w_bf16]
    specs = [pl.BlockSpec((tm, k), row)] * (len(o_list) + len(lse_list))
    specs.append(pl.BlockSpec((k, D_MODEL), fixed))
    if bias is not None:
        args.append(bias)
        specs.append(pl.BlockSpec((1, D_MODEL), fixed))
    args += [x, ln_g, ln_b]
    specs += [pl.BlockSpec((tm, D_MODEL), row), pl.BlockSpec((1, D_MODEL), fixed),
              pl.BlockSpec((1, D_MODEL), fixed)]
    return pl.pallas_call(
        functools.partial(_proj_ln_kernel, n_merge=len(lse_list), has_bias=bias is not None),
        grid=(n // tm,),
        in_specs=specs,
        out_specs=pl.BlockSpec((tm, D_MODEL), row),
        out_shape=jax.ShapeDtypeStruct((n, D_MODEL), _F32),
        compiler_params=_cparams("parallel"),
        name="proj_ln",
    )(*args)


N_KEY_BLOCKS = PEER_HEADS * 2
HALF_KEY_DIM = PEER_KEY_DIM // 2


def _peer_score_kernel(x_ref, wq_ref, keys_ref, sc_ref):
    q = jnp.dot(x_ref[...].astype(_BF16), wq_ref[...], preferred_element_type=_F32).astype(_BF16)
    for blk in range(N_KEY_BLOCKS):
        sc_ref[blk * PEER_N_KEYS:(blk + 1) * PEER_N_KEYS, :] = lax.dot_general(
            keys_ref[blk], q[:, blk * HALF_KEY_DIM:(blk + 1) * HALF_KEY_DIM],
            (((1,), (1,)), ((), ())), preferred_element_type=_F32)


def _peer_scores(x, wq_bf16, keys_bf16):
    n = x.shape[0]
    tm = TOKEN_TILE
    return pl.pallas_call(
        _peer_score_kernel,
        grid=(n // tm,),
        in_specs=[
            pl.BlockSpec((tm, D_MODEL), lambda i: (i, 0)),
            pl.BlockSpec((D_MODEL, PEER_HEADS * PEER_KEY_DIM), lambda i: (0, 0)),
            pl.BlockSpec((N_KEY_BLOCKS, PEER_N_KEYS, HALF_KEY_DIM), lambda i: (0, 0, 0)),
        ],
        out_specs=pl.BlockSpec((N_KEY_BLOCKS * PEER_N_KEYS, tm), lambda i: (0, i)),
        out_shape=jax.ShapeDtypeStruct((N_KEY_BLOCKS * PEER_N_KEYS, n), _F32),
        compiler_params=_cparams("parallel"),
        name="peer_scores",
    )(x, wq_bf16, keys_bf16)


TOPK_TOKENS = 128
SUBLANES = 8
N_CAND_GROUPS = 10


def _top1_rows(vals, ids, big):
    m = jnp.max(vals, axis=0, keepdims=True)
    sel = jnp.min(jnp.where(vals == m, ids, big), axis=0, keepdims=True)
    return m, sel, ids == sel


def _peer_topk_kernel(sc_ref, e_ref, gate_ref, top_s, top_i, sel_e, sel_g):
    tt = sc_ref.shape[1]
    key_id = lax.broadcasted_iota(jnp.int32, (PEER_N_KEYS, tt), 0).astype(_F32)

    def stage1(blk, carry):
        s = sc_ref[pl.ds(pl.multiple_of(blk * PEER_N_KEYS, PEER_N_KEYS), PEER_N_KEYS), :]
        for k in range(PEER_TOPK):
            m, sel, onehot = _top1_rows(s, key_id, float(PEER_N_KEYS))
            top_s[blk, k:k + 1, :] = m
            top_i[blk, k:k + 1, :] = sel
            s = jnp.where(onehot, _NEG_INF, s)
        return carry

    lax.fori_loop(0, N_KEY_BLOCKS, stage1, 0)

    sub = lax.broadcasted_iota(jnp.int32, (SUBLANES, tt), 0).astype(_F32)
    flat_id = jnp.concatenate(
        [sub, sub + SUBLANES] + [sub + a * PEER_TOPK for a in range(1, SUBLANES)]
        + [(sub + SUBLANES) * PEER_TOPK], axis=0)

    def bcast(row):
        return jnp.broadcast_to(row, (SUBLANES, tt))

    def stage2(h, carry):
        s0, s1 = top_s[2 * h], top_s[2 * h + 1]
        i0, i1 = top_i[2 * h] * PEER_N_KEYS, top_i[2 * h + 1]
        lo_s, hi_s, lo_i, hi_i = s1[:SUBLANES], s1[SUBLANES:], i1[:SUBLANES], i1[SUBLANES:]
        cand = jnp.concatenate(
            [bcast(s0[0:1]) + lo_s, bcast(s0[0:1]) + hi_s]
            + [bcast(s0[a:a + 1]) + lo_s for a in range(1, SUBLANES)]
            + [s0[SUBLANES:] + bcast(s1[0:1])], axis=0)
        cand_e = jnp.concatenate(
            [bcast(i0[0:1]) + lo_i, bcast(i0[0:1]) + hi_i]
            + [bcast(i0[a:a + 1]) + lo_i for a in range(1, SUBLANES)]
            + [i0[SUBLANES:] + bcast(i1[0:1])], axis=0)
        best = []
        for k in range(PEER_TOPK):
            m, _, onehot = _top1_rows(cand, flat_id, float(PEER_TOPK * PEER_TOPK))
            best.append(m)
            sel_e[h, k:k + 1, :] = jnp.max(jnp.where(onehot, cand_e, -1.0), axis=0, keepdims=True)
            cand = jnp.where(onehot, _NEG_INF, cand)
        ex = [jnp.exp(b - best[0]) for b in best]
        den = functools.reduce(jnp.add, ex)
        for k in range(PEER_TOPK):
            sel_g[h, k:k + 1, :] = ex[k] / den
        return carry

    lax.fori_loop(0, PEER_HEADS, stage2, 0)
    n_sel = PEER_HEADS * PEER_TOPK
    e_ref[...] = sel_e[...].reshape(n_sel, tt).T.astype(jnp.int32)
    gate_ref[...] = sel_g[...].reshape(n_sel, tt).T


def _peer_topk(sc_t):
    n = sc_t.shape[1]
    tt = TOPK_TOKENS
    n_sel = PEER_HEADS * PEER_TOPK
    return pl.pallas_call(
        _peer_topk_kernel,
        grid=(n // tt,),
        in_specs=[pl.BlockSpec((N_KEY_BLOCKS * PEER_N_KEYS, tt), lambda i: (0, i))],
        out_specs=[pl.BlockSpec((tt, n_sel), lambda i: (i, 0)),
                   pl.BlockSpec((tt, n_sel), lambda i: (i, 0))],
        out_shape=[jax.ShapeDtypeStruct((n, n_sel), jnp.int32),
                   jax.ShapeDtypeStruct((n, n_sel), _F32)],
        scratch_shapes=[pltpu.VMEM((N_KEY_BLOCKS, PEER_TOPK, tt), _F32),
                        pltpu.VMEM((N_KEY_BLOCKS, PEER_TOPK, tt), _F32),
                        pltpu.VMEM((PEER_HEADS, PEER_TOPK, tt), _F32),
                        pltpu.VMEM((PEER_HEADS, PEER_TOPK, tt), _F32)],
        compiler_params=_cparams("parallel"),
        name="peer_topk",
    )(sc_t)


GATE_TOKENS = 64


def _peer_gate_kernel(e_ref, gate_ref, g_ref):
    n_sel = e_ref.shape[2]
    row_id = lax.broadcasted_iota(jnp.int32, (PEER_N_KEYS, n_sel), 0)

    def body(n, carry):
        e = e_ref[n]
        gate = gate_ref[n]
        hi = gate.astype(_BF16).astype(_F32)
        lo = gate - hi
        is_i = row_id == (e >> 7)
        lhs = jnp.concatenate([jnp.where(is_i, hi, 0.0), jnp.where(is_i, lo, 0.0)], axis=0)
        rhs = jnp.where(row_id == (e & (PEER_N_KEYS - 1)), 1.0, 0.0)
        res = lax.dot_general(lhs.astype(_BF16), rhs.astype(_BF16), (((1,), (1,)), ((), ())),
                              preferred_element_type=_F32)
        g_ref[pl.ds(n, 1), :, :] = (res[:PEER_N_KEYS] + res[PEER_N_KEYS:])[None]
        return carry

    lax.fori_loop(0, e_ref.shape[0], body, 0, unroll=4)


def _peer_gates(expert, gate):
    n, n_sel = expert.shape
    tb = GATE_TOKENS
    expert = expert.reshape(n, 1, n_sel)
    gate = gate.reshape(n, 1, n_sel)
    return pl.pallas_call(
        _peer_gate_kernel,
        grid=(n // tb,),
        in_specs=[pl.BlockSpec((tb, 1, n_sel), lambda i: (i, 0, 0)),
                  pl.BlockSpec((tb, 1, n_sel), lambda i: (i, 0, 0))],
        out_specs=pl.BlockSpec((tb, PEER_N_KEYS, PEER_N_KEYS), lambda i: (i, 0, 0)),
        out_shape=jax.ShapeDtypeStruct((n, PEER_N_KEYS, PEER_N_KEYS), _F32),
        compiler_params=_cparams("parallel"),
        name="peer_gates",
    )(expert, gate)


EXPERT_CHUNK = 512
_SQRT_HALF = math.sqrt(0.5)


def _peer_main_kernel(x_ref, ut_ref, g_ref, v_ref, lng_ref, lnb_ref, o_ref, xb_scr, acc_scr):
    c = pl.program_id(1)

    @pl.when(c == 0)
    def _():
        xb_scr[...] = x_ref[...].astype(_BF16)
        acc_scr[...] = jnp.zeros_like(acc_scr)

    a = jnp.dot(xb_scr[...], ut_ref[...], preferred_element_type=_F32)
    gelu = 0.5 * a * (1.0 + lax.erf(a * _SQRT_HALF))
    h = (g_ref[...] * gelu).astype(_BF16)
    acc_scr[...] += jnp.dot(h, v_ref[...], preferred_element_type=_F32)

    @pl.when(c == pl.num_programs(1) - 1)
    def _():
        o_ref[...] = _layer_norm_rows(DN_ALPHA * x_ref[...] + acc_scr[...], lng_ref[...], lnb_ref[...])


def _peer_main(x, ut_bf16, gates, v_bf16, ln_g, ln_b):
    n = x.shape[0]
    tm, ce = TOKEN_TILE, EXPERT_CHUNK
    return pl.pallas_call(
        _peer_main_kernel,
        grid=(n // tm, PEER_N_EXPERTS // ce),
        in_specs=[
            pl.BlockSpec((tm, D_MODEL), lambda i, c: (i, 0)),
            pl.BlockSpec((D_MODEL, ce), lambda i, c: (0, c)),
            pl.BlockSpec((tm, ce), lambda i, c: (i, c)),
            pl.BlockSpec((ce, D_MODEL), lambda i, c: (c, 0)),
            pl.BlockSpec((1, D_MODEL), lambda i, c: (0, 0)),
            pl.BlockSpec((1, D_MODEL), lambda i, c: (0, 0)),
        ],
        out_specs=pl.BlockSpec((tm, D_MODEL), lambda i, c: (i, 0)),
        out_shape=jax.ShapeDtypeStruct((n, D_MODEL), _F32),
        scratch_shapes=[pltpu.VMEM((tm, D_MODEL), _BF16), pltpu.VMEM((tm, D_MODEL), _F32)],
        compiler_params=_cparams("parallel", "arbitrary"),
        name="peer_main",
    )(x, ut_bf16, gates, v_bf16, ln_g, ln_b)


def _peer_ffn_ln(x, wq_bf16, keys_bf16, ut_bf16, v_bf16, ln_g, ln_b):
    sc_t = _peer_scores(x, wq_bf16, keys_bf16)
    expert, gate = _peer_topk(sc_t)
    gates = _peer_gates(expert, gate).reshape(x.shape[0], PEER_N_EXPERTS)
    return _peer_main(x, ut_bf16, gates, v_bf16, ln_g, ln_b)


def _softmax_lse(s, valid, sink):
    s = jnp.where(valid, s, -jnp.inf)
    m = s.max(-1, keepdims=True)
    if sink is not None:
        m = jnp.maximum(m, sink)
    p = jnp.exp(s - m)
    den = p.sum(-1, keepdims=True)
    if sink is not None:
        den = den + jnp.exp(sink - m)
    return p / den, (m + jnp.log(den))[..., 0]


def _a_window_step(q, k_new, v_new, cache_kv, sinks):
    db, t_len = q.shape[:2]
    length = cache_kv.shape[1]
    g = A_HEADS // A_KV_HEADS
    k = jnp.concatenate([cache_kv[:, :, 0], k_new], 1)
    v = jnp.concatenate([cache_kv[:, :, 1], v_new], 1)
    qg = q.reshape(db, t_len, A_KV_HEADS, g, HEAD_DIM)
    s = jnp.einsum('btkgd,bskd->bkgts', qg, k, preferred_element_type=_F32)
    dist = (length + jnp.arange(t_len))[:, None] - jnp.arange(length + t_len)[None, :]
    valid = (dist >= 0) & (dist <= A_WINDOW)
    p, _ = _softmax_lse(s, valid, sinks.astype(_F32).reshape(1, A_KV_HEADS, g, 1, 1))
    o = jnp.einsum('bkgts,bskd->btkgd', p, v)
    return o.reshape(db, t_len, A_HEADS * HEAD_DIM)


def _dilated_step(q, k_new, v_new, cache_kv, window, dilation):
    t_len = q.shape[1]
    length = cache_kv.shape[1]
    k = jnp.concatenate([cache_kv[:, :, 0], k_new], 1)
    v = jnp.concatenate([cache_kv[:, :, 1], v_new], 1)
    n_k = window // dilation + 1
    idx = (length + jnp.arange(t_len))[:, None] - dilation * jnp.arange(n_k)[None, :]
    valid = idx >= 0
    idx_c = jnp.maximum(idx, 0)
    kg = k[:, idx_c]
    vg = v[:, idx_c]
    s = jnp.einsum('bthd,btkhd->bthk', q, kg, preferred_element_type=_F32)
    p, lse = _softmax_lse(s, valid[None, :, None, :], None)
    o = jnp.einsum('bthk,btkhd->bthd', p, vg)
    return o, lse


def _rope_tables(pos):
    half = HEAD_DIM // 2
    inv = ROPE_THETA ** (-jnp.arange(half, dtype=_F32) / half)
    ang = pos.astype(_F32)[:, None] * inv[None, :]
    cos, sin = jnp.cos(ang), jnp.sin(ang)
    cos = jnp.tile(cos, (1, LANES // half))
    sin = jnp.tile(jnp.concatenate([-sin, sin], axis=1), (1, LANES // HEAD_DIM))
    return cos, sin


def kernel(x_prompt, x_sample, cache_a_kv, cache_b1_kv, cache_b2_kv, cache_b3_kv, w_qkv_a, b_qkv_a, sinks_a, w_o_a, b_o_a, w_qkv_b, w_o_b, ln_mix_g, ln_mix_b, ln_ffn_g, ln_ffn_b, peer_w_query, peer_sub_keys, peer_u, peer_v):
    n_batch, seq = x_prompt.shape[:2]
    db, t_len = x_sample.shape[:2]
    xp = x_prompt.reshape(n_batch * seq, D_MODEL)
    xs = x_sample.reshape(db * t_len, D_MODEL)
    cos_p, sin_p = _rope_tables(jnp.tile(jnp.arange(seq, dtype=jnp.int32), n_batch))
    cos_s, sin_s = _rope_tables(jnp.tile(PAST_LEN + jnp.arange(t_len, dtype=jnp.int32), db))
    scale = HEAD_DIM ** -0.5
    row = lambda v: v.reshape(1, -1)
    caches_b = (cache_b1_kv, cache_b2_kv, cache_b3_kv)
    outs_kv = {}

    for layer in range(DEPTH):
        if layer % 2 == 0:
            i = layer // 2
            nq = A_HEADS * HEAD_DIM
            nk = A_KV_HEADS * HEAD_DIM
            col_scale = jnp.concatenate([jnp.full((nq,), scale, _F32), jnp.ones((2 * nk,), _F32)])
            w = (w_qkv_a[i] * col_scale).astype(_BF16)
            bias = row(b_qkv_a[i] * col_scale)
            qkv_p = _qkv_rope(xp, w, bias, cos_p, sin_p, nq + nk, tn=nk)
            qkv_s = _qkv_rope(xs, w, bias, cos_s, sin_s, nq + nk, tn=nk)
            (o_p,) = _band_attention(
                qkv_p, sinks_a[i], n_batch=n_batch, dil=1, q_blk0=0, k_blk0=nq // LANES,
                v_blk0=(nq + nk) // LANES, q_width=nq // A_KV_HEADS, kv_width=LANES,
                n_groups=A_KV_HEADS, shared_kv=True, row_width=A_QKV, out_width=nq, with_lse=False)
            qs = qkv_s[:, :nq].reshape(db, t_len, A_HEADS, HEAD_DIM)
            ks = qkv_s[:, nq:nq + nk].reshape(db, t_len, A_KV_HEADS, HEAD_DIM)
            vs = qkv_s[:, nq + nk:].reshape(db, t_len, A_KV_HEADS, HEAD_DIM)
            o_s = _a_window_step(qs, ks, vs, cache_a_kv[i], sinks_a[i]).reshape(db * t_len, nq)
            wo = w_o_a[i].astype(_BF16)
            xp = _proj_ln([o_p], [], wo, row(b_o_a[i]), xp, row(ln_mix_g[layer]), row(ln_mix_b[layer]))
            xs = _proj_ln([o_s], [], wo, row(b_o_a[i]), xs, row(ln_mix_g[layer]), row(ln_mix_b[layer]))
            keep = min(A_WINDOW, seq)
            kv_p = qkv_p.reshape(n_batch, seq, A_QKV)[:, seq - keep:, nq:]
            outs_kv.setdefault("a_p", []).append(kv_p.reshape(n_batch, keep, 2, A_KV_HEADS, HEAD_DIM))
            outs_kv.setdefault("a_s", []).append(qkv_s[:, nq:].reshape(db, t_len, 2, A_KV_HEADS, HEAD_DIM))
        else:
            i = layer // 2
            gw = B_GROUP_WIDTH
            col_scale = jnp.concatenate([jnp.full((N_B_GROUPS * gw,), scale, _F32),
                                         jnp.ones((2 * N_B_GROUPS * gw,), _F32)])
            w = (w_qkv_b[i] * col_scale).astype(_BF16)
            bias = jnp.zeros((1, B_QKV), _F32)
            qkv_p = _qkv_rope(xp, w, bias, cos_p, sin_p, 2 * N_B_GROUPS * gw)
            qkv_s = _qkv_rope(xs, w, bias, cos_s, sin_s, 2 * N_B_GROUPS * gw)
            qkv_s5 = qkv_s.reshape(db, t_len, 3, N_B_GROUPS, B_HEADS, HEAD_DIM)
            qkv_p5 = qkv_p.reshape(n_batch, seq, 3, N_B_GROUPS, gw)
            o_ps, l_ps, o_ss, l_ss = [], [], [], []
            for g, (window, dil) in enumerate(B_GROUPS):
                o, l = _band_attention(
                    qkv_p, jnp.zeros((1,), _F32), n_batch=n_batch, dil=dil, q_blk0=g,
                    k_blk0=N_B_GROUPS + g, v_blk0=2 * N_B_GROUPS + g, q_width=gw, kv_width=gw,
                    n_groups=dil, shared_kv=False, row_width=B_QKV, out_width=gw, with_lse=True)
                o_ps.append(o)
                l_ps.append(l)
                o, l = _dilated_step(qkv_s5[:, :, 0, g], qkv_s5[:, :, 1, g], qkv_s5[:, :, 2, g],
                                     caches_b[g][i], window, dil)
                o_ss.append(o.reshape(db * t_len, gw))
                l_ss.append(jnp.repeat(l.reshape(db * t_len, B_HEADS), HEAD_DIM, axis=1))
                keep = min(window, seq)
                kv_p = qkv_p5[:, seq - keep:, 1:, g]
                outs_kv.setdefault("b%d_p" % g, []).append(
                    kv_p.reshape(n_batch, keep, 2, B_HEADS, HEAD_DIM))
                outs_kv.setdefault("b%d_s" % g, []).append(qkv_s5[:, :, 1:, g])
            wo = w_o_b[i].astype(_BF16)
            xp = _proj_ln(o_ps, l_ps, wo, None, xp, row(ln_mix_g[layer]), row(ln_mix_b[layer]))
            xs = _proj_ln(o_ss, l_ss, wo, None, xs, row(ln_mix_g[layer]), row(ln_mix_b[layer]))

        wq = peer_w_query[layer].astype(_BF16)
        keys = peer_sub_keys[layer].reshape(N_KEY_BLOCKS, PEER_N_KEYS, HALF_KEY_DIM).astype(_BF16)
        ut = peer_u[layer].T.astype(_BF16)
        vt = peer_v[layer].astype(_BF16)
        xp = _peer_ffn_ln(xp, wq, keys, ut, vt, row(ln_ffn_g[layer]), row(ln_ffn_b[layer]))
        xs = _peer_ffn_ln(xs, wq, keys, ut, vt, row(ln_ffn_g[layer]), row(ln_ffn_b[layer]))

    return (xp.reshape(n_batch, seq, D_MODEL), xs.reshape(db, t_len, D_MODEL),
            jnp.stack(outs_kv["a_p"]), jnp.stack(outs_kv["a_s"]),
            jnp.stack(outs_kv["b0_p"]), jnp.stack(outs_kv["b0_s"]),
            jnp.stack(outs_kv["b1_p"]), jnp.stack(outs_kv["b1_s"]),
            jnp.stack(outs_kv["b2_p"]), jnp.stack(outs_kv["b2_s"]))
```

```python
import functools
import math

import jax
import jax.numpy as jnp
from jax import lax
from jax.experimental import pallas as pl
from jax.experimental.pallas import tpu as pltpu

D_MODEL = 2048
SEQ = 2048
DEPTH = 2
DEC_SEQ = 4
PAST_LEN = 8192
HEAD_DIM = 64
ROPE_THETA = 10000.0
BLOCK = 128
A_HEADS = D_MODEL // HEAD_DIM
A_KV_HEADS = A_HEADS // 8
A_GROUP = A_HEADS // A_KV_HEADS
A_WINDOW = 128
A_QKV = (A_HEADS + 2 * A_KV_HEADS) * HEAD_DIM
B_GROUPS = ((128, 1), (512, 4), (2048, 16))
N_B_GROUPS = len(B_GROUPS)
B_HEADS = D_MODEL // (2 * HEAD_DIM)
B_GROUP_WIDTH = B_HEADS * HEAD_DIM
B_KEYS = 128
PEER_HEADS = 8
PEER_N_KEYS = 128
PEER_N_EXPERTS = PEER_N_KEYS * PEER_N_KEYS
PEER_TOPK = 16
PEER_KEY_DIM = 256
DN_ALPHA = (2.0 * DEPTH) ** 0.25
LN_EPS = 1e-5

LANES = 128
SUBLANES = 8
TOKEN_TILE = 512
VMEM_LIMIT = 56 * 1024 * 1024

_F32 = jnp.float32
_BF16 = jnp.bfloat16
_NEG_INF = float("-inf")
_NT = (((1,), (1,)), ((), ()))


def _cparams(*sem):
    return pltpu.CompilerParams(dimension_semantics=sem, vmem_limit_bytes=VMEM_LIMIT)


def _rope_store(acc, cos, sin, dst_ref):
    tm = acc.shape[0]
    lane = lax.broadcasted_iota(jnp.int32, (tm, LANES), 1)
    first_half = (lane % HEAD_DIM) < (HEAD_DIM // 2)
    for g in range(acc.shape[1] // LANES):
        blk = acc[:, g * LANES:(g + 1) * LANES]
        partner = jnp.where(first_half, pltpu.roll(blk, LANES - HEAD_DIM // 2, 1),
                            pltpu.roll(blk, HEAD_DIM // 2, 1))
        dst_ref[:, g * LANES:(g + 1) * LANES] = blk * cos + partner * sin


def _qkv_rope_kernel(x_ref, w_ref, b_ref, cos_ref, sin_ref, o_ref, *, n_rope_tiles):
    j = pl.program_id(1)
    acc = jnp.dot(x_ref[...].astype(_BF16), w_ref[...], preferred_element_type=_F32) + b_ref[...]

    @pl.when(j < n_rope_tiles)
    def _():
        _rope_store(acc, cos_ref[...], sin_ref[...], o_ref)

    @pl.when(j >= n_rope_tiles)
    def _():
        o_ref[...] = acc


def _qkv_rope(x, w_bf16, bias, cos, sin, n_rope_cols, tn):
    n, k = x.shape
    ncols = w_bf16.shape[1]
    tm = TOKEN_TILE
    return pl.pallas_call(
        functools.partial(_qkv_rope_kernel, n_rope_tiles=n_rope_cols // tn),
        grid=(n // tm, ncols // tn),
        in_specs=[
            pl.BlockSpec((tm, k), lambda i, j: (i, 0)),
            pl.BlockSpec((k, tn), lambda i, j: (0, j)),
            pl.BlockSpec((1, tn), lambda i, j: (0, j)),
            pl.BlockSpec((tm, LANES), lambda i, j: (i, 0)),
            pl.BlockSpec((tm, LANES), lambda i, j: (i, 0)),
        ],
        out_specs=pl.BlockSpec((tm, tn), lambda i, j: (i, j)),
        out_shape=jax.ShapeDtypeStruct((n, ncols), _F32),
        compiler_params=_cparams("parallel", "arbitrary"),
        name="qkv_rope",
    )(x, w_bf16, bias, cos, sin)


def _residue_major(src_ref, dst_ref, dil):
    n = src_ref.shape[1] // dil
    for g in range(src_ref.shape[0]):
        for r in range(dil):
            dst_ref[r * n:(r + 1) * n, g * LANES:(g + 1) * LANES] = src_ref[g, pl.ds(r, n, stride=dil), :]


def _qkv_group_kernel(x_ref, w_ref, cos_ref, sin_ref, *refs, dil):
    j = pl.program_id(1)
    acc = jnp.dot(x_ref[...].astype(_BF16), w_ref[...], preferred_element_type=_F32)
    if dil == 1:
        q_ref, kv_ref = refs
    else:
        q_ref, kv_ref, kvr_ref, scr = refs

    def to_scratch(val_ref):
        for g in range(scr.shape[0]):
            scr[g] = val_ref[:, g * LANES:(g + 1) * LANES]

    @pl.when(j == 0)
    def _():
        if dil == 1:
            _rope_store(acc, cos_ref[...], sin_ref[...], q_ref)
        else:
            _rope_store(acc, cos_ref[...], sin_ref[...], kv_ref)
            to_scratch(kv_ref)
            _residue_major(scr, q_ref, dil)

    @pl.when(j == 1)
    def _():
        _rope_store(acc, cos_ref[...], sin_ref[...], kv_ref)
        if dil > 1:
            to_scratch(kv_ref)
            _residue_major(scr, kvr_ref, dil)

    @pl.when(j == 2)
    def _():
        kv_ref[...] = acc
        if dil > 1:
            to_scratch(kv_ref)
            _residue_major(scr, kvr_ref, dil)


def _qkv_group(x, w_bf16, cos, sin, dil):
    n, k = x.shape
    tm, tn = TOKEN_TILE, B_GROUP_WIDTH
    q_spec = pl.BlockSpec((tm, tn), lambda i, j: (i, 0))
    kv_spec = pl.BlockSpec((tm, tn), lambda i, j: (i, jnp.maximum(j - 1, 0)))
    q_sds = jax.ShapeDtypeStruct((n, tn), _F32)
    kv_sds = jax.ShapeDtypeStruct((n, 2 * tn), _F32)
    return pl.pallas_call(
        functools.partial(_qkv_group_kernel, dil=dil),
        grid=(n // tm, 3),
        in_specs=[
            pl.BlockSpec((tm, k), lambda i, j: (i, 0)),
            pl.BlockSpec((k, tn), lambda i, j: (0, j)),
            pl.BlockSpec((tm, LANES), lambda i, j: (i, 0)),
            pl.BlockSpec((tm, LANES), lambda i, j: (i, 0)),
        ],
        out_specs=[q_spec, kv_spec] + ([kv_spec] if dil > 1 else []),
        out_shape=[q_sds, kv_sds] + ([kv_sds] if dil > 1 else []),
        scratch_shapes=[pltpu.VMEM((tn // LANES, tm, LANES), _F32)] if dil > 1 else [],
        compiler_params=_cparams("parallel", "arbitrary"),
        name="qkv_group_d%d" % dil,
    )(x, w_bf16, cos, sin)


def _band_attn_kernel(sink_ref, *refs, n_pieces, has_prev, n_pairs, shared_kv, has_sink, with_lse):
    q_refs = refs[:n_pieces]
    kc_refs = refs[n_pieces:2 * n_pieces]
    vc_refs = refs[2 * n_pieces:3 * n_pieces]
    pos = 3 * n_pieces
    if has_prev:
        kp_ref, vp_ref = refs[pos:pos + 2]
        pos += 2
    o_ref = refs[pos]
    lse_ref = refs[pos + 1] if with_lse else None
    n_keys = 2 * BLOCK if has_prev else BLOCK
    g = pl.program_id(1)
    j = pl.program_id(2)
    qi = lax.broadcasted_iota(jnp.int32, (BLOCK, n_keys), 0)
    kj = lax.broadcasted_iota(jnp.int32, (BLOCK, n_keys), 1)
    if has_prev:
        first_key = jnp.maximum(qi, jnp.where(j > 0, 0, BLOCK))
        valid = (kj >= first_key) & (kj <= qi + BLOCK)
    else:
        valid = kj <= qi
    lane = lax.broadcasted_iota(jnp.int32, (BLOCK, LANES), 1)
    low = lane < HEAD_DIM

    def rows(piece_refs, p):
        parts = [r[:, p * LANES:(p + 1) * LANES] for r in piece_refs]
        return parts[0] if len(parts) == 1 else jnp.concatenate(parts, axis=0)

    def keys(cur_refs, prev_ref, p):
        if shared_kv:
            keep = (lane // HEAD_DIM) == (g % 2)
            cur = cur_refs[0][...]
            cur = jnp.where(keep, cur, pltpu.roll(cur, HEAD_DIM, 1))
            prev = prev_ref[...]
            prev = jnp.where(keep, prev, pltpu.roll(prev, HEAD_DIM, 1))
        else:
            cur = rows(cur_refs, p)
            prev = prev_ref[:, p * LANES:(p + 1) * LANES] if has_prev else None
        return (jnp.concatenate([prev, cur], axis=0) if has_prev else cur).astype(_BF16)

    for p in range(n_pairs):
        if p == 0 or not shared_kv:
            kcat = keys(kc_refs, kp_ref if has_prev else None, p)
            vcat = keys(vc_refs, vp_ref if has_prev else None, p)
        q2 = rows(q_refs, p)
        outs, lses = [], []
        for h in range(2):
            qm = jnp.where(low if h == 0 else ~low, q2, 0.0).astype(_BF16)
            s = lax.dot_general(qm, kcat, _NT, preferred_element_type=_F32)
            s = jnp.where(valid, s, _NEG_INF)
            m = jnp.max(s, axis=-1, keepdims=True)
            if has_sink:
                sink = sink_ref[g * (2 * n_pairs) + 2 * p + h]
                m = jnp.maximum(m, sink)
            e = jnp.exp(s - m)
            den = jnp.sum(e, axis=-1, keepdims=True)
            if has_sink:
                den = den + jnp.exp(sink - m)
            pn = (e / den).astype(_BF16)
            outs.append(jnp.dot(pn, vcat, preferred_element_type=_F32))
            lses.append(m + jnp.log(den))
        o_ref[:, p * LANES:(p + 1) * LANES] = jnp.where(low, outs[0], outs[1])
        if with_lse:
            lse_ref[:, p * LANES:(p + 1) * LANES] = jnp.where(low, lses[0], lses[1])


def _band_attention_a(qkv, sinks, n_batch):
    nb = SEQ // BLOCK
    nq = A_HEADS * HEAD_DIM
    q_width = A_GROUP * HEAD_DIM
    k_blk0 = nq // LANES
    v_blk0 = k_blk0 + A_KV_HEADS * HEAD_DIM // LANES
    prev = lambda b, j: b * nb + jnp.maximum(j - 1, 0)
    (o,) = pl.pallas_call(
        functools.partial(_band_attn_kernel, n_pieces=1, has_prev=True, n_pairs=q_width // LANES,
                          shared_kv=True, has_sink=True, with_lse=False),
        grid_spec=pltpu.PrefetchScalarGridSpec(
            num_scalar_prefetch=1,
            grid=(n_batch, A_KV_HEADS, nb),
            in_specs=[
                pl.BlockSpec((BLOCK, q_width), lambda b, g, j, s: (b * nb + j, g)),
                pl.BlockSpec((BLOCK, LANES), lambda b, g, j, s: (b * nb + j, k_blk0 + g // 2)),
                pl.BlockSpec((BLOCK, LANES), lambda b, g, j, s: (b * nb + j, v_blk0 + g // 2)),
                pl.BlockSpec((BLOCK, LANES), lambda b, g, j, s: (prev(b, j), k_blk0 + g // 2)),
                pl.BlockSpec((BLOCK, LANES), lambda b, g, j, s: (prev(b, j), v_blk0 + g // 2)),
            ],
            out_specs=[pl.BlockSpec((BLOCK, q_width), lambda b, g, j, s: (b * nb + j, g))],
        ),
        out_shape=[jax.ShapeDtypeStruct((n_batch * SEQ, nq), _F32)],
        compiler_params=_cparams("parallel", "parallel", "arbitrary"),
        name="band_attn_a",
    )(sinks, qkv, qkv, qkv, qkv, qkv)
    return o


def _band_attention_b(q, kv, dil, n_batch):
    gw = B_GROUP_WIDTH
    nb = SEQ // dil // BLOCK
    per_tile = TOKEN_TILE // dil
    n_pieces = max(1, BLOCK // per_tile)
    piece = BLOCK // n_pieces
    tiles_per_seq = SEQ // TOKEN_TILE
    blocks_per_tile = TOKEN_TILE // piece

    def row_blk(b, r, j, c):
        if n_pieces == 1:
            tiles_per_blk = per_tile // BLOCK
            tile = b * tiles_per_seq + j // tiles_per_blk
            return tile * blocks_per_tile + r * tiles_per_blk + j % tiles_per_blk
        tile = b * tiles_per_seq + j * n_pieces + c
        return tile * blocks_per_tile + r

    def spec(col, c, prev=False):
        if prev:
            return pl.BlockSpec((piece, gw), lambda b, r, j, s: (row_blk(b, r, jnp.maximum(j - 1, 0), c), col))
        return pl.BlockSpec((piece, gw), lambda b, r, j, s: (row_blk(b, r, j, c), col))

    has_prev = nb > 1
    in_specs = ([spec(0, c) for c in range(n_pieces)] + [spec(0, c) for c in range(n_pieces)]
                + [spec(1, c) for c in range(n_pieces)])
    args = [q] * n_pieces + [kv] * (2 * n_pieces)
    if has_prev:
        in_specs += [spec(0, 0, prev=True), spec(1, 0, prev=True)]
        args += [kv, kv]
    o_sds = jax.ShapeDtypeStruct((n_batch * SEQ, gw), _F32)
    o_spec = pl.BlockSpec((BLOCK, gw), lambda b, r, j, s: ((b * dil + r) * nb + j, 0))
    o, lse = pl.pallas_call(
        functools.partial(_band_attn_kernel, n_pieces=n_pieces, has_prev=has_prev,
                          n_pairs=gw // LANES, shared_kv=False, has_sink=False, with_lse=True),
        grid_spec=pltpu.PrefetchScalarGridSpec(
            num_scalar_prefetch=1,
            grid=(n_batch, dil, nb),
            in_specs=in_specs,
            out_specs=[o_spec, o_spec],
        ),
        out_shape=[o_sds, o_sds],
        compiler_params=_cparams("parallel", "parallel", "arbitrary"),
        name="band_attn_d%d" % dil,
    )(jnp.zeros((1,), _F32), *args)
    if dil > 1:
        o, lse = _token_major(o, dil, n_batch), _token_major(lse, dil, n_batch)
    return o, lse


def _token_major(x, dil, n_batch):
    w = x.shape[1]
    x = x.reshape(n_batch, dil, SEQ // dil, w)
    return jnp.swapaxes(x, 1, 2).reshape(n_batch * SEQ, w)


def _masked_attention(q, kmat, vmat, valid, sink=None):
    s = lax.dot_general(q.astype(_BF16), kmat.astype(_BF16), _NT, preferred_element_type=_F32)
    s = jnp.where(valid, s, _NEG_INF)
    m = jnp.max(s, axis=-1, keepdims=True)
    if sink is not None:
        m = jnp.maximum(m, sink)
    e = jnp.exp(s - m)
    den = jnp.sum(e, axis=-1, keepdims=True)
    if sink is not None:
        den = den + jnp.exp(sink - m)
    pn = (e / den).astype(_BF16)
    o = jnp.dot(pn, vmat.astype(_BF16), preferred_element_type=_F32)
    return o, m + jnp.log(den)


STEP_B_KEY_ROWS = 17 * LANES


def _step_b_kernel(q_ref, kvn_ref, cache_ref, o_ref, lse_ref, *, dil):
    heads = B_HEADS
    cache_rows = B_KEYS * heads

    def pad_rows(x):
        return jnp.concatenate([x, jnp.zeros((STEP_B_KEY_ROWS - x.shape[0], HEAD_DIM), _F32)], axis=0)

    if dil == 1:
        n_q = DEC_SEQ * heads
        q = q_ref[0].reshape(n_q, HEAD_DIM)
        kmat = pad_rows(jnp.concatenate([cache_ref[0, :, 0].reshape(cache_rows, HEAD_DIM),
                                         kvn_ref[0, :, 0].reshape(n_q, HEAD_DIM)], axis=0))
        vmat = pad_rows(jnp.concatenate([cache_ref[0, :, 1].reshape(cache_rows, HEAD_DIM),
                                         kvn_ref[0, :, 1].reshape(n_q, HEAD_DIM)], axis=0))
        row = lax.broadcasted_iota(jnp.int32, (n_q, STEP_B_KEY_ROWS), 0)
        col = lax.broadcasted_iota(jnp.int32, (n_q, STEP_B_KEY_ROWS), 1)
        t, key = row // heads, col // heads
        valid = ((row % heads) == (col % heads)) & (key >= t) & (key <= B_KEYS + t)
        o, lse = _masked_attention(q, kmat, vmat, valid)
        o_ref[0] = o.reshape(DEC_SEQ, heads, HEAD_DIM)
        lse_ref[0] = jnp.broadcast_to(lse, (n_q, HEAD_DIM)).reshape(DEC_SEQ, heads, HEAD_DIM)
    else:
        row = lax.broadcasted_iota(jnp.int32, (heads, STEP_B_KEY_ROWS), 0)
        col = lax.broadcasted_iota(jnp.int32, (heads, STEP_B_KEY_ROWS), 1)
        valid = (row == (col % heads)) & (col < cache_rows + heads)
        for t in range(DEC_SEQ):
            kmat = pad_rows(jnp.concatenate(
                [cache_ref[0, :, t, 0].reshape(cache_rows, HEAD_DIM), kvn_ref[0, t, 0]], axis=0))
            vmat = pad_rows(jnp.concatenate(
                [cache_ref[0, :, t, 1].reshape(cache_rows, HEAD_DIM), kvn_ref[0, t, 1]], axis=0))
            o, lse = _masked_attention(q_ref[0, t], kmat, vmat, valid)
            o_ref[0, t] = o
            lse_ref[0, t] = jnp.broadcast_to(lse, (heads, HEAD_DIM))


def _step_b(q, kv_new, cache, dil):
    db = q.shape[0]
    tail = (2, B_HEADS, HEAD_DIM)
    if dil == 1:
        cache_spec = pl.BlockSpec((1, B_KEYS) + tail, lambda b: (b, 0, 0, 0, 0))
    else:
        cache = cache.reshape((db, B_KEYS, dil) + tail)
        cache_spec = pl.BlockSpec((1, B_KEYS, DEC_SEQ) + tail, lambda b: (b, 0, 0, 0, 0, 0))
    o_spec = pl.BlockSpec((1, DEC_SEQ, B_HEADS, HEAD_DIM), lambda b: (b, 0, 0, 0))
    o_sds = jax.ShapeDtypeStruct((db, DEC_SEQ, B_HEADS, HEAD_DIM), _F32)
    return pl.pallas_call(
        functools.partial(_step_b_kernel, dil=dil),
        grid=(db,),
        in_specs=[o_spec, pl.BlockSpec((1, DEC_SEQ) + tail, lambda b: (b, 0, 0, 0, 0)), cache_spec],
        out_specs=[o_spec, o_spec],
        out_shape=[o_sds, o_sds],
        compiler_params=_cparams("parallel"),
        name="step_b_d%d" % dil,
    )(q, kv_new, cache)


STEP_A_SAMPLES = 8
STEP_A_NEW_ROWS = 8
STEP_A_KEY_ROWS = 2 * LANES


def _step_a_kernel(q_ref, kn_ref, vn_ref, cache_ref, sink_ref, o_ref):
    n_q = DEC_SEQ * A_GROUP
    row = lax.broadcasted_iota(jnp.int32, (n_q, STEP_A_KEY_ROWS), 0)
    col = lax.broadcasted_iota(jnp.int32, (n_q, STEP_A_KEY_ROWS), 1)
    t = row // A_GROUP
    valid = (col >= t) & (col <= A_WINDOW + t)
    zeros = jnp.zeros((STEP_A_KEY_ROWS - A_WINDOW - STEP_A_NEW_ROWS, HEAD_DIM), _F32)
    for b in range(STEP_A_SAMPLES):
        for g in range(A_KV_HEADS):
            kmat = jnp.concatenate([cache_ref[b, 0, g], kn_ref[b, g], zeros], axis=0)
            vmat = jnp.concatenate([cache_ref[b, 1, g], vn_ref[b, g], zeros], axis=0)
            o, _ = _masked_attention(q_ref[b, g], kmat, vmat, valid, sink_ref[g])
            o_ref[b, g] = o


def _step_a(qkv_s, cache, sinks, db):
    nq = A_HEADS * HEAD_DIM
    nk = A_KV_HEADS * HEAD_DIM
    q = qkv_s[:, :nq].reshape(db, DEC_SEQ, A_KV_HEADS, A_GROUP, HEAD_DIM)
    q = q.transpose(0, 2, 1, 3, 4).reshape(db, A_KV_HEADS, DEC_SEQ * A_GROUP, HEAD_DIM)
    pad = [(0, 0), (0, 0), (0, STEP_A_NEW_ROWS - DEC_SEQ), (0, 0)]
    kn = jnp.pad(qkv_s[:, nq:nq + nk].reshape(db, DEC_SEQ, A_KV_HEADS, HEAD_DIM).transpose(0, 2, 1, 3), pad)
    vn = jnp.pad(qkv_s[:, nq + nk:].reshape(db, DEC_SEQ, A_KV_HEADS, HEAD_DIM).transpose(0, 2, 1, 3), pad)
    cache = cache.transpose(0, 2, 3, 1, 4)
    sink_rows = jnp.tile(sinks.reshape(A_KV_HEADS, 1, A_GROUP), (1, DEC_SEQ, 1))
    sink_rows = sink_rows.reshape(A_KV_HEADS, DEC_SEQ * A_GROUP, 1)
    bb = STEP_A_SAMPLES
    q_spec = pl.BlockSpec((bb, A_KV_HEADS, DEC_SEQ * A_GROUP, HEAD_DIM), lambda i: (i, 0, 0, 0))
    n_spec = pl.BlockSpec((bb, A_KV_HEADS, STEP_A_NEW_ROWS, HEAD_DIM), lambda i: (i, 0, 0, 0))
    o = pl.pallas_call(
        _step_a_kernel,
        grid=(db // bb,),
        in_specs=[q_spec, n_spec, n_spec,
                  pl.BlockSpec((bb, 2, A_KV_HEADS, A_WINDOW, HEAD_DIM), lambda i: (i, 0, 0, 0, 0)),
                  pl.BlockSpec((A_KV_HEADS, DEC_SEQ * A_GROUP, 1), lambda i: (0, 0, 0))],
        out_specs=q_spec,
        out_shape=jax.ShapeDtypeStruct(q.shape, _F32),
        compiler_params=_cparams("parallel"),
        name="step_a",
    )(q, kn, vn, cache, sink_rows)
    o = o.reshape(db, A_KV_HEADS, DEC_SEQ, A_GROUP, HEAD_DIM).transpose(0, 2, 1, 3, 4)
    return o.reshape(db * DEC_SEQ, nq)


def _layer_norm_rows(z, g, b):
    mu = jnp.mean(z, axis=-1, keepdims=True)
    zc = z - mu
    var = jnp.mean(zc * zc, axis=-1, keepdims=True)
    return zc * lax.rsqrt(var + LN_EPS) * g + b


def _proj_ln_kernel(*refs, n_merge, has_bias):
    if n_merge:
        o_refs = refs[:n_merge]
        l_refs = refs[n_merge:2 * n_merge]
        pos = 2 * n_merge
        ls = [r[...] for r in l_refs]
        m = functools.reduce(jnp.maximum, ls)
        es = [jnp.exp(l - m) for l in ls]
        den = functools.reduce(jnp.add, es)
        o = functools.reduce(jnp.add, [(e / den) * r[...] for e, r in zip(es, o_refs)])
    else:
        o = refs[0][...]
        pos = 1
    w_ref = refs[pos]
    pos += 1
    y = jnp.dot(o.astype(_BF16), w_ref[...], preferred_element_type=_F32)
    if has_bias:
        y = y + refs[pos][...]
        pos += 1
    x_ref, g_ref, b_ref, out_ref = refs[pos:pos + 4]
    out_ref[...] = _layer_norm_rows(DN_ALPHA * x_ref[...] + y, g_ref[...], b_ref[...])


def _proj_ln(o_list, lse_list, w_bf16, bias, x, ln_g, ln_b, tm=256):
    n = x.shape[0]
    k = w_bf16.shape[0]
    row = lambda i: (i, 0)
    fixed = lambda i: (0, 0)
    args = list(o_list) + list(lse_list) + [w_bf16]
    specs = [pl.BlockSpec((tm, k), row)] * (len(o_list) + len(lse_list))
    specs.append(pl.BlockSpec((k, D_MODEL), fixed))
    if bias is not None:
        args.append(bias)
        specs.append(pl.BlockSpec((1, D_MODEL), fixed))
    args += [x, ln_g, ln_b]
    specs += [pl.BlockSpec((tm, D_MODEL), row), pl.BlockSpec((1, D_MODEL), fixed),
              pl.BlockSpec((1, D_MODEL), fixed)]
    return pl.pallas_call(
        functools.partial(_proj_ln_kernel, n_merge=len(lse_list), has_bias=bias is not None),
        grid=(n // tm,),
        in_specs=specs,
        out_specs=pl.BlockSpec((tm, D_MODEL), row),
        out_shape=jax.ShapeDtypeStruct((n, D_MODEL), _F32),
        compiler_params=_cparams("parallel"),
        name="proj_ln",
    )(*args)


N_KEY_BLOCKS = PEER_HEADS * 2
HALF_KEY_DIM = PEER_KEY_DIM // 2


def _peer_score_kernel(x_ref, wq_ref, keys_ref, sc_ref):
    q = jnp.dot(x_ref[...].astype(_BF16), wq_ref[...], preferred_element_type=_F32).astype(_BF16)
    for blk in range(N_KEY_BLOCKS):
        sc_ref[blk * PEER_N_KEYS:(blk + 1) * PEER_N_KEYS, :] = lax.dot_general(
            keys_ref[blk], q[:, blk * HALF_KEY_DIM:(blk + 1) * HALF_KEY_DIM], _NT,
            preferred_element_type=_F32)


def _peer_scores(x, wq_bf16, keys_bf16):
    n = x.shape[0]
    tm = TOKEN_TILE
    return pl.pallas_call(
        _peer_score_kernel,
        grid=(n // tm,),
        in_specs=[
            pl.BlockSpec((tm, D_MODEL), lambda i: (i, 0)),
            pl.BlockSpec((D_MODEL, PEER_HEADS * PEER_KEY_DIM), lambda i: (0, 0)),
            pl.BlockSpec((N_KEY_BLOCKS, PEER_N_KEYS, HALF_KEY_DIM), lambda i: (0, 0, 0)),
        ],
        out_specs=pl.BlockSpec((N_KEY_BLOCKS * PEER_N_KEYS, tm), lambda i: (0, i)),
        out_shape=jax.ShapeDtypeStruct((N_KEY_BLOCKS * PEER_N_KEYS, n), _F32),
        compiler_params=_cparams("parallel"),
        name="peer_scores",
    )(x, wq_bf16, keys_bf16)


TOPK_TOKENS = 128


def _top1_rows(vals, ids, big):
    m = jnp.max(vals, axis=0, keepdims=True)
    sel = jnp.min(jnp.where(vals == m, ids, big), axis=0, keepdims=True)
    return m, sel, ids == sel


def _peer_topk_kernel(sc_ref, e_ref, gate_ref, top_s, top_i, sel_e, sel_g):
    tt = sc_ref.shape[1]
    key_id = lax.broadcasted_iota(jnp.int32, (PEER_N_KEYS, tt), 0).astype(_F32)

    def stage1(blk, carry):
        s = sc_ref[pl.ds(pl.multiple_of(blk * PEER_N_KEYS, PEER_N_KEYS), PEER_N_KEYS), :]
        for k in range(PEER_TOPK):
            m, sel, onehot = _top1_rows(s, key_id, float(PEER_N_KEYS))
            top_s[blk, k:k + 1, :] = m
            top_i[blk, k:k + 1, :] = sel
            s = jnp.where(onehot, _NEG_INF, s)
        return carry

    lax.fori_loop(0, N_KEY_BLOCKS, stage1, 0, unroll=2)

    sub = lax.broadcasted_iota(jnp.int32, (SUBLANES, tt), 0).astype(_F32)
    flat_id = jnp.concatenate(
        [sub, sub + SUBLANES] + [sub + a * PEER_TOPK for a in range(1, SUBLANES)]
        + [(sub + SUBLANES) * PEER_TOPK], axis=0)

    def bcast(row):
        return jnp.broadcast_to(row, (SUBLANES, tt))

    def stage2(h, carry):
        s0, s1 = top_s[2 * h], top_s[2 * h + 1]
        i0, i1 = top_i[2 * h] * PEER_N_KEYS, top_i[2 * h + 1]
        lo_s, hi_s, lo_i, hi_i = s1[:SUBLANES], s1[SUBLANES:], i1[:SUBLANES], i1[SUBLANES:]
        cand = jnp.concatenate(
            [bcast(s0[0:1]) + lo_s, bcast(s0[0:1]) + hi_s]
            + [bcast(s0[a:a + 1]) + lo_s for a in range(1, SUBLANES)]
            + [s0[SUBLANES:] + bcast(s1[0:1])], axis=0)
        cand_e = jnp.concatenate(
            [bcast(i0[0:1]) + lo_i, bcast(i0[0:1]) + hi_i]
            + [bcast(i0[a:a + 1]) + lo_i for a in range(1, SUBLANES)]
            + [i0[SUBLANES:] + bcast(i1[0:1])], axis=0)
        best = []
        for k in range(PEER_TOPK):
            m, _, onehot = _top1_rows(cand, flat_id, float(PEER_TOPK * PEER_TOPK))
            best.append(m)
            sel_e[h, k:k + 1, :] = jnp.max(jnp.where(onehot, cand_e, -1.0), axis=0, keepdims=True)
            cand = jnp.where(onehot, _NEG_INF, cand)
        ex = [jnp.exp(b - best[0]) for b in best]
        den = functools.reduce(jnp.add, ex)
        for k in range(PEER_TOPK):
            sel_g[h, k:k + 1, :] = ex[k] / den
        return carry

    lax.fori_loop(0, PEER_HEADS, stage2, 0, unroll=2)
    n_sel = PEER_HEADS * PEER_TOPK
    e_ref[...] = sel_e[...].reshape(n_sel, tt).T.astype(jnp.int32)
    gate_ref[...] = sel_g[...].reshape(n_sel, tt).T


def _peer_topk(sc_t):
    n = sc_t.shape[1]
    tt = TOPK_TOKENS
    n_sel = PEER_HEADS * PEER_TOPK
    return pl.pallas_call(
        _peer_topk_kernel,
        grid=(n // tt,),
        in_specs=[pl.BlockSpec((N_KEY_BLOCKS * PEER_N_KEYS, tt), lambda i: (0, i))],
        out_specs=[pl.BlockSpec((tt, n_sel), lambda i: (i, 0)),
                   pl.BlockSpec((tt, n_sel), lambda i: (i, 0))],
        out_shape=[jax.ShapeDtypeStruct((n, n_sel), jnp.int32),
                   jax.ShapeDtypeStruct((n, n_sel), _F32)],
        scratch_shapes=[pltpu.VMEM((N_KEY_BLOCKS, PEER_TOPK, tt), _F32),
                        pltpu.VMEM((N_KEY_BLOCKS, PEER_TOPK, tt), _F32),
                        pltpu.VMEM((PEER_HEADS, PEER_TOPK, tt), _F32),
                        pltpu.VMEM((PEER_HEADS, PEER_TOPK, tt), _F32)],
        compiler_params=_cparams("parallel"),
        name="peer_topk",
    )(sc_t)


GATE_TOKENS = 64


def _peer_gate_kernel(e_ref, gate_ref, g_ref, scr):
    tb, _, n_sel = e_ref.shape
    row_id = lax.broadcasted_iota(jnp.int32, (PEER_N_KEYS, n_sel), 0)

    def body(n, carry):
        e = e_ref[n]
        gate = gate_ref[n]
        hi = gate.astype(_BF16).astype(_F32)
        lo = gate - hi
        is_i = row_id == (e >> 7)
        lhs = jnp.concatenate([jnp.where(is_i, hi, 0.0), jnp.where(is_i, lo, 0.0)], axis=0)
        rhs = jnp.where(row_id == (e & (PEER_N_KEYS - 1)), 1.0, 0.0)
        res = lax.dot_general(lhs.astype(_BF16), rhs.astype(_BF16), _NT, preferred_element_type=_F32)
        start = pl.multiple_of(n * PEER_N_KEYS, PEER_N_KEYS)
        scr[pl.ds(start, PEER_N_KEYS), :] = res[:PEER_N_KEYS] + res[PEER_N_KEYS:]
        return carry

    lax.fori_loop(0, tb, body, 0, unroll=4)
    for i in range(PEER_N_KEYS):
        g_ref[:, i * PEER_N_KEYS:(i + 1) * PEER_N_KEYS] = scr[pl.ds(i, tb, stride=PEER_N_KEYS), :]


def _peer_gates(expert, gate):
    n, n_sel = expert.shape
    tb = GATE_TOKENS
    expert = expert.reshape(n, 1, n_sel)
    gate = gate.reshape(n, 1, n_sel)
    return pl.pallas_call(
        _peer_gate_kernel,
        grid=(n // tb,),
        in_specs=[pl.BlockSpec((tb, 1, n_sel), lambda i: (i, 0, 0)),
                  pl.BlockSpec((tb, 1, n_sel), lambda i: (i, 0, 0))],
        out_specs=pl.BlockSpec((tb, PEER_N_EXPERTS), lambda i: (i, 0)),
        out_shape=jax.ShapeDtypeStruct((n, PEER_N_EXPERTS), _F32),
        scratch_shapes=[pltpu.VMEM((tb * PEER_N_KEYS, PEER_N_KEYS), _F32)],
        compiler_params=_cparams("parallel"),
        name="peer_gates",
    )(expert, gate)


EXPERT_CHUNK = 512
_SQRT_HALF = math.sqrt(0.5)


def _peer_main_kernel(x_ref, ut_ref, g_ref, v_ref, lng_ref, lnb_ref, o_ref, xb_scr, acc_scr):
    c = pl.program_id(1)

    @pl.when(c == 0)
    def _():
        xb_scr[...] = x_ref[...].astype(_BF16)
        acc_scr[...] = jnp.zeros_like(acc_scr)

    a = jnp.dot(xb_scr[...], ut_ref[...], preferred_element_type=_F32)
    gelu = 0.5 * a * (1.0 + lax.erf(a * _SQRT_HALF))
    h = (g_ref[...] * gelu).astype(_BF16)
    acc_scr[...] += jnp.dot(h, v_ref[...], preferred_element_type=_F32)

    @pl.when(c == pl.num_programs(1) - 1)
    def _():
        o_ref[...] = _layer_norm_rows(DN_ALPHA * x_ref[...] + acc_scr[...], lng_ref[...], lnb_ref[...])


def _peer_main(x, ut_bf16, gates, v_bf16, ln_g, ln_b):
    n = x.shape[0]
    tm, ce = TOKEN_TILE, EXPERT_CHUNK
    return pl.pallas_call(
        _peer_main_kernel,
        grid=(n // tm, PEER_N_EXPERTS // ce),
        in_specs=[
            pl.BlockSpec((tm, D_MODEL), lambda i, c: (i, 0)),
            pl.BlockSpec((D_MODEL, ce), lambda i, c: (0, c)),
            pl.BlockSpec((tm, ce), lambda i, c: (i, c)),
            pl.BlockSpec((ce, D_MODEL), lambda i, c: (c, 0)),
            pl.BlockSpec((1, D_MODEL), lambda i, c: (0, 0)),
            pl.BlockSpec((1, D_MODEL), lambda i, c: (0, 0)),
        ],
        out_specs=pl.BlockSpec((tm, D_MODEL), lambda i, c: (i, 0)),
        out_shape=jax.ShapeDtypeStruct((n, D_MODEL), _F32),
        scratch_shapes=[pltpu.VMEM((tm, D_MODEL), _BF16), pltpu.VMEM((tm, D_MODEL), _F32)],
        compiler_params=_cparams("parallel", "arbitrary"),
        name="peer_main",
    )(x, ut_bf16, gates, v_bf16, ln_g, ln_b)


def _peer_ffn_ln(x, wq_bf16, keys_bf16, ut_bf16, v_bf16, ln_g, ln_b):
    sc_t = _peer_scores(x, wq_bf16, keys_bf16)
    expert, gate = _peer_topk(sc_t)
    gates = _peer_gates(expert, gate)
    return _peer_main(x, ut_bf16, gates, v_bf16, ln_g, ln_b)


def _rope_tables(pos):
    half = HEAD_DIM // 2
    inv = ROPE_THETA ** (-jnp.arange(half, dtype=_F32) / half)
    ang = pos.astype(_F32)[:, None] * inv[None, :]
    cos, sin = jnp.cos(ang), jnp.sin(ang)
    cos = jnp.tile(cos, (1, LANES // half))
    sin = jnp.tile(jnp.concatenate([-sin, sin], axis=1), (1, LANES // HEAD_DIM))
    return cos, sin


def kernel(x_prompt, x_sample, cache_a_kv, cache_b1_kv, cache_b2_kv, cache_b3_kv, w_qkv_a, b_qkv_a, sinks_a, w_o_a, b_o_a, w_qkv_b, w_o_b, ln_mix_g, ln_mix_b, ln_ffn_g, ln_ffn_b, peer_w_query, peer_sub_keys, peer_u, peer_v):
    n_batch, seq = x_prompt.shape[:2]
    db, t_len = x_sample.shape[:2]
    xp = x_prompt.reshape(n_batch * seq, D_MODEL)
    xs = x_sample.reshape(db * t_len, D_MODEL)
    cos_p, sin_p = _rope_tables(jnp.tile(jnp.arange(seq, dtype=jnp.int32), n_batch))
    cos_s, sin_s = _rope_tables(jnp.tile(PAST_LEN + jnp.arange(t_len, dtype=jnp.int32), db))
    scale = HEAD_DIM ** -0.5
    row = lambda v: v.reshape(1, -1)
    caches_b = (cache_b1_kv, cache_b2_kv, cache_b3_kv)
    outs_kv = {}

    for layer in range(DEPTH):
        i = layer // 2
        ln_g, ln_b = row(ln_mix_g[layer]), row(ln_mix_b[layer])
        if layer % 2 == 0:
            nq = A_HEADS * HEAD_DIM
            nk = A_KV_HEADS * HEAD_DIM
            col_scale = jnp.concatenate([jnp.full((nq,), scale, _F32), jnp.ones((2 * nk,), _F32)])
            w = (w_qkv_a[i] * col_scale).astype(_BF16)
            bias = row(b_qkv_a[i] * col_scale)
            qkv_p = _qkv_rope(xp, w, bias, cos_p, sin_p, nq + nk, tn=nk)
            qkv_s = _qkv_rope(xs, w, bias, cos_s, sin_s, nq + nk, tn=nk)
            o_p = _band_attention_a(qkv_p, sinks_a[i], n_batch)
            o_s = _step_a(qkv_s, cache_a_kv[i], sinks_a[i], db)
            wo = w_o_a[i].astype(_BF16)
            xp = _proj_ln([o_p], [], wo, row(b_o_a[i]), xp, ln_g, ln_b)
            xs = _proj_ln([o_s], [], wo, row(b_o_a[i]), xs, ln_g, ln_b)
            keep = min(A_WINDOW, seq)
            kv_p = qkv_p.reshape(n_batch, seq, A_QKV)[:, seq - keep:, nq:]
            outs_kv.setdefault("a_p", []).append(kv_p.reshape(n_batch, keep, 2, A_KV_HEADS, HEAD_DIM))
            outs_kv.setdefault("a_s", []).append(qkv_s[:, nq:].reshape(db, t_len, 2, A_KV_HEADS, HEAD_DIM))
        else:
            gw = B_GROUP_WIDTH
            w3 = w_qkv_b[i].reshape(D_MODEL, 3, N_B_GROUPS, gw)
            o_ps, l_ps, o_ss, l_ss = [], [], [], []
            for g, (window, dil) in enumerate(B_GROUPS):
                w = jnp.concatenate([w3[:, 0, g] * scale, w3[:, 1, g], w3[:, 2, g]], axis=1).astype(_BF16)
                outs = _qkv_group(xp, w, cos_p, sin_p, dil)
                q_p, kv_p, kvr_p = outs if dil > 1 else (outs[0], outs[1], outs[1])
                q_s, kv_s = _qkv_group(xs, w, cos_s, sin_s, 1)
                o, l = _band_attention_b(q_p, kvr_p, dil, n_batch)
                o_ps.append(o)
                l_ps.append(l)
                kv_s5 = kv_s.reshape(db, t_len, 2, B_HEADS, HEAD_DIM)
                o, l = _step_b(q_s.reshape(db, t_len, B_HEADS, HEAD_DIM), kv_s5, caches_b[g][i], dil)
                o_ss.append(o.reshape(db * t_len, gw))
                l_ss.append(l.reshape(db * t_len, gw))
                keep = min(window, seq)
                kv_keep = kv_p.reshape(n_batch, seq, 2 * gw)[:, seq - keep:]
                outs_kv.setdefault("b%d_p" % g, []).append(
                    kv_keep.reshape(n_batch, keep, 2, B_HEADS, HEAD_DIM))
                outs_kv.setdefault("b%d_s" % g, []).append(kv_s5)
            wo = w_o_b[i].astype(_BF16)
            xp = _proj_ln(o_ps, l_ps, wo, None, xp, ln_g, ln_b)
            xs = _proj_ln(o_ss, l_ss, wo, None, xs, ln_g, ln_b)

        wq = peer_w_query[layer].astype(_BF16)
        keys = peer_sub_keys[layer].reshape(N_KEY_BLOCKS, PEER_N_KEYS, HALF_KEY_DIM).astype(_BF16)
        ut = peer_u[layer].T.astype(_BF16)
        vt = peer_v[layer].astype(_BF16)
        xp = _peer_ffn_ln(xp, wq, keys, ut, vt, row(ln_ffn_g[layer]), row(ln_ffn_b[layer]))
        xs = _peer_ffn_ln(xs, wq, keys, ut, vt, row(ln_ffn_g[layer]), row(ln_ffn_b[layer]))

    return (xp.reshape(n_batch, seq, D_MODEL), xs.reshape(db, t_len, D_MODEL),
            jnp.stack(outs_kv["a_p"]), jnp.stack(outs_kv["a_s"]),
            jnp.stack(outs_kv["b0_p"]), jnp.stack(outs_kv["b0_s"]),
            jnp.stack(outs_kv["b1_p"]), jnp.stack(outs_kv["b1_s"]),
            jnp.stack(outs_kv["b2_p"]), jnp.stack(outs_kv["b2_s"]))
```

```python
import functools
import math

import jax
import jax.numpy as jnp
from jax import lax
from jax.experimental import pallas as pl
from jax.experimental.pallas import tpu as pltpu

D_MODEL = 2048
SEQ = 2048
DEPTH = 2
DEC_SEQ = 4
PAST_LEN = 8192
HEAD_DIM = 64
ROPE_THETA = 10000.0
BLOCK = 128
A_HEADS = D_MODEL // HEAD_DIM
A_KV_HEADS = A_HEADS // 8
A_GROUP = A_HEADS // A_KV_HEADS
A_WINDOW = 128
A_QKV = (A_HEADS + 2 * A_KV_HEADS) * HEAD_DIM
B_GROUPS = ((128, 1), (512, 4), (2048, 16))
N_B_GROUPS = len(B_GROUPS)
B_HEADS = D_MODEL // (2 * HEAD_DIM)
B_GROUP_WIDTH = B_HEADS * HEAD_DIM
B_KEYS = 128
PEER_HEADS = 8
PEER_N_KEYS = 128
PEER_N_EXPERTS = PEER_N_KEYS * PEER_N_KEYS
PEER_TOPK = 16
PEER_KEY_DIM = 256
DN_ALPHA = (2.0 * DEPTH) ** 0.25
LN_EPS = 1e-5

LANES = 128
SUBLANES = 8
TOKEN_TILE = 512
VMEM_LIMIT = 56 * 1024 * 1024

_F32 = jnp.float32
_BF16 = jnp.bfloat16
_NEG_INF = float("-inf")
_NT = (((1,), (1,)), ((), ()))


def _cparams(*sem):
    return pltpu.CompilerParams(dimension_semantics=sem, vmem_limit_bytes=VMEM_LIMIT)


MXU_COLS = 256


def _rope(blk, cos, sin):
    lane = lax.broadcasted_iota(jnp.int32, blk.shape, 1)
    first_half = (lane % HEAD_DIM) < (HEAD_DIM // 2)
    partner = jnp.where(first_half, pltpu.roll(blk, LANES - HEAD_DIM // 2, 1),
                        pltpu.roll(blk, HEAD_DIM // 2, 1))
    return blk * cos + partner * sin


def _project_chunks(xb, w_ref, emit, bias_ref=None):
    for c in range(w_ref.shape[1] // MXU_COLS):
        cols = slice(c * MXU_COLS, (c + 1) * MXU_COLS)
        acc = jnp.dot(xb, w_ref[:, cols], preferred_element_type=_F32)
        if bias_ref is not None:
            acc = acc + bias_ref[:, cols]
        for h in range(MXU_COLS // LANES):
            emit(c * (MXU_COLS // LANES) + h, acc[:, h * LANES:(h + 1) * LANES])


def _qkv_rope_kernel(x_ref, w_ref, b_ref, cos_ref, sin_ref, o_ref, *, n_rope_groups):
    def emit(g, blk):
        if g < n_rope_groups:
            blk = _rope(blk, cos_ref[...], sin_ref[...])
        o_ref[:, g * LANES:(g + 1) * LANES] = blk

    _project_chunks(x_ref[...].astype(_BF16), w_ref, emit, b_ref)


def _qkv_rope(x, w_bf16, bias, cos, sin, n_rope_cols):
    n, k = x.shape
    ncols = w_bf16.shape[1]
    tm = TOKEN_TILE
    return pl.pallas_call(
        functools.partial(_qkv_rope_kernel, n_rope_groups=n_rope_cols // LANES),
        grid=(n // tm,),
        in_specs=[
            pl.BlockSpec((tm, k), lambda i: (i, 0)),
            pl.BlockSpec((k, ncols), lambda i: (0, 0)),
            pl.BlockSpec((1, ncols), lambda i: (0, 0)),
            pl.BlockSpec((tm, LANES), lambda i: (i, 0)),
            pl.BlockSpec((tm, LANES), lambda i: (i, 0)),
        ],
        out_specs=pl.BlockSpec((tm, ncols), lambda i: (i, 0)),
        out_shape=jax.ShapeDtypeStruct((n, ncols), _F32),
        compiler_params=_cparams("parallel"),
        name="qkv_rope",
    )(x, w_bf16, bias, cos, sin)


def _qkv_group_kernel(x_ref, w_ref, cos_ref, sin_ref, *refs, dil):
    j = pl.program_id(1)
    if dil == 1:
        q_ref, kv_ref = refs
    else:
        q_ref, kv_ref, kvr_ref, scr = refs
    xb = x_ref[...].astype(_BF16)
    per_residue = x_ref.shape[0] // dil

    def emitter(token_ref, residue_ref, rotate):
        def emit(g, blk):
            lanes = slice(g * LANES, (g + 1) * LANES)
            if rotate:
                blk = _rope(blk, cos_ref[...], sin_ref[...])
            if token_ref is not None:
                token_ref[:, lanes] = blk
            if residue_ref is not None:
                scr[g] = blk
                for r in range(dil):
                    residue_ref[r * per_residue:(r + 1) * per_residue, lanes] = (
                        scr[g, pl.ds(r, per_residue, stride=dil), :])
        return emit

    @pl.when(j == 0)
    def _():
        _project_chunks(xb, w_ref, emitter(q_ref if dil == 1 else None,
                                           q_ref if dil > 1 else None, True))

    @pl.when(j == 1)
    def _():
        _project_chunks(xb, w_ref, emitter(kv_ref, kvr_ref if dil > 1 else None, True))

    @pl.when(j == 2)
    def _():
        _project_chunks(xb, w_ref, emitter(kv_ref, kvr_ref if dil > 1 else None, False))


def _qkv_group(x, w_bf16, cos, sin, dil):
    n, k = x.shape
    tm, tn = TOKEN_TILE, B_GROUP_WIDTH
    q_spec = pl.BlockSpec((tm, tn), lambda i, j: (i, 0))
    kv_spec = pl.BlockSpec((tm, tn), lambda i, j: (i, jnp.maximum(j - 1, 0)))
    q_sds = jax.ShapeDtypeStruct((n, tn), _F32)
    kv_sds = jax.ShapeDtypeStruct((n, 2 * tn), _F32)
    return pl.pallas_call(
        functools.partial(_qkv_group_kernel, dil=dil),
        grid=(n // tm, 3),
        in_specs=[
            pl.BlockSpec((tm, k), lambda i, j: (i, 0)),
            pl.BlockSpec((k, tn), lambda i, j: (0, j)),
            pl.BlockSpec((tm, LANES), lambda i, j: (i, 0)),
            pl.BlockSpec((tm, LANES), lambda i, j: (i, 0)),
        ],
        out_specs=[q_spec, kv_spec] + ([kv_spec] if dil > 1 else []),
        out_shape=[q_sds, kv_sds] + ([kv_sds] if dil > 1 else []),
        scratch_shapes=[pltpu.VMEM((tn // LANES, tm, LANES), _F32)] if dil > 1 else [],
        compiler_params=_cparams("parallel", "arbitrary"),
        name="qkv_group_d%d" % dil,
    )(x, w_bf16, cos, sin)


def _band_attn_kernel(sink_ref, *refs, n_pieces, has_prev, n_pairs, shared_kv, has_sink, with_lse):
    q_refs = refs[:n_pieces]
    kc_refs = refs[n_pieces:2 * n_pieces]
    vc_refs = refs[2 * n_pieces:3 * n_pieces]
    pos = 3 * n_pieces
    if has_prev:
        kp_ref, vp_ref = refs[pos:pos + 2]
        pos += 2
    o_ref = refs[pos]
    lse_ref = refs[pos + 1] if with_lse else None
    n_keys = 2 * BLOCK if has_prev else BLOCK
    g = pl.program_id(1)
    j = pl.program_id(2)
    qi = lax.broadcasted_iota(jnp.int32, (BLOCK, n_keys), 0)
    kj = lax.broadcasted_iota(jnp.int32, (BLOCK, n_keys), 1)
    if has_prev:
        first_key = jnp.maximum(qi, jnp.where(j > 0, 0, BLOCK))
        valid = (kj >= first_key) & (kj <= qi + BLOCK)
    else:
        valid = kj <= qi
    lane = lax.broadcasted_iota(jnp.int32, (BLOCK, LANES), 1)
    low = lane < HEAD_DIM

    def rows(piece_refs, p):
        parts = [r[:, p * LANES:(p + 1) * LANES] for r in piece_refs]
        return parts[0] if len(parts) == 1 else jnp.concatenate(parts, axis=0)

    def keys(cur_refs, prev_ref, p):
        if shared_kv:
            keep = (lane // HEAD_DIM) == (g % 2)
            cur = cur_refs[0][...]
            cur = jnp.where(keep, cur, pltpu.roll(cur, HEAD_DIM, 1))
            prev = prev_ref[...]
            prev = jnp.where(keep, prev, pltpu.roll(prev, HEAD_DIM, 1))
        else:
            cur = rows(cur_refs, p)
            prev = prev_ref[:, p * LANES:(p + 1) * LANES] if has_prev else None
        return (jnp.concatenate([prev, cur], axis=0) if has_prev else cur).astype(_BF16)

    for p in range(n_pairs):
        if p == 0 or not shared_kv:
            kcat = keys(kc_refs, kp_ref if has_prev else None, p)
            vcat = keys(vc_refs, vp_ref if has_prev else None, p)
        q2 = rows(q_refs, p)
        outs, lses = [], []
        for h in range(2):
            qm = jnp.where(low if h == 0 else ~low, q2, 0.0).astype(_BF16)
            s = lax.dot_general(qm, kcat, _NT, preferred_element_type=_F32)
            s = jnp.where(valid, s, _NEG_INF)
            m = jnp.max(s, axis=-1, keepdims=True)
            if has_sink:
                sink = sink_ref[g * (2 * n_pairs) + 2 * p + h]
                m = jnp.maximum(m, sink)
            e = jnp.exp(s - m)
            den = jnp.sum(e, axis=-1, keepdims=True)
            if has_sink:
                den = den + jnp.exp(sink - m)
            pn = (e / den).astype(_BF16)
            outs.append(jnp.dot(pn, vcat, preferred_element_type=_F32))
            lses.append(m + jnp.log(den))
        o_ref[:, p * LANES:(p + 1) * LANES] = jnp.where(low, outs[0], outs[1])
        if with_lse:
            lse_ref[:, p * LANES:(p + 1) * LANES] = jnp.where(low, lses[0], lses[1])


def _band_attention_a(qkv, sinks, n_batch):
    nb = SEQ // BLOCK
    nq = A_HEADS * HEAD_DIM
    q_width = A_GROUP * HEAD_DIM
    k_blk0 = nq // LANES
    v_blk0 = k_blk0 + A_KV_HEADS * HEAD_DIM // LANES
    prev = lambda b, j: b * nb + jnp.maximum(j - 1, 0)
    (o,) = pl.pallas_call(
        functools.partial(_band_attn_kernel, n_pieces=1, has_prev=True, n_pairs=q_width // LANES,
                          shared_kv=True, has_sink=True, with_lse=False),
        grid_spec=pltpu.PrefetchScalarGridSpec(
            num_scalar_prefetch=1,
            grid=(n_batch, A_KV_HEADS, nb),
            in_specs=[
                pl.BlockSpec((BLOCK, q_width), lambda b, g, j, s: (b * nb + j, g)),
                pl.BlockSpec((BLOCK, LANES), lambda b, g, j, s: (b * nb + j, k_blk0 + g // 2)),
                pl.BlockSpec((BLOCK, LANES), lambda b, g, j, s: (b * nb + j, v_blk0 + g // 2)),
                pl.BlockSpec((BLOCK, LANES), lambda b, g, j, s: (prev(b, j), k_blk0 + g // 2)),
                pl.BlockSpec((BLOCK, LANES), lambda b, g, j, s: (prev(b, j), v_blk0 + g // 2)),
            ],
            out_specs=[pl.BlockSpec((BLOCK, q_width), lambda b, g, j, s: (b * nb + j, g))],
        ),
        out_shape=[jax.ShapeDtypeStruct((n_batch * SEQ, nq), _F32)],
        compiler_params=_cparams("parallel", "parallel", "arbitrary"),
        name="band_attn_a",
    )(sinks, qkv, qkv, qkv, qkv, qkv)
    return o


def _band_attention_b(q, kv, dil, n_batch):
    gw = B_GROUP_WIDTH
    nb = SEQ // dil // BLOCK
    per_tile = TOKEN_TILE // dil
    n_pieces = max(1, BLOCK // per_tile)
    piece = BLOCK // n_pieces
    tiles_per_seq = SEQ // TOKEN_TILE
    blocks_per_tile = TOKEN_TILE // piece

    def row_blk(b, r, j, c):
        if n_pieces == 1:
            tiles_per_blk = per_tile // BLOCK
            tile = b * tiles_per_seq + j // tiles_per_blk
            return tile * blocks_per_tile + r * tiles_per_blk + j % tiles_per_blk
        tile = b * tiles_per_seq + j * n_pieces + c
        return tile * blocks_per_tile + r

    def spec(col, c, prev=False):
        if prev:
            return pl.BlockSpec((piece, gw), lambda b, r, j, s: (row_blk(b, r, jnp.maximum(j - 1, 0), c), col))
        return pl.BlockSpec((piece, gw), lambda b, r, j, s: (row_blk(b, r, j, c), col))

    has_prev = nb > 1
    in_specs = ([spec(0, c) for c in range(n_pieces)] + [spec(0, c) for c in range(n_pieces)]
                + [spec(1, c) for c in range(n_pieces)])
    args = [q] * n_pieces + [kv] * (2 * n_pieces)
    if has_prev:
        in_specs += [spec(0, 0, prev=True), spec(1, 0, prev=True)]
        args += [kv, kv]
    o_sds = jax.ShapeDtypeStruct((n_batch * SEQ, gw), _F32)
    o_spec = pl.BlockSpec((BLOCK, gw), lambda b, r, j, s: ((b * dil + r) * nb + j, 0))
    o, lse = pl.pallas_call(
        functools.partial(_band_attn_kernel, n_pieces=n_pieces, has_prev=has_prev,
                          n_pairs=gw // LANES, shared_kv=False, has_sink=False, with_lse=True),
        grid_spec=pltpu.PrefetchScalarGridSpec(
            num_scalar_prefetch=1,
            grid=(n_batch, dil, nb),
            in_specs=in_specs,
            out_specs=[o_spec, o_spec],
        ),
        out_shape=[o_sds, o_sds],
        compiler_params=_cparams("parallel", "parallel", "arbitrary"),
        name="band_attn_d%d" % dil,
    )(jnp.zeros((1,), _F32), *args)
    if dil > 1:
        o, lse = _token_major(o, dil, n_batch), _token_major(lse, dil, n_batch)
    return o, lse


def _token_major(x, dil, n_batch):
    w = x.shape[1]
    x = x.reshape(n_batch, dil, SEQ // dil, w)
    return jnp.swapaxes(x, 1, 2).reshape(n_batch * SEQ, w)


STEP_NEW_ROWS = 8
STEP_CACHE_BYTES = 8 * 1024 * 1024


def _step_attn_kernel(*refs, dil, rows_per_t, has_sink, with_lse):
    q_ref, kn_ref, vn_ref, cache_ref = refs[:4]
    pos = 4
    if has_sink:
        sink_ref = refs[pos]
        pos += 1
    o_ref = refs[pos]
    lse_ref = refs[pos + 1] if with_lse else None
    n_b, n_h, n_rows = q_ref.shape[:3]
    length = cache_ref.shape[-1]

    def step_of(shape):
        return (lax.broadcasted_iota(jnp.int32, shape, 0) // rows_per_t) % DEC_SEQ

    col_c = lax.broadcasted_iota(jnp.int32, (n_rows, length), 1)
    col_n = lax.broadcasted_iota(jnp.int32, (n_rows, STEP_NEW_ROWS), 1)
    if dil == 1:
        valid_c = col_c >= step_of((n_rows, length))
        valid_n = col_n <= step_of((n_rows, STEP_NEW_ROWS))
    else:
        valid_c = (col_c % dil) == step_of((n_rows, length))
        valid_n = col_n == step_of((n_rows, STEP_NEW_ROWS))

    def slot(i, carry):
        b, h = i // n_h, i % n_h
        q = q_ref[b, h].astype(_BF16)
        s_c = jnp.dot(q, cache_ref[b, 0, h].astype(_BF16), preferred_element_type=_F32)
        s_n = lax.dot_general(q, kn_ref[b, h].astype(_BF16), _NT, preferred_element_type=_F32)
        s_c = jnp.where(valid_c, s_c, _NEG_INF)
        s_n = jnp.where(valid_n, s_n, _NEG_INF)
        m = jnp.maximum(jnp.max(s_c, axis=-1, keepdims=True), jnp.max(s_n, axis=-1, keepdims=True))
        if has_sink:
            sink = sink_ref[h]
            m = jnp.maximum(m, sink)
        e_c = jnp.exp(s_c - m)
        e_n = jnp.exp(s_n - m)
        den = jnp.sum(e_c, axis=-1, keepdims=True) + jnp.sum(e_n, axis=-1, keepdims=True)
        if has_sink:
            den = den + jnp.exp(sink - m)
        o = lax.dot_general((e_c / den).astype(_BF16), cache_ref[b, 1, h].astype(_BF16), _NT,
                            preferred_element_type=_F32)
        o = o + jnp.dot((e_n / den).astype(_BF16), vn_ref[b, h].astype(_BF16),
                        preferred_element_type=_F32)
        o_ref[b, h] = o
        if with_lse:
            lse_ref[b, h] = jnp.broadcast_to(m + jnp.log(den), (n_rows, HEAD_DIM))
        return carry

    lax.fori_loop(0, n_b * n_h, slot, 0, unroll=2)


def _step_attention(q, kn, vn, cache_t, sink_rows, *, dil, rows_per_t, with_lse):
    db, n_h, n_rows = q.shape[:3]
    length = cache_t.shape[-1]
    slot_bytes = 2 * HEAD_DIM * length * 4
    hb = min(n_h, max(1, STEP_CACHE_BYTES // slot_bytes))
    bb = min(db, max(1, STEP_CACHE_BYTES // (slot_bytes * hb)))
    q_spec = pl.BlockSpec((bb, hb, n_rows, HEAD_DIM), lambda i, j: (i, j, 0, 0))
    n_spec = pl.BlockSpec((bb, hb, STEP_NEW_ROWS, HEAD_DIM), lambda i, j: (i, j, 0, 0))
    in_specs = [q_spec, n_spec, n_spec,
                pl.BlockSpec((bb, 2, hb, HEAD_DIM, length), lambda i, j: (i, 0, j, 0, 0))]
    args = [q, kn, vn, cache_t]
    if sink_rows is not None:
        in_specs.append(pl.BlockSpec((hb, n_rows, 1), lambda i, j: (j, 0, 0)))
        args.append(sink_rows)
    o_sds = jax.ShapeDtypeStruct(q.shape, _F32)
    return pl.pallas_call(
        functools.partial(_step_attn_kernel, dil=dil, rows_per_t=rows_per_t,
                          has_sink=sink_rows is not None, with_lse=with_lse),
        grid=(db // bb, n_h // hb),
        in_specs=in_specs,
        out_specs=[q_spec, q_spec] if with_lse else [q_spec],
        out_shape=[o_sds, o_sds] if with_lse else [o_sds],
        compiler_params=_cparams("parallel", "parallel"),
        name="step_attn_l%d" % length,
    )(*args)


def _slots(x, db, n_heads):
    x = x.reshape(db, DEC_SEQ, n_heads, HEAD_DIM).transpose(0, 2, 1, 3)
    return jnp.pad(x, [(0, 0), (0, 0), (0, STEP_NEW_ROWS - DEC_SEQ), (0, 0)])


def _step_b(q_s, kv_s, cache, dil):
    db = cache.shape[0]
    gw = B_GROUP_WIDTH
    cache_t = cache.transpose(0, 2, 3, 4, 1)
    o, lse = _step_attention(_slots(q_s, db, B_HEADS), _slots(kv_s[:, :gw], db, B_HEADS),
                             _slots(kv_s[:, gw:], db, B_HEADS), cache_t, None,
                             dil=dil, rows_per_t=1, with_lse=True)
    back = lambda x: x[:, :, :DEC_SEQ].transpose(0, 2, 1, 3).reshape(db * DEC_SEQ, gw)
    return back(o), back(lse)


def _step_a(qkv_s, cache, sinks, db):
    nq = A_HEADS * HEAD_DIM
    nk = A_KV_HEADS * HEAD_DIM
    q = qkv_s[:, :nq].reshape(db, DEC_SEQ, A_KV_HEADS, A_GROUP, HEAD_DIM)
    q = q.transpose(0, 2, 1, 3, 4).reshape(db, A_KV_HEADS, DEC_SEQ * A_GROUP, HEAD_DIM)
    sink_rows = jnp.tile(sinks.reshape(A_KV_HEADS, 1, A_GROUP), (1, DEC_SEQ, 1))
    sink_rows = sink_rows.reshape(A_KV_HEADS, DEC_SEQ * A_GROUP, 1)
    (o,) = _step_attention(q, _slots(qkv_s[:, nq:nq + nk], db, A_KV_HEADS),
                           _slots(qkv_s[:, nq + nk:], db, A_KV_HEADS),
                           cache.transpose(0, 2, 3, 4, 1), sink_rows,
                           dil=1, rows_per_t=A_GROUP, with_lse=False)
    o = o.reshape(db, A_KV_HEADS, DEC_SEQ, A_GROUP, HEAD_DIM).transpose(0, 2, 1, 3, 4)
    return o.reshape(db * DEC_SEQ, nq)


def _layer_norm_rows(z, g, b):
    mu = jnp.mean(z, axis=-1, keepdims=True)
    zc = z - mu
    var = jnp.mean(zc * zc, axis=-1, keepdims=True)
    return zc * lax.rsqrt(var + LN_EPS) * g + b


def _proj_ln_kernel(*refs, n_merge, has_bias):
    if n_merge:
        o_refs = refs[:n_merge]
        l_refs = refs[n_merge:2 * n_merge]
        pos = 2 * n_merge
        ls = [r[...] for r in l_refs]
        m = functools.reduce(jnp.maximum, ls)
        es = [jnp.exp(l - m) for l in ls]
        den = functools.reduce(jnp.add, es)
        o = functools.reduce(jnp.add, [(e / den) * r[...] for e, r in zip(es, o_refs)])
    else:
        o = refs[0][...]
        pos = 1
    w_ref = refs[pos]
    pos += 1
    y = jnp.dot(o.astype(_BF16), w_ref[...], preferred_element_type=_F32)
    if has_bias:
        y = y + refs[pos][...]
        pos += 1
    x_ref, g_ref, b_ref, out_ref = refs[pos:pos + 4]
    out_ref[...] = _layer_norm_rows(DN_ALPHA * x_ref[...] + y, g_ref[...], b_ref[...])


def _proj_ln(o_list, lse_list, w_bf16, bias, x, ln_g, ln_b, tm=256):
    n = x.shape[0]
    k = w_bf16.shape[0]
    row = lambda i: (i, 0)
    fixed = lambda i: (0, 0)
    args = list(o_list) + list(lse_list) + [w_bf16]
    specs = [pl.BlockSpec((tm, k), row)] * (len(o_list) + len(lse_list))
    specs.append(pl.BlockSpec((k, D_MODEL), fixed))
    if bias is not None:
        args.append(bias)
        specs.append(pl.BlockSpec((1, D_MODEL), fixed))
    args += [x, ln_g, ln_b]
    specs += [pl.BlockSpec((tm, D_MODEL), row), pl.BlockSpec((1, D_MODEL), fixed),
              pl.BlockSpec((1, D_MODEL), fixed)]
    return pl.pallas_call(
        functools.partial(_proj_ln_kernel, n_merge=len(lse_list), has_bias=bias is not None),
        grid=(n // tm,),
        in_specs=specs,
        out_specs=pl.BlockSpec((tm, D_MODEL), row),
        out_shape=jax.ShapeDtypeStruct((n, D_MODEL), _F32),
        compiler_params=_cparams("parallel"),
        name="proj_ln",
    )(*args)


N_KEY_BLOCKS = PEER_HEADS * 2
HALF_KEY_DIM = PEER_KEY_DIM // 2


def _peer_score_kernel(x_ref, wq_ref, keys_ref, sc_ref):
    q = jnp.dot(x_ref[...].astype(_BF16), wq_ref[...], preferred_element_type=_F32).astype(_BF16)
    for blk in range(N_KEY_BLOCKS):
        sc_ref[blk * PEER_N_KEYS:(blk + 1) * PEER_N_KEYS, :] = lax.dot_general(
            keys_ref[blk], q[:, blk * HALF_KEY_DIM:(blk + 1) * HALF_KEY_DIM], _NT,
            preferred_element_type=_F32)


def _peer_scores(x, wq_bf16, keys_bf16):
    n = x.shape[0]
    tm = TOKEN_TILE
    return pl.pallas_call(
        _peer_score_kernel,
        grid=(n // tm,),
        in_specs=[
            pl.BlockSpec((tm, D_MODEL), lambda i: (i, 0)),
            pl.BlockSpec((D_MODEL, PEER_HEADS * PEER_KEY_DIM), lambda i: (0, 0)),
            pl.BlockSpec((N_KEY_BLOCKS, PEER_N_KEYS, HALF_KEY_DIM), lambda i: (0, 0, 0)),
        ],
        out_specs=pl.BlockSpec((N_KEY_BLOCKS * PEER_N_KEYS, tm), lambda i: (0, i)),
        out_shape=jax.ShapeDtypeStruct((N_KEY_BLOCKS * PEER_N_KEYS, n), _F32),
        compiler_params=_cparams("parallel"),
        name="peer_scores",
    )(x, wq_bf16, keys_bf16)


TOPK_TOKENS = 128


def _top1_rows(vals, ids, big):
    m = jnp.max(vals, axis=0, keepdims=True)
    sel = jnp.min(jnp.where(vals == m, ids, big), axis=0, keepdims=True)
    return m, sel, ids == sel


def _peer_topk_kernel(sc_ref, e_ref, gate_ref, top_s, top_i, sel_e, sel_g):
    tt = sc_ref.shape[1]
    key_id = lax.broadcasted_iota(jnp.int32, (PEER_N_KEYS, tt), 0).astype(_F32)

    def stage1(blk, carry):
        s = sc_ref[pl.ds(pl.multiple_of(blk * PEER_N_KEYS, PEER_N_KEYS), PEER_N_KEYS), :]
        for k in range(PEER_TOPK):
            m, sel, onehot = _top1_rows(s, key_id, float(PEER_N_KEYS))
            top_s[blk, k:k + 1, :] = m
            top_i[blk, k:k + 1, :] = sel
            s = jnp.where(onehot, _NEG_INF, s)
        return carry

    lax.fori_loop(0, N_KEY_BLOCKS, stage1, 0, unroll=2)

    sub = lax.broadcasted_iota(jnp.int32, (SUBLANES, tt), 0).astype(_F32)
    flat_id = jnp.concatenate(
        [sub, sub + SUBLANES] + [sub + a * PEER_TOPK for a in range(1, SUBLANES)]
        + [(sub + SUBLANES) * PEER_TOPK], axis=0)

    def bcast(row):
        return jnp.broadcast_to(row, (SUBLANES, tt))

    def stage2(h, carry):
        s0, s1 = top_s[2 * h], top_s[2 * h + 1]
        i0, i1 = top_i[2 * h] * PEER_N_KEYS, top_i[2 * h + 1]
        lo_s, hi_s, lo_i, hi_i = s1[:SUBLANES], s1[SUBLANES:], i1[:SUBLANES], i1[SUBLANES:]
        cand = jnp.concatenate(
            [bcast(s0[0:1]) + lo_s, bcast(s0[0:1]) + hi_s]
            + [bcast(s0[a:a + 1]) + lo_s for a in range(1, SUBLANES)]
            + [s0[SUBLANES:] + bcast(s1[0:1])], axis=0)
        cand_e = jnp.concatenate(
            [bcast(i0[0:1]) + lo_i, bcast(i0[0:1]) + hi_i]
            + [bcast(i0[a:a + 1]) + lo_i for a in range(1, SUBLANES)]
            + [i0[SUBLANES:] + bcast(i1[0:1])], axis=0)
        best = []
        for k in range(PEER_TOPK):
            m, _, onehot = _top1_rows(cand, flat_id, float(PEER_TOPK * PEER_TOPK))
            best.append(m)
            sel_e[h, k:k + 1, :] = jnp.max(jnp.where(onehot, cand_e, -1.0), axis=0, keepdims=True)
            cand = jnp.where(onehot, _NEG_INF, cand)
        ex = [jnp.exp(b - best[0]) for b in best]
        den = functools.reduce(jnp.add, ex)
        for k in range(PEER_TOPK):
            sel_g[h, k:k + 1, :] = ex[k] / den
        return carry

    lax.fori_loop(0, PEER_HEADS, stage2, 0, unroll=2)
    n_sel = PEER_HEADS * PEER_TOPK
    e_ref[...] = sel_e[...].reshape(n_sel, tt).T.astype(jnp.int32)
    gate_ref[...] = sel_g[...].reshape(n_sel, tt).T


def _peer_topk(sc_t):
    n = sc_t.shape[1]
    tt = TOPK_TOKENS
    n_sel = PEER_HEADS * PEER_TOPK
    return pl.pallas_call(
        _peer_topk_kernel,
        grid=(n // tt,),
        in_specs=[pl.BlockSpec((N_KEY_BLOCKS * PEER_N_KEYS, tt), lambda i: (0, i))],
        out_specs=[pl.BlockSpec((tt, n_sel), lambda i: (i, 0)),
                   pl.BlockSpec((tt, n_sel), lambda i: (i, 0))],
        out_shape=[jax.ShapeDtypeStruct((n, n_sel), jnp.int32),
                   jax.ShapeDtypeStruct((n, n_sel), _F32)],
        scratch_shapes=[pltpu.VMEM((N_KEY_BLOCKS, PEER_TOPK, tt), _F32),
                        pltpu.VMEM((N_KEY_BLOCKS, PEER_TOPK, tt), _F32),
                        pltpu.VMEM((PEER_HEADS, PEER_TOPK, tt), _F32),
                        pltpu.VMEM((PEER_HEADS, PEER_TOPK, tt), _F32)],
        compiler_params=_cparams("parallel"),
        name="peer_topk",
    )(sc_t)


GATE_TOKENS = 64
GATE_UNROLL = 16


def _peer_gate_kernel(e_ref, gate_ref, g_ref, scr):
    tb, _, n_sel = e_ref.shape
    row_id = lax.broadcasted_iota(jnp.int32, (PEER_N_KEYS, n_sel), 0)

    def body(n, carry):
        e = e_ref[n]
        gate = gate_ref[n]
        hi = gate.astype(_BF16).astype(_F32)
        lo = gate - hi
        is_i = row_id == (e >> 7)
        lhs = jnp.concatenate([jnp.where(is_i, hi, 0.0), jnp.where(is_i, lo, 0.0)], axis=1)
        one = jnp.where(row_id == (e & (PEER_N_KEYS - 1)), 1.0, 0.0)
        rhs = jnp.concatenate([one, one], axis=1)
        res = lax.dot_general(lhs.astype(_BF16), rhs.astype(_BF16), _NT, preferred_element_type=_F32)
        for ib in range(PEER_N_KEYS // SUBLANES):
            start = pl.multiple_of((ib * tb + n) * SUBLANES, SUBLANES)
            scr[pl.ds(start, SUBLANES), :] = res[ib * SUBLANES:(ib + 1) * SUBLANES]
        return carry

    lax.fori_loop(0, tb, body, 0, unroll=GATE_UNROLL)
    for i in range(PEER_N_KEYS):
        start = (i // SUBLANES) * tb * SUBLANES + i % SUBLANES
        g_ref[:, i * PEER_N_KEYS:(i + 1) * PEER_N_KEYS] = scr[pl.ds(start, tb, stride=SUBLANES), :]


def _peer_gates(expert, gate):
    n, n_sel = expert.shape
    tb = GATE_TOKENS
    expert = expert.reshape(n, 1, n_sel)
    gate = gate.reshape(n, 1, n_sel)
    return pl.pallas_call(
        _peer_gate_kernel,
        grid=(n // tb,),
        in_specs=[pl.BlockSpec((tb, 1, n_sel), lambda i: (i, 0, 0)),
                  pl.BlockSpec((tb, 1, n_sel), lambda i: (i, 0, 0))],
        out_specs=pl.BlockSpec((tb, PEER_N_EXPERTS), lambda i: (i, 0)),
        out_shape=jax.ShapeDtypeStruct((n, PEER_N_EXPERTS), _F32),
        scratch_shapes=[pltpu.VMEM((tb * PEER_N_KEYS, PEER_N_KEYS), _F32)],
        compiler_params=_cparams("parallel"),
        name="peer_gates",
    )(expert, gate)


EXPERT_CHUNK = 1024
_SQRT_HALF = math.sqrt(0.5)


def _peer_main_kernel(x_ref, ut_ref, g_ref, v_ref, lng_ref, lnb_ref, o_ref, xb_scr, acc_scr):
    c = pl.program_id(1)

    @pl.when(c == 0)
    def _():
        xb_scr[...] = x_ref[...].astype(_BF16)
        acc_scr[...] = jnp.zeros_like(acc_scr)

    h_parts = []
    for k in range(ut_ref.shape[1] // MXU_COLS):
        cols = slice(k * MXU_COLS, (k + 1) * MXU_COLS)
        a = jnp.dot(xb_scr[...], ut_ref[:, cols], preferred_element_type=_F32)
        gelu = 0.5 * a * (1.0 + lax.erf(a * _SQRT_HALF))
        h_parts.append((g_ref[:, cols] * gelu).astype(_BF16))
    h = jnp.concatenate(h_parts, axis=1)
    acc_scr[...] += jnp.dot(h, v_ref[...], preferred_element_type=_F32)

    @pl.when(c == pl.num_programs(1) - 1)
    def _():
        o_ref[...] = _layer_norm_rows(DN_ALPHA * x_ref[...] + acc_scr[...], lng_ref[...], lnb_ref[...])


def _peer_main(x, ut_bf16, gates, v_bf16, ln_g, ln_b):
    n = x.shape[0]
    tm, ce = TOKEN_TILE, EXPERT_CHUNK
    return pl.pallas_call(
        _peer_main_kernel,
        grid=(n // tm, PEER_N_EXPERTS // ce),
        in_specs=[
            pl.BlockSpec((tm, D_MODEL), lambda i, c: (i, 0)),
            pl.BlockSpec((D_MODEL, ce), lambda i, c: (0, c)),
            pl.BlockSpec((tm, ce), lambda i, c: (i, c)),
            pl.BlockSpec((ce, D_MODEL), lambda i, c: (c, 0)),
            pl.BlockSpec((1, D_MODEL), lambda i, c: (0, 0)),
            pl.BlockSpec((1, D_MODEL), lambda i, c: (0, 0)),
        ],
        out_specs=pl.BlockSpec((tm, D_MODEL), lambda i, c: (i, 0)),
        out_shape=jax.ShapeDtypeStruct((n, D_MODEL), _F32),
        scratch_shapes=[pltpu.VMEM((tm, D_MODEL), _BF16), pltpu.VMEM((tm, D_MODEL), _F32)],
        compiler_params=_cparams("parallel", "arbitrary"),
        name="peer_main",
    )(x, ut_bf16, gates, v_bf16, ln_g, ln_b)


def _peer_ffn_ln(x, wq_bf16, keys_bf16, ut_bf16, v_bf16, ln_g, ln_b):
    sc_t = _peer_scores(x, wq_bf16, keys_bf16)
    expert, gate = _peer_topk(sc_t)
    gates = _peer_gates(expert, gate)
    return _peer_main(x, ut_bf16, gates, v_bf16, ln_g, ln_b)


def _rope_tables(pos):
    half = HEAD_DIM // 2
    inv = ROPE_THETA ** (-jnp.arange(half, dtype=_F32) / half)
    ang = pos.astype(_F32)[:, None] * inv[None, :]
    cos, sin = jnp.cos(ang), jnp.sin(ang)
    cos = jnp.tile(cos, (1, LANES // half))
    sin = jnp.tile(jnp.concatenate([-sin, sin], axis=1), (1, LANES // HEAD_DIM))
    return cos, sin


def kernel(x_prompt, x_sample, cache_a_kv, cache_b1_kv, cache_b2_kv, cache_b3_kv, w_qkv_a, b_qkv_a, sinks_a, w_o_a, b_o_a, w_qkv_b, w_o_b, ln_mix_g, ln_mix_b, ln_ffn_g, ln_ffn_b, peer_w_query, peer_sub_keys, peer_u, peer_v):
    n_batch, seq = x_prompt.shape[:2]
    db, t_len = x_sample.shape[:2]
    xp = x_prompt.reshape(n_batch * seq, D_MODEL)
    xs = x_sample.reshape(db * t_len, D_MODEL)
    cos_p, sin_p = _rope_tables(jnp.tile(jnp.arange(seq, dtype=jnp.int32), n_batch))
    cos_s, sin_s = _rope_tables(jnp.tile(PAST_LEN + jnp.arange(t_len, dtype=jnp.int32), db))
    scale = HEAD_DIM ** -0.5
    row = lambda v: v.reshape(1, -1)
    caches_b = (cache_b1_kv, cache_b2_kv, cache_b3_kv)
    outs_kv = {}

    for layer in range(DEPTH):
        i = layer // 2
        ln_g, ln_b = row(ln_mix_g[layer]), row(ln_mix_b[layer])
        if layer % 2 == 0:
            nq = A_HEADS * HEAD_DIM
            nk = A_KV_HEADS * HEAD_DIM
            col_scale = jnp.concatenate([jnp.full((nq,), scale, _F32), jnp.ones((2 * nk,), _F32)])
            w = (w_qkv_a[i] * col_scale).astype(_BF16)
            bias = row(b_qkv_a[i] * col_scale)
            qkv_p = _qkv_rope(xp, w, bias, cos_p, sin_p, nq + nk)
            qkv_s = _qkv_rope(xs, w, bias, cos_s, sin_s, nq + nk)
            o_p = _band_attention_a(qkv_p, sinks_a[i], n_batch)
            o_s = _step_a(qkv_s, cache_a_kv[i], sinks_a[i], db)
            wo = w_o_a[i].astype(_BF16)
            xp = _proj_ln([o_p], [], wo, row(b_o_a[i]), xp, ln_g, ln_b)
            xs = _proj_ln([o_s], [], wo, row(b_o_a[i]), xs, ln_g, ln_b)
            keep = min(A_WINDOW, seq)
            kv_p = qkv_p.reshape(n_batch, seq, A_QKV)[:, seq - keep:, nq:]
            outs_kv.setdefault("a_p", []).append(kv_p.reshape(n_batch, keep, 2, A_KV_HEADS, HEAD_DIM))
            outs_kv.setdefault("a_s", []).append(qkv_s[:, nq:].reshape(db, t_len, 2, A_KV_HEADS, HEAD_DIM))
        else:
            gw = B_GROUP_WIDTH
            w3 = w_qkv_b[i].reshape(D_MODEL, 3, N_B_GROUPS, gw)
            o_ps, l_ps, o_ss, l_ss = [], [], [], []
            for g, (window, dil) in enumerate(B_GROUPS):
                w = jnp.concatenate([w3[:, 0, g] * scale, w3[:, 1, g], w3[:, 2, g]], axis=1).astype(_BF16)
                outs = _qkv_group(xp, w, cos_p, sin_p, dil)
                q_p, kv_p, kvr_p = outs if dil > 1 else (outs[0], outs[1], outs[1])
                q_s, kv_s = _qkv_group(xs, w, cos_s, sin_s, 1)
                o, l = _band_attention_b(q_p, kvr_p, dil, n_batch)
                o_ps.append(o)
                l_ps.append(l)
                kv_s5 = kv_s.reshape(db, t_len, 2, B_HEADS, HEAD_DIM)
                o, l = _step_b(q_s, kv_s, caches_b[g][i], dil)
                o_ss.append(o)
                l_ss.append(l)
                keep = min(window, seq)
                kv_keep = kv_p.reshape(n_batch, seq, 2 * gw)[:, seq - keep:]
                outs_kv.setdefault("b%d_p" % g, []).append(
                    kv_keep.reshape(n_batch, keep, 2, B_HEADS, HEAD_DIM))
                outs_kv.setdefault("b%d_s" % g, []).append(kv_s5)
            wo = w_o_b[i].astype(_BF16)
            xp = _proj_ln(o_ps, l_ps, wo, None, xp, ln_g, ln_b)
            xs = _proj_ln(o_ss, l_ss, wo, None, xs, ln_g, ln_b)

        wq = peer_w_query[layer].astype(_BF16)
        keys = peer_sub_keys[layer].reshape(N_KEY_BLOCKS, PEER_N_KEYS, HALF_KEY_DIM).astype(_BF16)
        ut = peer_u[layer].T.astype(_BF16)
        vt = peer_v[layer].astype(_BF16)
        xp = _peer_ffn_ln(xp, wq, keys, ut, vt, row(ln_ffn_g[layer]), row(ln_ffn_b[layer]))
        xs = _peer_ffn_ln(xs, wq, keys, ut, vt, row(ln_ffn_g[layer]), row(ln_ffn_b[layer]))

    return (xp.reshape(n_batch, seq, D_MODEL), xs.reshape(db, t_len, D_MODEL),
            jnp.stack(outs_kv["a_p"]), jnp.stack(outs_kv["a_s"]),
            jnp.stack(outs_kv["b0_p"]), jnp.stack(outs_kv["b0_s"]),
            jnp.stack(outs_kv["b1_p"]), jnp.stack(outs_kv["b1_s"]),
            jnp.stack(outs_kv["b2_p"]), jnp.stack(outs_kv["b2_s"]))
```

```python
import functools
import math

import jax
import jax.numpy as jnp
from jax import lax
from jax.experimental import pallas as pl
from jax.experimental.pallas import tpu as pltpu

D_MODEL = 2048
SEQ = 2048
DEPTH = 2
DEC_SEQ = 4
PAST_LEN = 8192
HEAD_DIM = 64
ROPE_THETA = 10000.0
BLOCK = 128
A_HEADS = D_MODEL // HEAD_DIM
A_KV_HEADS = A_HEADS // 8
A_GROUP = A_HEADS // A_KV_HEADS
A_WINDOW = 128
A_QKV = (A_HEADS + 2 * A_KV_HEADS) * HEAD_DIM
B_GROUPS = ((128, 1), (512, 4), (2048, 16))
N_B_GROUPS = len(B_GROUPS)
B_HEADS = D_MODEL // (2 * HEAD_DIM)
B_GROUP_WIDTH = B_HEADS * HEAD_DIM
B_KEYS = 128
PEER_HEADS = 8
PEER_N_KEYS = 128
PEER_N_EXPERTS = PEER_N_KEYS * PEER_N_KEYS
PEER_TOPK = 16
PEER_KEY_DIM = 256
DN_ALPHA = (2.0 * DEPTH) ** 0.25
LN_EPS = 1e-5

LANES = 128
SUBLANES = 8
TOKEN_TILE = 512
VMEM_LIMIT = 56 * 1024 * 1024

_F32 = jnp.float32
_BF16 = jnp.bfloat16
_NEG_INF = float("-inf")
_NT = (((1,), (1,)), ((), ()))


def _cparams(*sem):
    return pltpu.CompilerParams(dimension_semantics=sem, vmem_limit_bytes=VMEM_LIMIT)


MXU_COLS = 256


def _rope(blk, cos, sin):
    lane = lax.broadcasted_iota(jnp.int32, blk.shape, 1)
    first_half = (lane % HEAD_DIM) < (HEAD_DIM // 2)
    partner = jnp.where(first_half, pltpu.roll(blk, LANES - HEAD_DIM // 2, 1),
                        pltpu.roll(blk, HEAD_DIM // 2, 1))
    return blk * cos + partner * sin


def _project_chunks(xb, w_ref, emit, bias_ref=None):
    for c in range(w_ref.shape[1] // MXU_COLS):
        cols = slice(c * MXU_COLS, (c + 1) * MXU_COLS)
        acc = jnp.dot(xb, w_ref[:, cols], preferred_element_type=_F32)
        if bias_ref is not None:
            acc = acc + bias_ref[:, cols]
        for h in range(MXU_COLS // LANES):
            emit(c * (MXU_COLS // LANES) + h, acc[:, h * LANES:(h + 1) * LANES])


def _qkv_rope_kernel(x_ref, w_ref, b_ref, cos_ref, sin_ref, o_ref, *, n_rope_groups):
    def emit(g, blk):
        if g < n_rope_groups:
            blk = _rope(blk, cos_ref[...], sin_ref[...])
        o_ref[:, g * LANES:(g + 1) * LANES] = blk

    _project_chunks(x_ref[...].astype(_BF16), w_ref, emit, b_ref)


def _qkv_rope(x, w_bf16, bias, cos, sin, n_rope_cols):
    n, k = x.shape
    ncols = w_bf16.shape[1]
    tm = TOKEN_TILE
    return pl.pallas_call(
        functools.partial(_qkv_rope_kernel, n_rope_groups=n_rope_cols // LANES),
        grid=(n // tm,),
        in_specs=[
            pl.BlockSpec((tm, k), lambda i: (i, 0)),
            pl.BlockSpec((k, ncols), lambda i: (0, 0)),
            pl.BlockSpec((1, ncols), lambda i: (0, 0)),
            pl.BlockSpec((tm, LANES), lambda i: (i, 0)),
            pl.BlockSpec((tm, LANES), lambda i: (i, 0)),
        ],
        out_specs=pl.BlockSpec((tm, ncols), lambda i: (i, 0)),
        out_shape=jax.ShapeDtypeStruct((n, ncols), _F32),
        compiler_params=_cparams("parallel"),
        name="qkv_rope",
    )(x, w_bf16, bias, cos, sin)


def _qkv_group_kernel(x_ref, w_ref, cos_ref, sin_ref, *refs, dil):
    j = pl.program_id(1)
    if dil == 1:
        q_ref, kv_ref = refs
    else:
        q_ref, kv_ref, kvr_ref, scr = refs
    xb = x_ref[...].astype(_BF16)
    per_residue = x_ref.shape[0] // dil

    def emitter(token_ref, residue_ref, rotate):
        def emit(g, blk):
            lanes = slice(g * LANES, (g + 1) * LANES)
            if rotate:
                blk = _rope(blk, cos_ref[...], sin_ref[...])
            if token_ref is not None:
                token_ref[:, lanes] = blk
            if residue_ref is not None:
                scr[g] = blk
                for r in range(dil):
                    residue_ref[r, :, lanes] = scr[g, pl.ds(r, per_residue, stride=dil), :]
        return emit

    @pl.when(j == 0)
    def _():
        _project_chunks(xb, w_ref, emitter(q_ref if dil == 1 else None,
                                           q_ref if dil > 1 else None, True))

    @pl.when(j == 1)
    def _():
        _project_chunks(xb, w_ref, emitter(kv_ref, kvr_ref if dil > 1 else None, True))

    @pl.when(j == 2)
    def _():
        _project_chunks(xb, w_ref, emitter(kv_ref, kvr_ref if dil > 1 else None, False))


def _qkv_group(x, w_bf16, cos, sin, dil):
    n, k = x.shape
    tm, tn = TOKEN_TILE, B_GROUP_WIDTH
    kv_col = lambda j: jnp.maximum(j - 1, 0)
    kv_spec = pl.BlockSpec((tm, tn), lambda i, j: (i, kv_col(j)))
    kv_sds = jax.ShapeDtypeStruct((n, 2 * tn), _F32)
    if dil == 1:
        out_specs = [pl.BlockSpec((tm, tn), lambda i, j: (i, 0)), kv_spec]
        out_shape = [jax.ShapeDtypeStruct((n, tn), _F32), kv_sds]
        scratch = []
    else:
        tiles = SEQ // tm
        res_block = (dil, tm // dil, tn)
        res_rows = (n // SEQ * dil, SEQ // dil)
        out_specs = [pl.BlockSpec(res_block, lambda i, j: (i // tiles, i % tiles, 0)), kv_spec,
                     pl.BlockSpec(res_block, lambda i, j: (i // tiles, i % tiles, kv_col(j)))]
        out_shape = [jax.ShapeDtypeStruct(res_rows + (tn,), _F32), kv_sds,
                     jax.ShapeDtypeStruct(res_rows + (2 * tn,), _F32)]
        scratch = [pltpu.VMEM((tn // LANES, tm, LANES), _F32)]
    return pl.pallas_call(
        functools.partial(_qkv_group_kernel, dil=dil),
        grid=(n // tm, 3),
        in_specs=[
            pl.BlockSpec((tm, k), lambda i, j: (i, 0)),
            pl.BlockSpec((k, tn), lambda i, j: (0, j)),
            pl.BlockSpec((tm, LANES), lambda i, j: (i, 0)),
            pl.BlockSpec((tm, LANES), lambda i, j: (i, 0)),
        ],
        out_specs=out_specs,
        out_shape=out_shape,
        scratch_shapes=scratch,
        compiler_params=_cparams("parallel", "arbitrary"),
        name="qkv_group_d%d" % dil,
    )(x, w_bf16, cos, sin)


def _band_attn_kernel(sink_ref, q_ref, kc_ref, vc_ref, *refs, has_prev, kv_share, has_sink, with_lse):
    if has_prev:
        kp_ref, vp_ref = refs[:2]
        refs = refs[2:]
    o_ref = refs[0]
    lse_ref = refs[1] if with_lse else None
    n_keys = 2 * BLOCK if has_prev else BLOCK
    j = pl.program_id(2)
    qi = lax.broadcasted_iota(jnp.int32, (BLOCK, n_keys), 0)
    kj = lax.broadcasted_iota(jnp.int32, (BLOCK, n_keys), 1)
    if has_prev:
        first_key = jnp.maximum(qi, jnp.where(j > 0, 0, BLOCK))
        valid = (kj >= first_key) & (kj <= qi + BLOCK)
    else:
        valid = kj <= qi
    lane = lax.broadcasted_iota(jnp.int32, (BLOCK, LANES), 1)
    low = lane < HEAD_DIM

    def keys(cur_ref, prev_ref, kv_head):
        grp = kv_head // 2 if kv_share else kv_head
        lanes = slice(grp * LANES, (grp + 1) * LANES)
        parts = ([prev_ref[:, lanes]] if has_prev else []) + [cur_ref[:, lanes]]
        if kv_share:
            keep = low if kv_head % 2 == 0 else ~low
            parts = [jnp.where(keep, x, pltpu.roll(x, HEAD_DIM, 1)) for x in parts]
        return (jnp.concatenate(parts, axis=0) if has_prev else parts[0]).astype(_BF16)

    for p in range(q_ref.shape[1] // LANES):
        if kv_share == 0 or p % kv_share == 0:
            kv_head = p // kv_share if kv_share else p
            kcat = keys(kc_ref, kp_ref if has_prev else None, kv_head)
            vcat = keys(vc_ref, vp_ref if has_prev else None, kv_head)
        q2 = q_ref[:, p * LANES:(p + 1) * LANES]
        outs, lses = [], []
        for h in range(2):
            qm = jnp.where(low if h == 0 else ~low, q2, 0.0).astype(_BF16)
            s = lax.dot_general(qm, kcat, _NT, preferred_element_type=_F32)
            s = jnp.where(valid, s, _NEG_INF)
            m = jnp.max(s, axis=-1, keepdims=True)
            if has_sink:
                sink = sink_ref[2 * p + h]
                m = jnp.maximum(m, sink)
            e = jnp.exp(s - m)
            den = jnp.sum(e, axis=-1, keepdims=True)
            if has_sink:
                den = den + jnp.exp(sink - m)
            pn = (e / den).astype(_BF16)
            outs.append(jnp.dot(pn, vcat, preferred_element_type=_F32))
            lses.append(m + jnp.log(den))
        o_ref[:, p * LANES:(p + 1) * LANES] = jnp.where(low, outs[0], outs[1])
        if with_lse:
            lse_ref[:, p * LANES:(p + 1) * LANES] = jnp.where(low, lses[0], lses[1])


def _band_attention(sinks, q, kv, *, q_col, k_col, v_col, q_width, kv_width, n_seq, nb, kv_share,
                    with_lse, name):
    has_prev = True
    cur = lambda b, r, j, s: b * nb + j
    prev = lambda b, r, j, s: b * nb + jnp.maximum(j - 1, 0)
    blk = lambda rows, width, col: pl.BlockSpec((BLOCK, width), lambda *a: (rows(*a), col))
    in_specs = [blk(cur, q_width, q_col), blk(cur, kv_width, k_col), blk(cur, kv_width, v_col)]
    args = [q, kv, kv]
    if has_prev:
        in_specs += [blk(prev, kv_width, k_col), blk(prev, kv_width, v_col)]
        args += [kv, kv]
    o_spec = blk(cur, q_width, 0)
    o_sds = jax.ShapeDtypeStruct((n_seq * nb * BLOCK, q_width), _F32)
    return pl.pallas_call(
        functools.partial(_band_attn_kernel, has_prev=has_prev, kv_share=kv_share,
                          has_sink=sinks is not None, with_lse=with_lse),
        grid_spec=pltpu.PrefetchScalarGridSpec(
            num_scalar_prefetch=1,
            grid=(n_seq, 1, nb),
            in_specs=in_specs,
            out_specs=[o_spec, o_spec] if with_lse else [o_spec],
        ),
        out_shape=[o_sds, o_sds] if with_lse else [o_sds],
        compiler_params=_cparams("parallel", "arbitrary", "arbitrary"),
        name=name,
    )(jnp.zeros((1,), _F32) if sinks is None else sinks, *args)


def _band_attention_a(qkv, sinks, n_batch):
    nq = A_HEADS * HEAD_DIM
    nk = A_KV_HEADS * HEAD_DIM
    (o,) = _band_attention(sinks, qkv, qkv, q_col=0, k_col=nq // nk, v_col=nq // nk + 1,
                           q_width=nq, kv_width=nk, n_seq=n_batch, nb=SEQ // BLOCK,
                           kv_share=A_GROUP // 2, with_lse=False, name="band_attn_a")
    return o


def _band_attention_b(q, kv, dil, n_batch):
    gw = B_GROUP_WIDTH
    o, lse = _band_attention(None, q.reshape(-1, gw), kv.reshape(-1, 2 * gw), q_col=0, k_col=0, v_col=1,
                             q_width=gw, kv_width=gw, n_seq=n_batch * dil, nb=SEQ // dil // BLOCK,
                             kv_share=0, with_lse=True, name="band_attn_d%d" % dil)
    if dil > 1:
        o, lse = _token_major(o, dil, n_batch), _token_major(lse, dil, n_batch)
    return o, lse


def _token_major(x, dil, n_batch):
    w = x.shape[1]
    x = x.reshape(n_batch, dil, SEQ // dil, w)
    return jnp.swapaxes(x, 1, 2).reshape(n_batch * SEQ, w)


STEP_NEW_ROWS = 8
STEP_CACHE_BYTES = 8 * 1024 * 1024


STEP_A_SAMPLES = 32


def _step_a_kernel(q_ref, kn_ref, vn_ref, cache_ref, sink_ref, o_ref):
    n_b, n_h, n_rows = q_ref.shape[:3]
    length = cache_ref.shape[-1]

    def step_of(shape):
        return lax.broadcasted_iota(jnp.int32, shape, 0) // A_GROUP

    valid_c = lax.broadcasted_iota(jnp.int32, (n_rows, length), 1) >= step_of((n_rows, length))
    valid_n = (lax.broadcasted_iota(jnp.int32, (n_rows, STEP_NEW_ROWS), 1)
               <= step_of((n_rows, STEP_NEW_ROWS)))

    def slot(i, carry):
        b, h = i // n_h, i % n_h
        q = q_ref[b, h].astype(_BF16)
        s_c = jnp.dot(q, cache_ref[b, 0, h].astype(_BF16), preferred_element_type=_F32)
        s_n = lax.dot_general(q, kn_ref[b, h].astype(_BF16), _NT, preferred_element_type=_F32)
        s_c = jnp.where(valid_c, s_c, _NEG_INF)
        s_n = jnp.where(valid_n, s_n, _NEG_INF)
        sink = sink_ref[h]
        m = jnp.maximum(jnp.max(s_c, axis=-1, keepdims=True), jnp.max(s_n, axis=-1, keepdims=True))
        m = jnp.maximum(m, sink)
        e_c = jnp.exp(s_c - m)
        e_n = jnp.exp(s_n - m)
        den = (jnp.sum(e_c, axis=-1, keepdims=True) + jnp.sum(e_n, axis=-1, keepdims=True)
               + jnp.exp(sink - m))
        o = lax.dot_general((e_c / den).astype(_BF16), cache_ref[b, 1, h].astype(_BF16), _NT,
                            preferred_element_type=_F32)
        o_ref[b, h] = o + jnp.dot((e_n / den).astype(_BF16), vn_ref[b, h].astype(_BF16),
                                  preferred_element_type=_F32)
        return carry

    lax.fori_loop(0, n_b * n_h, slot, 0, unroll=8)


def _slots(x, db, n_heads):
    x = x.reshape(db, DEC_SEQ, n_heads, HEAD_DIM).transpose(0, 2, 1, 3)
    return jnp.pad(x, [(0, 0), (0, 0), (0, STEP_NEW_ROWS - DEC_SEQ), (0, 0)])


def _step_b_kernel(q_ref, kn_ref, vn_ref, cache_ref, o_ref, lse_ref, *, dil):
    n_h, length = cache_ref.shape[2], cache_ref.shape[4]
    width = n_h * HEAD_DIM
    n_rows = n_h * STEP_NEW_ROWS
    seg = lax.broadcasted_iota(jnp.int32, (STEP_NEW_ROWS, width), 1) // HEAD_DIM

    def per_head(fn):
        return [fn(h) for h in range(n_h)]

    def own_head(blocks):
        return functools.reduce(jnp.add, [jnp.where(seg == h, blk, 0.0) for h, blk in enumerate(blocks)])

    q_tok = q_ref[0]
    q_bd = jnp.concatenate(per_head(lambda h: jnp.where(seg == h, q_tok, 0.0)), axis=0).astype(_BF16)
    kt = cache_ref[0, 0].reshape(width, length).astype(_BF16)
    vt = cache_ref[0, 1].reshape(width, length).astype(_BF16)
    s_c = jnp.dot(q_bd, kt, preferred_element_type=_F32)
    s_n = lax.dot_general(q_bd, kn_ref[0].astype(_BF16), _NT, preferred_element_type=_F32)

    def step_of(shape):
        return (lax.broadcasted_iota(jnp.int32, shape, 0) % STEP_NEW_ROWS) % DEC_SEQ

    col_c = lax.broadcasted_iota(jnp.int32, (n_rows, length), 1)
    col_n = lax.broadcasted_iota(jnp.int32, (n_rows, STEP_NEW_ROWS), 1)
    if dil == 1:
        valid_c = col_c >= step_of((n_rows, length))
        valid_n = col_n <= step_of((n_rows, STEP_NEW_ROWS))
    else:
        valid_c = (col_c % dil) == step_of((n_rows, length))
        valid_n = col_n == step_of((n_rows, STEP_NEW_ROWS))
    s_c = jnp.where(valid_c, s_c, _NEG_INF)
    s_n = jnp.where(valid_n, s_n, _NEG_INF)
    m = jnp.maximum(jnp.max(s_c, axis=-1, keepdims=True), jnp.max(s_n, axis=-1, keepdims=True))
    e_c = jnp.exp(s_c - m)
    e_n = jnp.exp(s_n - m)
    den = jnp.sum(e_c, axis=-1, keepdims=True) + jnp.sum(e_n, axis=-1, keepdims=True)
    o_all = lax.dot_general((e_c / den).astype(_BF16), vt, _NT, preferred_element_type=_F32)
    o_all = o_all + jnp.dot((e_n / den).astype(_BF16), vn_ref[0].astype(_BF16),
                            preferred_element_type=_F32)
    rows = lambda x, h: x[h * STEP_NEW_ROWS:(h + 1) * STEP_NEW_ROWS]
    o_ref[0] = own_head(per_head(lambda h: rows(o_all, h)))
    lse = m + jnp.log(den)
    lse_ref[0] = own_head(per_head(lambda h: jnp.broadcast_to(rows(lse, h), (STEP_NEW_ROWS, width))))


def _step_b(q_s, kv_s, cache, dil):
    db, length = cache.shape[:2]
    gw = B_GROUP_WIDTH
    cache_t = cache.transpose(0, 2, 3, 4, 1)
    hb = min(B_HEADS, max(1, STEP_CACHE_BYTES // (2 * HEAD_DIM * length * 4)))
    pad = lambda x: jnp.pad(x.reshape(db, DEC_SEQ, gw), [(0, 0), (0, STEP_NEW_ROWS - DEC_SEQ), (0, 0)])
    tok_spec = pl.BlockSpec((1, STEP_NEW_ROWS, hb * HEAD_DIM), lambda b, j: (b, 0, j))
    tok_sds = jax.ShapeDtypeStruct((db, STEP_NEW_ROWS, gw), _F32)
    o, lse = pl.pallas_call(
        functools.partial(_step_b_kernel, dil=dil),
        grid=(db, B_HEADS // hb),
        in_specs=[tok_spec, tok_spec, tok_spec,
                  pl.BlockSpec((1, 2, hb, HEAD_DIM, length), lambda b, j: (b, 0, j, 0, 0))],
        out_specs=[tok_spec, tok_spec],
        out_shape=[tok_sds, tok_sds],
        compiler_params=_cparams("parallel", "parallel"),
        name="step_b_l%d" % length,
    )(pad(q_s), pad(kv_s[:, :gw]), pad(kv_s[:, gw:]), cache_t)
    back = lambda x: x[:, :DEC_SEQ].reshape(db * DEC_SEQ, gw)
    return back(o), back(lse)


def _step_a(qkv_s, cache, sinks, db):
    nq = A_HEADS * HEAD_DIM
    nk = A_KV_HEADS * HEAD_DIM
    q = qkv_s[:, :nq].reshape(db, DEC_SEQ, A_KV_HEADS, A_GROUP, HEAD_DIM)
    q = q.transpose(0, 2, 1, 3, 4).reshape(db, A_KV_HEADS, DEC_SEQ * A_GROUP, HEAD_DIM)
    sink_rows = jnp.tile(sinks.reshape(A_KV_HEADS, 1, A_GROUP), (1, DEC_SEQ, 1))
    sink_rows = sink_rows.reshape(A_KV_HEADS, DEC_SEQ * A_GROUP, 1)
    bb = STEP_A_SAMPLES
    q_spec = pl.BlockSpec((bb,) + q.shape[1:], lambda i: (i, 0, 0, 0))
    n_spec = pl.BlockSpec((bb, A_KV_HEADS, STEP_NEW_ROWS, HEAD_DIM), lambda i: (i, 0, 0, 0))
    o = pl.pallas_call(
        _step_a_kernel,
        grid=(db // bb,),
        in_specs=[q_spec, n_spec, n_spec,
                  pl.BlockSpec((bb, 2, A_KV_HEADS, HEAD_DIM, A_WINDOW), lambda i: (i, 0, 0, 0, 0)),
                  pl.BlockSpec(sink_rows.shape, lambda i: (0, 0, 0))],
        out_specs=q_spec,
        out_shape=jax.ShapeDtypeStruct(q.shape, _F32),
        compiler_params=_cparams("parallel"),
        name="step_a",
    )(q, _slots(qkv_s[:, nq:nq + nk], db, A_KV_HEADS), _slots(qkv_s[:, nq + nk:], db, A_KV_HEADS),
      cache.transpose(0, 2, 3, 4, 1), sink_rows)
    o = o.reshape(db, A_KV_HEADS, DEC_SEQ, A_GROUP, HEAD_DIM).transpose(0, 2, 1, 3, 4)
    return o.reshape(db * DEC_SEQ, nq)


def _layer_norm_rows(z, g, b):
    mu = jnp.mean(z, axis=-1, keepdims=True)
    zc = z - mu
    var = jnp.mean(zc * zc, axis=-1, keepdims=True)
    return zc * lax.rsqrt(var + LN_EPS) * g + b


def _proj_ln_kernel(*refs, n_merge, has_bias):
    if n_merge:
        o_refs = refs[:n_merge]
        l_refs = refs[n_merge:2 * n_merge]
        pos = 2 * n_merge
        ls = [r[...] for r in l_refs]
        m = functools.reduce(jnp.maximum, ls)
        es = [jnp.exp(l - m) for l in ls]
        den = functools.reduce(jnp.add, es)
        o = functools.reduce(jnp.add, [(e / den) * r[...] for e, r in zip(es, o_refs)])
    else:
        o = refs[0][...]
        pos = 1
    w_ref = refs[pos]
    pos += 1
    y = jnp.dot(o.astype(_BF16), w_ref[...], preferred_element_type=_F32)
    if has_bias:
        y = y + refs[pos][...]
        pos += 1
    x_ref, g_ref, b_ref, out_ref = refs[pos:pos + 4]
    out_ref[...] = _layer_norm_rows(DN_ALPHA * x_ref[...] + y, g_ref[...], b_ref[...])


def _proj_ln(o_list, lse_list, w_bf16, bias, x, ln_g, ln_b, tm=256):
    n = x.shape[0]
    k = w_bf16.shape[0]
    row = lambda i: (i, 0)
    fixed = lambda i: (0, 0)
    args = list(o_list) + list(lse_list) + [w_bf16]
    specs = [pl.BlockSpec((tm, k), row)] * (len(o_list) + len(lse_list))
    specs.append(pl.BlockSpec((k, D_MODEL), fixed))
    if bias is not None:
        args.append(bias)
        specs.append(pl.BlockSpec((1, D_MODEL), fixed))
    args += [x, ln_g, ln_b]
    specs += [pl.BlockSpec((tm, D_MODEL), row), pl.BlockSpec((1, D_MODEL), fixed),
              pl.BlockSpec((1, D_MODEL), fixed)]
    return pl.pallas_call(
        functools.partial(_proj_ln_kernel, n_merge=len(lse_list), has_bias=bias is not None),
        grid=(n // tm,),
        in_specs=specs,
        out_specs=pl.BlockSpec((tm, D_MODEL), row),
        out_shape=jax.ShapeDtypeStruct((n, D_MODEL), _F32),
        compiler_params=_cparams("parallel"),
        name="proj_ln",
    )(*args)


N_KEY_BLOCKS = PEER_HEADS * 2
HALF_KEY_DIM = PEER_KEY_DIM // 2


def _peer_score_kernel(x_ref, wq_ref, keys_ref, sc_ref):
    q = jnp.dot(x_ref[...].astype(_BF16), wq_ref[...], preferred_element_type=_F32).astype(_BF16)
    for blk in range(N_KEY_BLOCKS):
        sc_ref[blk * PEER_N_KEYS:(blk + 1) * PEER_N_KEYS, :] = lax.dot_general(
            keys_ref[blk], q[:, blk * HALF_KEY_DIM:(blk + 1) * HALF_KEY_DIM], _NT,
            preferred_element_type=_F32)


def _peer_scores(x, wq_bf16, keys_bf16):
    n = x.shape[0]
    tm = TOKEN_TILE
    return pl.pallas_call(
        _peer_score_kernel,
        grid=(n // tm,),
        in_specs=[
            pl.BlockSpec((tm, D_MODEL), lambda i: (i, 0)),
            pl.BlockSpec((D_MODEL, PEER_HEADS * PEER_KEY_DIM), lambda i: (0, 0)),
            pl.BlockSpec((N_KEY_BLOCKS, PEER_N_KEYS, HALF_KEY_DIM), lambda i: (0, 0, 0)),
        ],
        out_specs=pl.BlockSpec((N_KEY_BLOCKS * PEER_N_KEYS, tm), lambda i: (0, i)),
        out_shape=jax.ShapeDtypeStruct((N_KEY_BLOCKS * PEER_N_KEYS, n), _F32),
        compiler_params=_cparams("parallel"),
        name="peer_scores",
    )(x, wq_bf16, keys_bf16)


TOPK_TOKENS = 128


def _top1_rows(vals, ids, big):
    m = jnp.max(vals, axis=0, keepdims=True)
    sel = jnp.min(jnp.where(vals == m, ids, big), axis=0, keepdims=True)
    return m, sel, ids == sel


def _peer_topk_kernel(sc_ref, e_ref, gate_ref, top_s, top_i, sel_e, sel_g):
    tt = sc_ref.shape[1]
    key_id = lax.broadcasted_iota(jnp.int32, (PEER_N_KEYS, tt), 0).astype(_F32)

    def stage1(blk, carry):
        s = sc_ref[pl.ds(pl.multiple_of(blk * PEER_N_KEYS, PEER_N_KEYS), PEER_N_KEYS), :]
        for k in range(PEER_TOPK):
            m, sel, onehot = _top1_rows(s, key_id, float(PEER_N_KEYS))
            top_s[blk, k:k + 1, :] = m
            top_i[blk, k:k + 1, :] = sel
            s = jnp.where(onehot, _NEG_INF, s)
        return carry

    lax.fori_loop(0, N_KEY_BLOCKS, stage1, 0, unroll=2)

    sub = lax.broadcasted_iota(jnp.int32, (SUBLANES, tt), 0).astype(_F32)
    flat_id = jnp.concatenate(
        [sub, sub + SUBLANES] + [sub + a * PEER_TOPK for a in range(1, SUBLANES)]
        + [(sub + SUBLANES) * PEER_TOPK], axis=0)

    def bcast(row):
        return jnp.broadcast_to(row, (SUBLANES, tt))

    def stage2(h, carry):
        s0, s1 = top_s[2 * h], top_s[2 * h + 1]
        i0, i1 = top_i[2 * h] * PEER_N_KEYS, top_i[2 * h + 1]
        lo_s, hi_s, lo_i, hi_i = s1[:SUBLANES], s1[SUBLANES:], i1[:SUBLANES], i1[SUBLANES:]
        cand = jnp.concatenate(
            [bcast(s0[0:1]) + lo_s, bcast(s0[0:1]) + hi_s]
            + [bcast(s0[a:a + 1]) + lo_s for a in range(1, SUBLANES)]
            + [s0[SUBLANES:] + bcast(s1[0:1])], axis=0)
        cand_e = jnp.concatenate(
            [bcast(i0[0:1]) + lo_i, bcast(i0[0:1]) + hi_i]
            + [bcast(i0[a:a + 1]) + lo_i for a in range(1, SUBLANES)]
            + [i0[SUBLANES:] + bcast(i1[0:1])], axis=0)
        best = []
        for k in range(PEER_TOPK):
            m, _, onehot = _top1_rows(cand, flat_id, float(PEER_TOPK * PEER_TOPK))
            best.append(m)
            sel_e[h, k:k + 1, :] = jnp.max(jnp.where(onehot, cand_e, -1.0), axis=0, keepdims=True)
            cand = jnp.where(onehot, _NEG_INF, cand)
        ex = [jnp.exp(b - best[0]) for b in best]
        den = functools.reduce(jnp.add, ex)
        for k in range(PEER_TOPK):
            sel_g[h, k:k + 1, :] = ex[k] / den
        return carry

    lax.fori_loop(0, PEER_HEADS, stage2, 0, unroll=2)
    n_sel = PEER_HEADS * PEER_TOPK
    e_ref[...] = sel_e[...].reshape(n_sel, tt).T.astype(jnp.int32)
    gate_ref[...] = sel_g[...].reshape(n_sel, tt).T


def _peer_topk(sc_t):
    n = sc_t.shape[1]
    tt = TOPK_TOKENS
    n_sel = PEER_HEADS * PEER_TOPK
    return pl.pallas_call(
        _peer_topk_kernel,
        grid=(n // tt,),
        in_specs=[pl.BlockSpec((N_KEY_BLOCKS * PEER_N_KEYS, tt), lambda i: (0, i))],
        out_specs=[pl.BlockSpec((tt, n_sel), lambda i: (i, 0)),
                   pl.BlockSpec((tt, n_sel), lambda i: (i, 0))],
        out_shape=[jax.ShapeDtypeStruct((n, n_sel), jnp.int32),
                   jax.ShapeDtypeStruct((n, n_sel), _F32)],
        scratch_shapes=[pltpu.VMEM((N_KEY_BLOCKS, PEER_TOPK, tt), _F32),
                        pltpu.VMEM((N_KEY_BLOCKS, PEER_TOPK, tt), _F32),
                        pltpu.VMEM((PEER_HEADS, PEER_TOPK, tt), _F32),
                        pltpu.VMEM((PEER_HEADS, PEER_TOPK, tt), _F32)],
        compiler_params=_cparams("parallel"),
        name="peer_topk",
    )(sc_t)


GATE_TOKENS = 64
GATE_UNROLL = 16


def _peer_gate_kernel(e_ref, gate_ref, g_ref, scr):
    tb, _, n_sel = e_ref.shape
    row_id = lax.broadcasted_iota(jnp.int32, (PEER_N_KEYS, n_sel), 0)

    def body(n, carry):
        e = e_ref[n]
        gate = gate_ref[n]
        hi = gate.astype(_BF16).astype(_F32)
        lo = gate - hi
        is_i = row_id == (e >> 7)
        lhs = jnp.concatenate([jnp.where(is_i, hi, 0.0), jnp.where(is_i, lo, 0.0)], axis=1)
        one = jnp.where(row_id == (e & (PEER_N_KEYS - 1)), 1.0, 0.0)
        rhs = jnp.concatenate([one, one], axis=1)
        res = lax.dot_general(lhs.astype(_BF16), rhs.astype(_BF16), _NT, preferred_element_type=_F32)
        for ib in range(PEER_N_KEYS // SUBLANES):
            start = pl.multiple_of((ib * tb + n) * SUBLANES, SUBLANES)
            scr[pl.ds(start, SUBLANES), :] = res[ib * SUBLANES:(ib + 1) * SUBLANES]
        return carry

    lax.fori_loop(0, tb, body, 0, unroll=GATE_UNROLL)
    for i in range(PEER_N_KEYS):
        start = (i // SUBLANES) * tb * SUBLANES + i % SUBLANES
        g_ref[:, i * PEER_N_KEYS:(i + 1) * PEER_N_KEYS] = scr[pl.ds(start, tb, stride=SUBLANES), :]


def _peer_gates(expert, gate):
    n, n_sel = expert.shape
    tb = GATE_TOKENS
    expert = expert.reshape(n, 1, n_sel)
    gate = gate.reshape(n, 1, n_sel)
    return pl.pallas_call(
        _peer_gate_kernel,
        grid=(n // tb,),
        in_specs=[pl.BlockSpec((tb, 1, n_sel), lambda i: (i, 0, 0)),
                  pl.BlockSpec((tb, 1, n_sel), lambda i: (i, 0, 0))],
        out_specs=pl.BlockSpec((tb, PEER_N_EXPERTS), lambda i: (i, 0)),
        out_shape=jax.ShapeDtypeStruct((n, PEER_N_EXPERTS), _F32),
        scratch_shapes=[pltpu.VMEM((tb * PEER_N_KEYS, PEER_N_KEYS), _F32)],
        compiler_params=_cparams("parallel"),
        name="peer_gates",
    )(expert, gate)


EXPERT_CHUNK = 1024
_SQRT_HALF = math.sqrt(0.5)


def _peer_main_kernel(x_ref, ut_ref, g_ref, v_ref, lng_ref, lnb_ref, o_ref, xb_scr, acc_scr):
    c = pl.program_id(1)

    @pl.when(c == 0)
    def _():
        xb_scr[...] = x_ref[...].astype(_BF16)
        acc_scr[...] = jnp.zeros_like(acc_scr)

    h_parts = []
    for k in range(ut_ref.shape[1] // MXU_COLS):
        cols = slice(k * MXU_COLS, (k + 1) * MXU_COLS)
        a = jnp.dot(xb_scr[...], ut_ref[:, cols], preferred_element_type=_F32)
        gelu = 0.5 * a * (1.0 + lax.erf(a * _SQRT_HALF))
        h_parts.append((g_ref[:, cols] * gelu).astype(_BF16))
    h = jnp.concatenate(h_parts, axis=1)
    acc_scr[...] += jnp.dot(h, v_ref[...], preferred_element_type=_F32)

    @pl.when(c == pl.num_programs(1) - 1)
    def _():
        o_ref[...] = _layer_norm_rows(DN_ALPHA * x_ref[...] + acc_scr[...], lng_ref[...], lnb_ref[...])


def _peer_main(x, ut_bf16, gates, v_bf16, ln_g, ln_b):
    n = x.shape[0]
    tm, ce = TOKEN_TILE, EXPERT_CHUNK
    return pl.pallas_call(
        _peer_main_kernel,
        grid=(n // tm, PEER_N_EXPERTS // ce),
        in_specs=[
            pl.BlockSpec((tm, D_MODEL), lambda i, c: (i, 0)),
            pl.BlockSpec((D_MODEL, ce), lambda i, c: (0, c)),
            pl.BlockSpec((tm, ce), lambda i, c: (i, c)),
            pl.BlockSpec((ce, D_MODEL), lambda i, c: (c, 0)),
            pl.BlockSpec((1, D_MODEL), lambda i, c: (0, 0)),
            pl.BlockSpec((1, D_MODEL), lambda i, c: (0, 0)),
        ],
        out_specs=pl.BlockSpec((tm, D_MODEL), lambda i, c: (i, 0)),
        out_shape=jax.ShapeDtypeStruct((n, D_MODEL), _F32),
        scratch_shapes=[pltpu.VMEM((tm, D_MODEL), _BF16), pltpu.VMEM((tm, D_MODEL), _F32)],
        compiler_params=_cparams("parallel", "arbitrary"),
        name="peer_main",
    )(x, ut_bf16, gates, v_bf16, ln_g, ln_b)


def _peer_ffn_ln(x, wq_bf16, keys_bf16, ut_bf16, v_bf16, ln_g, ln_b):
    sc_t = _peer_scores(x, wq_bf16, keys_bf16)
    expert, gate = _peer_topk(sc_t)
    gates = _peer_gates(expert, gate)
    return _peer_main(x, ut_bf16, gates, v_bf16, ln_g, ln_b)


def _rope_tables(pos):
    half = HEAD_DIM // 2
    inv = ROPE_THETA ** (-jnp.arange(half, dtype=_F32) / half)
    ang = pos.astype(_F32)[:, None] * inv[None, :]
    cos, sin = jnp.cos(ang), jnp.sin(ang)
    cos = jnp.tile(cos, (1, LANES // half))
    sin = jnp.tile(jnp.concatenate([-sin, sin], axis=1), (1, LANES // HEAD_DIM))
    return cos, sin


def kernel(x_prompt, x_sample, cache_a_kv, cache_b1_kv, cache_b2_kv, cache_b3_kv, w_qkv_a, b_qkv_a, sinks_a, w_o_a, b_o_a, w_qkv_b, w_o_b, ln_mix_g, ln_mix_b, ln_ffn_g, ln_ffn_b, peer_w_query, peer_sub_keys, peer_u, peer_v):
    n_batch, seq = x_prompt.shape[:2]
    db, t_len = x_sample.shape[:2]
    xp = x_prompt.reshape(n_batch * seq, D_MODEL)
    xs = x_sample.reshape(db * t_len, D_MODEL)
    cos_p, sin_p = _rope_tables(jnp.tile(jnp.arange(seq, dtype=jnp.int32), n_batch))
    cos_s, sin_s = _rope_tables(jnp.tile(PAST_LEN + jnp.arange(t_len, dtype=jnp.int32), db))
    scale = HEAD_DIM ** -0.5
    row = lambda v: v.reshape(1, -1)
    caches_b = (cache_b1_kv, cache_b2_kv, cache_b3_kv)
    outs_kv = {}

    for layer in range(DEPTH):
        i = layer // 2
        ln_g, ln_b = row(ln_mix_g[layer]), row(ln_mix_b[layer])
        if layer % 2 == 0:
            nq = A_HEADS * HEAD_DIM
            nk = A_KV_HEADS * HEAD_DIM
            col_scale = jnp.concatenate([jnp.full((nq,), scale, _F32), jnp.ones((2 * nk,), _F32)])
            w = (w_qkv_a[i] * col_scale).astype(_BF16)
            bias = row(b_qkv_a[i] * col_scale)
            qkv_p = _qkv_rope(xp, w, bias, cos_p, sin_p, nq + nk)
            qkv_s = _qkv_rope(xs, w, bias, cos_s, sin_s, nq + nk)
            o_p = _band_attention_a(qkv_p, sinks_a[i], n_batch)
            o_s = _step_a(qkv_s, cache_a_kv[i], sinks_a[i], db)
            wo = w_o_a[i].astype(_BF16)
            xp = _proj_ln([o_p], [], wo, row(b_o_a[i]), xp, ln_g, ln_b)
            xs = _proj_ln([o_s], [], wo, row(b_o_a[i]), xs, ln_g, ln_b)
            keep = min(A_WINDOW, seq)
            kv_p = qkv_p.reshape(n_batch, seq, A_QKV)[:, seq - keep:, nq:]
            outs_kv.setdefault("a_p", []).append(kv_p.reshape(n_batch, keep, 2, A_KV_HEADS, HEAD_DIM))
            outs_kv.setdefault("a_s", []).append(qkv_s[:, nq:].reshape(db, t_len, 2, A_KV_HEADS, HEAD_DIM))
        else:
            gw = B_GROUP_WIDTH
            w3 = w_qkv_b[i].reshape(D_MODEL, 3, N_B_GROUPS, gw)
            o_ps, l_ps, o_ss, l_ss = [], [], [], []
            for g, (window, dil) in enumerate(B_GROUPS):
                w = jnp.concatenate([w3[:, 0, g] * scale, w3[:, 1, g], w3[:, 2, g]], axis=1).astype(_BF16)
                outs = _qkv_group(xp, w, cos_p, sin_p, dil)
                q_p, kv_p, kvr_p = outs if dil > 1 else (outs[0], outs[1], outs[1])
                q_s, kv_s = _qkv_group(xs, w, cos_s, sin_s, 1)
                o, l = _band_attention_b(q_p, kvr_p, dil, n_batch)
                o_ps.append(o)
                l_ps.append(l)
                kv_s5 = kv_s.reshape(db, t_len, 2, B_HEADS, HEAD_DIM)
                o, l = _step_b(q_s, kv_s, caches_b[g][i], dil)
                o_ss.append(o)
                l_ss.append(l)
                keep = min(window, seq)
                kv_keep = kv_p.reshape(n_batch, seq, 2 * gw)[:, seq - keep:]
                outs_kv.setdefault("b%d_p" % g, []).append(
                    kv_keep.reshape(n_batch, keep, 2, B_HEADS, HEAD_DIM))
                outs_kv.setdefault("b%d_s" % g, []).append(kv_s5)
            wo = w_o_b[i].astype(_BF16)
            xp = _proj_ln(o_ps, l_ps, wo, None, xp, ln_g, ln_b)
            xs = _proj_ln(o_ss, l_ss, wo, None, xs, ln_g, ln_b)

        wq = peer_w_query[layer].astype(_BF16)
        keys = peer_sub_keys[layer].reshape(N_KEY_BLOCKS, PEER_N_KEYS, HALF_KEY_DIM).astype(_BF16)
        ut = peer_u[layer].T.astype(_BF16)
        vt = peer_v[layer].astype(_BF16)
        xp = _peer_ffn_ln(xp, wq, keys, ut, vt, row(ln_ffn_g[layer]), row(ln_ffn_b[layer]))
        xs = _peer_ffn_ln(xs, wq, keys, ut, vt, row(ln_ffn_g[layer]), row(ln_ffn_b[layer]))

    return (xp.reshape(n_batch, seq, D_MODEL), xs.reshape(db, t_len, D_MODEL),
            jnp.stack(outs_kv["a_p"]), jnp.stack(outs_kv["a_s"]),
            jnp.stack(outs_kv["b0_p"]), jnp.stack(outs_kv["b0_s"]),
            jnp.stack(outs_kv["b1_p"]), jnp.stack(outs_kv["b1_s"]),
            jnp.stack(outs_kv["b2_p"]), jnp.stack(outs_kv["b2_s"]))
```

```python
import functools
import math

import jax
import jax.numpy as jnp
from jax import lax
from jax.experimental import pallas as pl
from jax.experimental.pallas import tpu as pltpu

D_MODEL = 2048
SEQ = 2048
DEPTH = 2
DEC_SEQ = 4
PAST_LEN = 8192
HEAD_DIM = 64
ROPE_THETA = 10000.0
BLOCK = 128
A_HEADS = D_MODEL // HEAD_DIM
A_KV_HEADS = A_HEADS // 8
A_GROUP = A_HEADS // A_KV_HEADS
A_WINDOW = 128
A_QKV = (A_HEADS + 2 * A_KV_HEADS) * HEAD_DIM
B_GROUPS = ((128, 1), (512, 4), (2048, 16))
N_B_GROUPS = len(B_GROUPS)
B_HEADS = D_MODEL // (2 * HEAD_DIM)
B_GROUP_WIDTH = B_HEADS * HEAD_DIM
B_KEYS = 128
PEER_HEADS = 8
PEER_N_KEYS = 128
PEER_N_EXPERTS = PEER_N_KEYS * PEER_N_KEYS
PEER_TOPK = 16
PEER_KEY_DIM = 256
DN_ALPHA = (2.0 * DEPTH) ** 0.25
LN_EPS = 1e-5

LANES = 128
SUBLANES = 8
TOKEN_TILE = 512
VMEM_LIMIT = 56 * 1024 * 1024

_F32 = jnp.float32
_BF16 = jnp.bfloat16
_NEG_INF = float("-inf")
_NT = (((1,), (1,)), ((), ()))


def _cparams(*sem):
    return pltpu.CompilerParams(dimension_semantics=sem, vmem_limit_bytes=VMEM_LIMIT)


MXU_COLS = 256


def _rope(blk, cos, sin):
    lane = lax.broadcasted_iota(jnp.int32, blk.shape, 1)
    first_half = (lane % HEAD_DIM) < (HEAD_DIM // 2)
    partner = jnp.where(first_half, pltpu.roll(blk, LANES - HEAD_DIM // 2, 1),
                        pltpu.roll(blk, HEAD_DIM // 2, 1))
    return blk * cos + partner * sin


def _project_chunks(xb, w_ref, emit, bias_ref=None):
    for c in range(w_ref.shape[1] // MXU_COLS):
        cols = slice(c * MXU_COLS, (c + 1) * MXU_COLS)
        acc = jnp.dot(xb, w_ref[:, cols], preferred_element_type=_F32)
        if bias_ref is not None:
            acc = acc + bias_ref[:, cols]
        for h in range(MXU_COLS // LANES):
            emit(c * (MXU_COLS // LANES) + h, acc[:, h * LANES:(h + 1) * LANES])


def _qkv_rope_kernel(x_ref, w_ref, b_ref, cos_ref, sin_ref, o_ref, *, n_rope_groups):
    def emit(g, blk):
        if g < n_rope_groups:
            blk = _rope(blk, cos_ref[...], sin_ref[...])
        o_ref[:, g * LANES:(g + 1) * LANES] = blk

    _project_chunks(x_ref[...].astype(_BF16), w_ref, emit, b_ref)


def _qkv_rope(x, w_bf16, bias, cos, sin, n_rope_cols):
    n, k = x.shape
    ncols = w_bf16.shape[1]
    tm = TOKEN_TILE
    return pl.pallas_call(
        functools.partial(_qkv_rope_kernel, n_rope_groups=n_rope_cols // LANES),
        grid=(n // tm,),
        in_specs=[
            pl.BlockSpec((tm, k), lambda i: (i, 0)),
            pl.BlockSpec((k, ncols), lambda i: (0, 0)),
            pl.BlockSpec((1, ncols), lambda i: (0, 0)),
            pl.BlockSpec((tm, LANES), lambda i: (i, 0)),
            pl.BlockSpec((tm, LANES), lambda i: (i, 0)),
        ],
        out_specs=pl.BlockSpec((tm, ncols), lambda i: (i, 0)),
        out_shape=jax.ShapeDtypeStruct((n, ncols), _F32),
        compiler_params=_cparams("parallel"),
        name="qkv_rope",
    )(x, w_bf16, bias, cos, sin)


def _qkv_group_kernel(x_ref, w_ref, cos_ref, sin_ref, *refs, dil):
    j = pl.program_id(1)
    if dil == 1:
        q_ref, kv_ref = refs
    else:
        q_ref, kv_ref, kvr_ref, scr = refs
    xb = x_ref[...].astype(_BF16)
    per_residue = x_ref.shape[0] // dil

    def emitter(token_ref, residue_ref, rotate):
        def emit(g, blk):
            lanes = slice(g * LANES, (g + 1) * LANES)
            if rotate:
                blk = _rope(blk, cos_ref[...], sin_ref[...])
            if token_ref is not None:
                token_ref[:, lanes] = blk.astype(token_ref.dtype)
            if residue_ref is not None:
                scr[g] = blk
                for r in range(dil):
                    residue_ref[r, :, lanes] = (
                        scr[g, pl.ds(r, per_residue, stride=dil), :].astype(residue_ref.dtype))
        return emit

    @pl.when(j == 0)
    def _():
        _project_chunks(xb, w_ref, emitter(q_ref if dil == 1 else None,
                                           q_ref if dil > 1 else None, True))

    @pl.when(j == 1)
    def _():
        _project_chunks(xb, w_ref, emitter(kv_ref, kvr_ref if dil > 1 else None, True))

    @pl.when(j == 2)
    def _():
        _project_chunks(xb, w_ref, emitter(kv_ref, kvr_ref if dil > 1 else None, False))


def _qkv_group(x, w_bf16, cos, sin, dil, group):
    n, k = x.shape
    tm, tn = TOKEN_TILE, B_GROUP_WIDTH
    kv_col = lambda j: jnp.maximum(j - 1, 0)
    kv_spec = pl.BlockSpec((tm, tn), lambda i, j: (i, kv_col(j)))
    kv_sds = jax.ShapeDtypeStruct((n, 2 * tn), _F32)
    if dil == 1:
        out_specs = [pl.BlockSpec((tm, tn), lambda i, j: (i, 0)), kv_spec]
        out_shape = [jax.ShapeDtypeStruct((n, tn), _BF16), kv_sds]
        scratch = []
    else:
        tiles = SEQ // tm
        res_block = (dil, tm // dil, tn)
        res_rows = (n // SEQ * dil, SEQ // dil)
        out_specs = [pl.BlockSpec(res_block, lambda i, j: (i // tiles, i % tiles, 0)), kv_spec,
                     pl.BlockSpec(res_block, lambda i, j: (i // tiles, i % tiles, kv_col(j)))]
        out_shape = [jax.ShapeDtypeStruct(res_rows + (tn,), _BF16), kv_sds,
                     jax.ShapeDtypeStruct(res_rows + (2 * tn,), _BF16)]
        scratch = [pltpu.VMEM((tn // LANES, tm, LANES), _F32)]
    return pl.pallas_call(
        functools.partial(_qkv_group_kernel, dil=dil),
        grid=(n // tm, 3),
        in_specs=[
            pl.BlockSpec((tm, k), lambda i, j: (i, 0)),
            pl.BlockSpec((k, tn), lambda i, j: (0, j * N_B_GROUPS + group)),
            pl.BlockSpec((tm, LANES), lambda i, j: (i, 0)),
            pl.BlockSpec((tm, LANES), lambda i, j: (i, 0)),
        ],
        out_specs=out_specs,
        out_shape=out_shape,
        scratch_shapes=scratch,
        compiler_params=_cparams("parallel", "arbitrary"),
        name="qkv_group_d%d" % dil,
    )(x, w_bf16, cos, sin)


def _band_attn_kernel(sink_ref, q_ref, kc_ref, vc_ref, *refs, has_prev, kv_share, has_sink, with_lse):
    if has_prev:
        kp_ref, vp_ref = refs[:2]
        refs = refs[2:]
    o_ref = refs[0]
    lse_ref = refs[1] if with_lse else None
    n_keys = 2 * BLOCK if has_prev else BLOCK
    j = pl.program_id(2)
    qi = lax.broadcasted_iota(jnp.int32, (BLOCK, n_keys), 0)
    kj = lax.broadcasted_iota(jnp.int32, (BLOCK, n_keys), 1)
    if has_prev:
        first_key = jnp.maximum(qi, jnp.where(j > 0, 0, BLOCK))
        valid = (kj >= first_key) & (kj <= qi + BLOCK)
    else:
        valid = kj <= qi
    lane = lax.broadcasted_iota(jnp.int32, (BLOCK, LANES), 1)
    low = lane < HEAD_DIM

    def keys(cur_ref, prev_ref, kv_head):
        grp = kv_head // 2 if kv_share else kv_head
        lanes = slice(grp * LANES, (grp + 1) * LANES)
        parts = ([prev_ref[:, lanes]] if has_prev else []) + [cur_ref[:, lanes]]
        if kv_share:
            keep = low if kv_head % 2 == 0 else ~low
            parts = [jnp.where(keep, x, pltpu.roll(x, HEAD_DIM, 1)) for x in parts]
        return (jnp.concatenate(parts, axis=0) if has_prev else parts[0]).astype(_BF16)

    for p in range(q_ref.shape[1] // LANES):
        if kv_share == 0 or p % kv_share == 0:
            kv_head = p // kv_share if kv_share else p
            kcat = keys(kc_ref, kp_ref if has_prev else None, kv_head)
            vcat = keys(vc_ref, vp_ref if has_prev else None, kv_head)
        q2 = q_ref[:, p * LANES:(p + 1) * LANES]
        outs, lses = [], []
        for h in range(2):
            qm = jnp.where(low if h == 0 else ~low, q2, 0.0).astype(_BF16)
            s = lax.dot_general(qm, kcat, _NT, preferred_element_type=_F32)
            s = jnp.where(valid, s, _NEG_INF)
            m = jnp.max(s, axis=-1, keepdims=True)
            if has_sink:
                sink = sink_ref[2 * p + h]
                m = jnp.maximum(m, sink)
            e = jnp.exp(s - m)
            den = jnp.sum(e, axis=-1, keepdims=True)
            if has_sink:
                den = den + jnp.exp(sink - m)
            pn = (e / den).astype(_BF16)
            outs.append(jnp.dot(pn, vcat, preferred_element_type=_F32))
            lses.append(m + jnp.log(den))
        o_ref[:, p * LANES:(p + 1) * LANES] = jnp.where(low, outs[0], outs[1])
        if with_lse:
            lse_ref[:, p * LANES:(p + 1) * LANES] = jnp.where(low, lses[0], lses[1])


def _band_attention(sinks, q, kv, *, q_col, k_col, v_col, q_width, kv_width, n_seq, nb, kv_share,
                    with_lse, name):
    has_prev = True
    cur = lambda b, r, j, s: b * nb + j
    prev = lambda b, r, j, s: b * nb + jnp.maximum(j - 1, 0)
    blk = lambda rows, width, col: pl.BlockSpec((BLOCK, width), lambda *a: (rows(*a), col))
    in_specs = [blk(cur, q_width, q_col), blk(cur, kv_width, k_col), blk(cur, kv_width, v_col)]
    args = [q, kv, kv]
    if has_prev:
        in_specs += [blk(prev, kv_width, k_col), blk(prev, kv_width, v_col)]
        args += [kv, kv]
    o_spec = blk(cur, q_width, 0)
    o_sds = jax.ShapeDtypeStruct((n_seq * nb * BLOCK, q_width), _F32)
    return pl.pallas_call(
        functools.partial(_band_attn_kernel, has_prev=has_prev, kv_share=kv_share,
                          has_sink=sinks is not None, with_lse=with_lse),
        grid_spec=pltpu.PrefetchScalarGridSpec(
            num_scalar_prefetch=1,
            grid=(n_seq, 1, nb),
            in_specs=in_specs,
            out_specs=[o_spec, o_spec] if with_lse else [o_spec],
        ),
        out_shape=[o_sds, o_sds] if with_lse else [o_sds],
        compiler_params=_cparams("parallel", "arbitrary", "arbitrary"),
        name=name,
    )(jnp.zeros((1,), _F32) if sinks is None else sinks, *args)


def _band_attention_a(qkv, sinks, n_batch):
    nq = A_HEADS * HEAD_DIM
    nk = A_KV_HEADS * HEAD_DIM
    (o,) = _band_attention(sinks, qkv, qkv, q_col=0, k_col=nq // nk, v_col=nq // nk + 1,
                           q_width=nq, kv_width=nk, n_seq=n_batch, nb=SEQ // BLOCK,
                           kv_share=A_GROUP // 2, with_lse=False, name="band_attn_a")
    return o


def _band_attention_b(q, kv, dil, n_batch):
    gw = B_GROUP_WIDTH
    o, lse = _band_attention(None, q.reshape(-1, gw), kv.reshape(-1, 2 * gw), q_col=0, k_col=0, v_col=1,
                             q_width=gw, kv_width=gw, n_seq=n_batch * dil, nb=SEQ // dil // BLOCK,
                             kv_share=0, with_lse=True, name="band_attn_d%d" % dil)
    return o.reshape(q.shape), lse.reshape(q.shape)


STEP_NEW_ROWS = 8
STEP_CACHE_BYTES = 8 * 1024 * 1024


STEP_A_SAMPLES = 32


def _step_a_kernel(q_ref, kn_ref, vn_ref, cache_ref, sink_ref, o_ref):
    n_b, n_h, n_rows = q_ref.shape[:3]
    length = cache_ref.shape[-1]

    def step_of(shape):
        return lax.broadcasted_iota(jnp.int32, shape, 0) // A_GROUP

    valid_c = lax.broadcasted_iota(jnp.int32, (n_rows, length), 1) >= step_of((n_rows, length))
    valid_n = (lax.broadcasted_iota(jnp.int32, (n_rows, STEP_NEW_ROWS), 1)
               <= step_of((n_rows, STEP_NEW_ROWS)))

    def slot(i, carry):
        b, h = i // n_h, i % n_h
        q = q_ref[b, h].astype(_BF16)
        s_c = jnp.dot(q, cache_ref[b, 0, h].astype(_BF16), preferred_element_type=_F32)
        s_n = lax.dot_general(q, kn_ref[b, h].astype(_BF16), _NT, preferred_element_type=_F32)
        s_c = jnp.where(valid_c, s_c, _NEG_INF)
        s_n = jnp.where(valid_n, s_n, _NEG_INF)
        sink = sink_ref[h]
        m = jnp.maximum(jnp.max(s_c, axis=-1, keepdims=True), jnp.max(s_n, axis=-1, keepdims=True))
        m = jnp.maximum(m, sink)
        e_c = jnp.exp(s_c - m)
        e_n = jnp.exp(s_n - m)
        den = (jnp.sum(e_c, axis=-1, keepdims=True) + jnp.sum(e_n, axis=-1, keepdims=True)
               + jnp.exp(sink - m))
        o = lax.dot_general((e_c / den).astype(_BF16), cache_ref[b, 1, h].astype(_BF16), _NT,
                            preferred_element_type=_F32)
        o_ref[b, h] = o + jnp.dot((e_n / den).astype(_BF16), vn_ref[b, h].astype(_BF16),
                                  preferred_element_type=_F32)
        return carry

    lax.fori_loop(0, n_b * n_h, slot, 0, unroll=8)


def _slots(x, db, n_heads):
    x = x.reshape(db, DEC_SEQ, n_heads, HEAD_DIM).transpose(0, 2, 1, 3)
    return jnp.pad(x, [(0, 0), (0, 0), (0, STEP_NEW_ROWS - DEC_SEQ), (0, 0)])


def _step_b_kernel(q_ref, kn_ref, vn_ref, cache_ref, o_ref, lse_ref, *, dil):
    n_h, length = cache_ref.shape[2], cache_ref.shape[4]
    width = n_h * HEAD_DIM
    n_rows = n_h * STEP_NEW_ROWS
    seg = lax.broadcasted_iota(jnp.int32, (STEP_NEW_ROWS, width), 1) // HEAD_DIM

    def per_head(fn):
        return [fn(h) for h in range(n_h)]

    def own_head(blocks):
        return functools.reduce(jnp.add, [jnp.where(seg == h, blk, 0.0) for h, blk in enumerate(blocks)])

    q_tok = q_ref[0]
    q_bd = jnp.concatenate(per_head(lambda h: jnp.where(seg == h, q_tok, 0.0)), axis=0).astype(_BF16)
    kt = cache_ref[0, 0].reshape(width, length).astype(_BF16)
    vt = cache_ref[0, 1].reshape(width, length).astype(_BF16)
    s_c = jnp.dot(q_bd, kt, preferred_element_type=_F32)
    s_n = lax.dot_general(q_bd, kn_ref[0].astype(_BF16), _NT, preferred_element_type=_F32)

    def step_of(shape):
        return (lax.broadcasted_iota(jnp.int32, shape, 0) % STEP_NEW_ROWS) % DEC_SEQ

    col_c = lax.broadcasted_iota(jnp.int32, (n_rows, length), 1)
    col_n = lax.broadcasted_iota(jnp.int32, (n_rows, STEP_NEW_ROWS), 1)
    if dil == 1:
        valid_c = col_c >= step_of((n_rows, length))
        valid_n = col_n <= step_of((n_rows, STEP_NEW_ROWS))
    else:
        valid_c = (col_c % dil) == step_of((n_rows, length))
        valid_n = col_n == step_of((n_rows, STEP_NEW_ROWS))
    s_c = jnp.where(valid_c, s_c, _NEG_INF)
    s_n = jnp.where(valid_n, s_n, _NEG_INF)
    m = jnp.maximum(jnp.max(s_c, axis=-1, keepdims=True), jnp.max(s_n, axis=-1, keepdims=True))
    e_c = jnp.exp(s_c - m)
    e_n = jnp.exp(s_n - m)
    den = jnp.sum(e_c, axis=-1, keepdims=True) + jnp.sum(e_n, axis=-1, keepdims=True)
    o_all = lax.dot_general((e_c / den).astype(_BF16), vt, _NT, preferred_element_type=_F32)
    o_all = o_all + jnp.dot((e_n / den).astype(_BF16), vn_ref[0].astype(_BF16),
                            preferred_element_type=_F32)
    rows = lambda x, h: x[h * STEP_NEW_ROWS:(h + 1) * STEP_NEW_ROWS]
    o_ref[0] = own_head(per_head(lambda h: rows(o_all, h)))
    lse = m + jnp.log(den)
    lse_ref[0] = own_head(per_head(lambda h: jnp.broadcast_to(rows(lse, h), (STEP_NEW_ROWS, width))))


def _step_b(q_s, kv_s, cache, dil):
    db, length = cache.shape[:2]
    gw = B_GROUP_WIDTH
    cache_t = cache.transpose(0, 2, 3, 4, 1)
    hb = min(B_HEADS, max(1, STEP_CACHE_BYTES // (2 * HEAD_DIM * length * 4)))
    pad = lambda x: jnp.pad(x.reshape(db, DEC_SEQ, gw), [(0, 0), (0, STEP_NEW_ROWS - DEC_SEQ), (0, 0)])
    tok_spec = pl.BlockSpec((1, STEP_NEW_ROWS, hb * HEAD_DIM), lambda b, j: (b, 0, j))
    tok_sds = jax.ShapeDtypeStruct((db, STEP_NEW_ROWS, gw), _F32)
    o, lse = pl.pallas_call(
        functools.partial(_step_b_kernel, dil=dil),
        grid=(db, B_HEADS // hb),
        in_specs=[tok_spec, tok_spec, tok_spec,
                  pl.BlockSpec((1, 2, hb, HEAD_DIM, length), lambda b, j: (b, 0, j, 0, 0))],
        out_specs=[tok_spec, tok_spec],
        out_shape=[tok_sds, tok_sds],
        compiler_params=_cparams("parallel", "parallel"),
        name="step_b_l%d" % length,
    )(pad(q_s.astype(_F32)), pad(kv_s[:, :gw]), pad(kv_s[:, gw:]), cache_t)
    back = lambda x: x[:, :DEC_SEQ].reshape(db * DEC_SEQ, gw)
    return back(o), back(lse)


def _step_a(qkv_s, cache, sinks, db):
    nq = A_HEADS * HEAD_DIM
    nk = A_KV_HEADS * HEAD_DIM
    q = qkv_s[:, :nq].reshape(db, DEC_SEQ, A_KV_HEADS, A_GROUP, HEAD_DIM)
    q = q.transpose(0, 2, 1, 3, 4).reshape(db, A_KV_HEADS, DEC_SEQ * A_GROUP, HEAD_DIM)
    sink_rows = jnp.tile(sinks.reshape(A_KV_HEADS, 1, A_GROUP), (1, DEC_SEQ, 1))
    sink_rows = sink_rows.reshape(A_KV_HEADS, DEC_SEQ * A_GROUP, 1)
    bb = STEP_A_SAMPLES
    q_spec = pl.BlockSpec((bb,) + q.shape[1:], lambda i: (i, 0, 0, 0))
    n_spec = pl.BlockSpec((bb, A_KV_HEADS, STEP_NEW_ROWS, HEAD_DIM), lambda i: (i, 0, 0, 0))
    o = pl.pallas_call(
        _step_a_kernel,
        grid=(db // bb,),
        in_specs=[q_spec, n_spec, n_spec,
                  pl.BlockSpec((bb, 2, A_KV_HEADS, HEAD_DIM, A_WINDOW), lambda i: (i, 0, 0, 0, 0)),
                  pl.BlockSpec(sink_rows.shape, lambda i: (0, 0, 0))],
        out_specs=q_spec,
        out_shape=jax.ShapeDtypeStruct(q.shape, _F32),
        compiler_params=_cparams("parallel"),
        name="step_a",
    )(q, _slots(qkv_s[:, nq:nq + nk], db, A_KV_HEADS), _slots(qkv_s[:, nq + nk:], db, A_KV_HEADS),
      cache.transpose(0, 2, 3, 4, 1), sink_rows)
    o = o.reshape(db, A_KV_HEADS, DEC_SEQ, A_GROUP, HEAD_DIM).transpose(0, 2, 1, 3, 4)
    return o.reshape(db * DEC_SEQ, nq)


def _layer_norm_rows(z, g, b):
    mu = jnp.mean(z, axis=-1, keepdims=True)
    zc = z - mu
    var = jnp.mean(zc * zc, axis=-1, keepdims=True)
    return zc * lax.rsqrt(var + LN_EPS) * g + b


def _proj_ln_kernel(*refs, dils, has_bias):
    n_merge = len(dils)
    n_in = 2 * n_merge if n_merge else 1
    w_ref = refs[n_in]
    pos = n_in + 1
    if has_bias:
        bias_ref = refs[pos]
        pos += 1
    x_ref, g_ref, b_ref, out_ref = refs[pos:pos + 4]
    scratch = refs[pos + 4:]
    tm = x_ref.shape[0]

    if n_merge:
        o_scr, order_scr = scratch[0], scratch[1:]

        def token_order(idx, ref, d, lanes):
            if d == 1:
                return ref[:, lanes]
            scr = order_scr[idx]
            for r in range(d):
                scr[pl.ds(r, tm // d, stride=d), :] = ref[r, :, lanes]
            return scr[...]

        reordered = [i for i, d in enumerate(dils) if d > 1]
        for g in range(w_ref.shape[0] // LANES):
            lanes = slice(g * LANES, (g + 1) * LANES)
            os = [token_order(2 * reordered.index(i) if d > 1 else 0, refs[i], d, lanes)
                  for i, d in enumerate(dils)]
            ls = [token_order(2 * reordered.index(i) + 1 if d > 1 else 0, refs[n_merge + i], d, lanes)
                  for i, d in enumerate(dils)]
            m = functools.reduce(jnp.maximum, ls)
            es = [jnp.exp(l - m) for l in ls]
            den = functools.reduce(jnp.add, es)
            o_scr[:, lanes] = functools.reduce(
                jnp.add, [(e / den) * o for e, o in zip(es, os)]).astype(_BF16)
        o = o_scr[...]
    else:
        o = refs[0][...].astype(_BF16)
    y = jnp.dot(o, w_ref[...], preferred_element_type=_F32)
    if has_bias:
        y = y + bias_ref[...]
    out_ref[...] = _layer_norm_rows(DN_ALPHA * x_ref[...] + y, g_ref[...], b_ref[...])


def _proj_ln(o_list, lse_list, dils, w_bf16, bias, x, ln_g, ln_b, tm=256):
    n = x.shape[0]
    k = w_bf16.shape[0]
    row = lambda i: (i, 0)
    fixed = lambda i: (0, 0)
    tiles = SEQ // tm

    def in_spec(d):
        if d == 1:
            return pl.BlockSpec((tm, k), row)
        return pl.BlockSpec((d, tm // d, k), lambda i: (i // tiles, i % tiles, 0))

    args = list(o_list) + list(lse_list) + [w_bf16]
    specs = [in_spec(d) for d in (dils or (1,))] * (2 if lse_list else 1)
    specs.append(pl.BlockSpec((k, D_MODEL), fixed))
    if bias is not None:
        args.append(bias)
        specs.append(pl.BlockSpec((1, D_MODEL), fixed))
    args += [x, ln_g, ln_b]
    specs += [pl.BlockSpec((tm, D_MODEL), row), pl.BlockSpec((1, D_MODEL), fixed),
              pl.BlockSpec((1, D_MODEL), fixed)]
    scratch = []
    if lse_list:
        scratch = [pltpu.VMEM((tm, k), _BF16)]
        scratch += [pltpu.VMEM((tm, LANES), _F32)] * (2 * sum(d > 1 for d in dils))
    return pl.pallas_call(
        functools.partial(_proj_ln_kernel, dils=tuple(dils), has_bias=bias is not None),
        grid=(n // tm,),
        in_specs=specs,
        out_specs=pl.BlockSpec((tm, D_MODEL), row),
        out_shape=jax.ShapeDtypeStruct((n, D_MODEL), _F32),
        scratch_shapes=scratch,
        compiler_params=_cparams("parallel"),
        name="proj_ln",
    )(*args)


N_KEY_BLOCKS = PEER_HEADS * 2
HALF_KEY_DIM = PEER_KEY_DIM // 2


def _peer_score_kernel(x_ref, wq_ref, keys_ref, sc_ref):
    q = jnp.dot(x_ref[...].astype(_BF16), wq_ref[...], preferred_element_type=_F32).astype(_BF16)
    for blk in range(N_KEY_BLOCKS):
        sc_ref[blk * PEER_N_KEYS:(blk + 1) * PEER_N_KEYS, :] = lax.dot_general(
            keys_ref[blk], q[:, blk * HALF_KEY_DIM:(blk + 1) * HALF_KEY_DIM], _NT,
            preferred_element_type=_F32)


def _peer_scores(x, wq_bf16, keys_bf16, layer):
    n = x.shape[0]
    tm = TOKEN_TILE
    return pl.pallas_call(
        _peer_score_kernel,
        grid=(n // tm,),
        in_specs=[
            pl.BlockSpec((tm, D_MODEL), lambda i: (i, 0)),
            pl.BlockSpec((None, D_MODEL, PEER_HEADS * PEER_KEY_DIM), lambda i: (layer, 0, 0)),
            pl.BlockSpec((None, N_KEY_BLOCKS, PEER_N_KEYS, HALF_KEY_DIM), lambda i: (layer, 0, 0, 0)),
        ],
        out_specs=pl.BlockSpec((N_KEY_BLOCKS * PEER_N_KEYS, tm), lambda i: (0, i)),
        out_shape=jax.ShapeDtypeStruct((N_KEY_BLOCKS * PEER_N_KEYS, n), _F32),
        compiler_params=_cparams("parallel"),
        name="peer_scores",
    )(x, wq_bf16, keys_bf16)


TOPK_TOKENS = 128


def _top1_rows(vals, ids, big):
    m = jnp.max(vals, axis=0, keepdims=True)
    sel = jnp.min(jnp.where(vals == m, ids, big), axis=0, keepdims=True)
    return m, sel, ids == sel


def _peer_topk_kernel(sc_ref, e_ref, gate_ref, top_s, top_i, sel_e, sel_g):
    tt = sc_ref.shape[1]
    key_id = lax.broadcasted_iota(jnp.int32, (PEER_N_KEYS, tt), 0).astype(_F32)

    def stage1(blk, carry):
        s = sc_ref[pl.ds(pl.multiple_of(blk * PEER_N_KEYS, PEER_N_KEYS), PEER_N_KEYS), :]
        for k in range(PEER_TOPK):
            m, sel, onehot = _top1_rows(s, key_id, float(PEER_N_KEYS))
            top_s[blk, k:k + 1, :] = m
            top_i[blk, k:k + 1, :] = sel
            s = jnp.where(onehot, _NEG_INF, s)
        return carry

    lax.fori_loop(0, N_KEY_BLOCKS, stage1, 0, unroll=4)

    sub = lax.broadcasted_iota(jnp.int32, (SUBLANES, tt), 0).astype(_F32)
    flat_id = jnp.concatenate(
        [sub, sub + SUBLANES] + [sub + a * PEER_TOPK for a in range(1, SUBLANES)]
        + [(sub + SUBLANES) * PEER_TOPK], axis=0)

    def bcast(row):
        return jnp.broadcast_to(row, (SUBLANES, tt))

    def stage2(h, carry):
        s0, s1 = top_s[2 * h], top_s[2 * h + 1]
        i0, i1 = top_i[2 * h] * PEER_N_KEYS, top_i[2 * h + 1]
        lo_s, hi_s, lo_i, hi_i = s1[:SUBLANES], s1[SUBLANES:], i1[:SUBLANES], i1[SUBLANES:]
        cand = jnp.concatenate(
            [bcast(s0[0:1]) + lo_s, bcast(s0[0:1]) + hi_s]
            + [bcast(s0[a:a + 1]) + lo_s for a in range(1, SUBLANES)]
            + [s0[SUBLANES:] + bcast(s1[0:1])], axis=0)
        cand_e = jnp.concatenate(
            [bcast(i0[0:1]) + lo_i, bcast(i0[0:1]) + hi_i]
            + [bcast(i0[a:a + 1]) + lo_i for a in range(1, SUBLANES)]
            + [i0[SUBLANES:] + bcast(i1[0:1])], axis=0)
        best = []
        for k in range(PEER_TOPK):
            m, _, onehot = _top1_rows(cand, flat_id, float(PEER_TOPK * PEER_TOPK))
            best.append(m)
            sel_e[h, k:k + 1, :] = jnp.max(jnp.where(onehot, cand_e, -1.0), axis=0, keepdims=True)
            cand = jnp.where(onehot, _NEG_INF, cand)
        ex = [jnp.exp(b - best[0]) for b in best]
        den = functools.reduce(jnp.add, ex)
        for k in range(PEER_TOPK):
            sel_g[h, k:k + 1, :] = ex[k] / den
        return carry

    lax.fori_loop(0, PEER_HEADS, stage2, 0, unroll=4)
    n_sel = PEER_HEADS * PEER_TOPK
    e_ref[...] = sel_e[...].reshape(n_sel, tt).T.astype(jnp.int32)
    gate_ref[...] = sel_g[...].reshape(n_sel, tt).T


def _peer_topk(sc_t):
    n = sc_t.shape[1]
    tt = TOPK_TOKENS
    n_sel = PEER_HEADS * PEER_TOPK
    return pl.pallas_call(
        _peer_topk_kernel,
        grid=(n // tt,),
        in_specs=[pl.BlockSpec((N_KEY_BLOCKS * PEER_N_KEYS, tt), lambda i: (0, i))],
        out_specs=[pl.BlockSpec((tt, n_sel), lambda i: (i, 0)),
                   pl.BlockSpec((tt, n_sel), lambda i: (i, 0))],
        out_shape=[jax.ShapeDtypeStruct((n, n_sel), jnp.int32),
                   jax.ShapeDtypeStruct((n, n_sel), _F32)],
        scratch_shapes=[pltpu.VMEM((N_KEY_BLOCKS, PEER_TOPK, tt), _F32),
                        pltpu.VMEM((N_KEY_BLOCKS, PEER_TOPK, tt), _F32),
                        pltpu.VMEM((PEER_HEADS, PEER_TOPK, tt), _F32),
                        pltpu.VMEM((PEER_HEADS, PEER_TOPK, tt), _F32)],
        compiler_params=_cparams("parallel"),
        name="peer_topk",
    )(sc_t)


GATE_TOKENS = 64
GATE_UNROLL = 16


def _peer_gate_kernel(e_ref, gate_ref, g_ref, scr):
    tb, _, n_sel = e_ref.shape
    row_id = lax.broadcasted_iota(jnp.int32, (PEER_N_KEYS, n_sel), 0)

    def body(n, carry):
        e = e_ref[n]
        gate = gate_ref[n]
        hi = gate.astype(_BF16).astype(_F32)
        lo = gate - hi
        is_i = row_id == (e >> 7)
        lhs = jnp.concatenate([jnp.where(is_i, hi, 0.0), jnp.where(is_i, lo, 0.0)], axis=1)
        one = jnp.where(row_id == (e & (PEER_N_KEYS - 1)), 1.0, 0.0)
        rhs = jnp.concatenate([one, one], axis=1)
        res = lax.dot_general(lhs.astype(_BF16), rhs.astype(_BF16), _NT, preferred_element_type=_F32)
        for ib in range(PEER_N_KEYS // SUBLANES):
            start = pl.multiple_of((ib * tb + n) * SUBLANES, SUBLANES)
            scr[pl.ds(start, SUBLANES), :] = res[ib * SUBLANES:(ib + 1) * SUBLANES]
        return carry

    lax.fori_loop(0, tb, body, 0, unroll=GATE_UNROLL)
    for i in range(PEER_N_KEYS):
        start = (i // SUBLANES) * tb * SUBLANES + i % SUBLANES
        g_ref[:, i * PEER_N_KEYS:(i + 1) * PEER_N_KEYS] = scr[pl.ds(start, tb, stride=SUBLANES), :]


def _peer_gates(expert, gate):
    n, n_sel = expert.shape
    tb = GATE_TOKENS
    expert = expert.reshape(n, 1, n_sel)
    gate = gate.reshape(n, 1, n_sel)
    return pl.pallas_call(
        _peer_gate_kernel,
        grid=(n // tb,),
        in_specs=[pl.BlockSpec((tb, 1, n_sel), lambda i: (i, 0, 0)),
                  pl.BlockSpec((tb, 1, n_sel), lambda i: (i, 0, 0))],
        out_specs=pl.BlockSpec((tb, PEER_N_EXPERTS), lambda i: (i, 0)),
        out_shape=jax.ShapeDtypeStruct((n, PEER_N_EXPERTS), _F32),
        scratch_shapes=[pltpu.VMEM((tb * PEER_N_KEYS, PEER_N_KEYS), _F32)],
        compiler_params=_cparams("parallel"),
        name="peer_gates",
    )(expert, gate)


EXPERT_CHUNK = 1024
_SQRT_HALF = math.sqrt(0.5)


def _peer_main_kernel(x_ref, ut_ref, g_ref, v_ref, lng_ref, lnb_ref, o_ref, xb_scr, acc_scr):
    c = pl.program_id(1)

    @pl.when(c == 0)
    def _():
        xb_scr[...] = x_ref[...].astype(_BF16)
        acc_scr[...] = jnp.zeros_like(acc_scr)

    h_parts = []
    for k in range(ut_ref.shape[1] // MXU_COLS):
        cols = slice(k * MXU_COLS, (k + 1) * MXU_COLS)
        a = jnp.dot(xb_scr[...], ut_ref[:, cols], preferred_element_type=_F32)
        gelu = 0.5 * a * (1.0 + lax.erf(a * _SQRT_HALF))
        h_parts.append((g_ref[:, cols] * gelu).astype(_BF16))
    h = jnp.concatenate(h_parts, axis=1)
    acc_scr[...] += jnp.dot(h, v_ref[...], preferred_element_type=_F32)

    @pl.when(c == pl.num_programs(1) - 1)
    def _():
        o_ref[...] = _layer_norm_rows(DN_ALPHA * x_ref[...] + acc_scr[...], lng_ref[...], lnb_ref[...])


def _peer_main(x, ut_bf16, gates, v_bf16, ln_g, ln_b, layer):
    n = x.shape[0]
    tm, ce = TOKEN_TILE, EXPERT_CHUNK
    return pl.pallas_call(
        _peer_main_kernel,
        grid=(n // tm, PEER_N_EXPERTS // ce),
        in_specs=[
            pl.BlockSpec((tm, D_MODEL), lambda i, c: (i, 0)),
            pl.BlockSpec((None, D_MODEL, ce), lambda i, c: (layer, 0, c)),
            pl.BlockSpec((tm, ce), lambda i, c: (i, c)),
            pl.BlockSpec((None, ce, D_MODEL), lambda i, c: (layer, c, 0)),
            pl.BlockSpec((1, D_MODEL), lambda i, c: (0, 0)),
            pl.BlockSpec((1, D_MODEL), lambda i, c: (0, 0)),
        ],
        out_specs=pl.BlockSpec((tm, D_MODEL), lambda i, c: (i, 0)),
        out_shape=jax.ShapeDtypeStruct((n, D_MODEL), _F32),
        scratch_shapes=[pltpu.VMEM((tm, D_MODEL), _BF16), pltpu.VMEM((tm, D_MODEL), _F32)],
        compiler_params=_cparams("parallel", "arbitrary"),
        name="peer_main",
    )(x, ut_bf16, gates, v_bf16, ln_g, ln_b)


def _peer_ffn_ln(x, wq_bf16, keys_bf16, ut_bf16, v_bf16, ln_g, ln_b, layer):
    sc_t = _peer_scores(x, wq_bf16, keys_bf16, layer)
    expert, gate = _peer_topk(sc_t)
    gates = _peer_gates(expert, gate)
    return _peer_main(x, ut_bf16, gates, v_bf16, ln_g, ln_b, layer)


def _rope_tables(pos):
    half = HEAD_DIM // 2
    inv = ROPE_THETA ** (-jnp.arange(half, dtype=_F32) / half)
    ang = pos.astype(_F32)[:, None] * inv[None, :]
    cos, sin = jnp.cos(ang), jnp.sin(ang)
    cos = jnp.tile(cos, (1, LANES // half))
    sin = jnp.tile(jnp.concatenate([-sin, sin], axis=1), (1, LANES // HEAD_DIM))
    return cos, sin


def kernel(x_prompt, x_sample, cache_a_kv, cache_b1_kv, cache_b2_kv, cache_b3_kv, w_qkv_a, b_qkv_a, sinks_a, w_o_a, b_o_a, w_qkv_b, w_o_b, ln_mix_g, ln_mix_b, ln_ffn_g, ln_ffn_b, peer_w_query, peer_sub_keys, peer_u, peer_v):
    n_batch, seq = x_prompt.shape[:2]
    db, t_len = x_sample.shape[:2]
    xp = x_prompt.reshape(n_batch * seq, D_MODEL)
    xs = x_sample.reshape(db * t_len, D_MODEL)
    cos_p, sin_p = _rope_tables(jnp.tile(jnp.arange(seq, dtype=jnp.int32), n_batch))
    cos_s, sin_s = _rope_tables(jnp.tile(PAST_LEN + jnp.arange(t_len, dtype=jnp.int32), db))
    scale = HEAD_DIM ** -0.5
    row = lambda v: v.reshape(1, -1)
    caches_b = (cache_b1_kv, cache_b2_kv, cache_b3_kv)
    outs_kv = {}
    wq_all = peer_w_query.astype(_BF16)
    keys_all = peer_sub_keys.reshape(DEPTH, N_KEY_BLOCKS, PEER_N_KEYS, HALF_KEY_DIM).astype(_BF16)
    ut_all = jnp.swapaxes(peer_u, 1, 2).astype(_BF16)
    v_all = peer_v.astype(_BF16)

    for layer in range(DEPTH):
        i = layer // 2
        ln_g, ln_b = row(ln_mix_g[layer]), row(ln_mix_b[layer])
        if layer % 2 == 0:
            nq = A_HEADS * HEAD_DIM
            nk = A_KV_HEADS * HEAD_DIM
            col_scale = jnp.concatenate([jnp.full((nq,), scale, _F32), jnp.ones((2 * nk,), _F32)])
            w = (w_qkv_a[i] * col_scale).astype(_BF16)
            bias = row(b_qkv_a[i] * col_scale)
            qkv_p = _qkv_rope(xp, w, bias, cos_p, sin_p, nq + nk)
            qkv_s = _qkv_rope(xs, w, bias, cos_s, sin_s, nq + nk)
            o_p = _band_attention_a(qkv_p, sinks_a[i], n_batch)
            o_s = _step_a(qkv_s, cache_a_kv[i], sinks_a[i], db)
            wo = w_o_a[i].astype(_BF16)
            xp = _proj_ln([o_p], [], (), wo, row(b_o_a[i]), xp, ln_g, ln_b)
            xs = _proj_ln([o_s], [], (), wo, row(b_o_a[i]), xs, ln_g, ln_b)
            keep = min(A_WINDOW, seq)
            kv_p = qkv_p.reshape(n_batch, seq, A_QKV)[:, seq - keep:, nq:]
            outs_kv.setdefault("a_p", []).append(kv_p.reshape(n_batch, keep, 2, A_KV_HEADS, HEAD_DIM))
            outs_kv.setdefault("a_s", []).append(qkv_s[:, nq:].reshape(db, t_len, 2, A_KV_HEADS, HEAD_DIM))
        else:
            gw = B_GROUP_WIDTH
            col_scale = jnp.concatenate([jnp.full((N_B_GROUPS * gw,), scale, _F32),
                                         jnp.ones((2 * N_B_GROUPS * gw,), _F32)])
            w = (w_qkv_b[i] * col_scale).astype(_BF16)
            o_ps, l_ps, o_ss, l_ss = [], [], [], []
            for g, (window, dil) in enumerate(B_GROUPS):
                outs = _qkv_group(xp, w, cos_p, sin_p, dil, g)
                q_p, kv_p, kvr_p = outs if dil > 1 else (outs[0], outs[1], outs[1])
                q_s, kv_s = _qkv_group(xs, w, cos_s, sin_s, 1, g)
                o, l = _band_attention_b(q_p, kvr_p, dil, n_batch)
                o_ps.append(o)
                l_ps.append(l)
                kv_s5 = kv_s.reshape(db, t_len, 2, B_HEADS, HEAD_DIM)
                o, l = _step_b(q_s, kv_s, caches_b[g][i], dil)
                o_ss.append(o)
                l_ss.append(l)
                keep = min(window, seq)
                kv_keep = kv_p.reshape(n_batch, seq, 2 * gw)[:, seq - keep:]
                outs_kv.setdefault("b%d_p" % g, []).append(
                    kv_keep.reshape(n_batch, keep, 2, B_HEADS, HEAD_DIM))
                outs_kv.setdefault("b%d_s" % g, []).append(kv_s5)
            wo = w_o_b[i].astype(_BF16)
            xp = _proj_ln(o_ps, l_ps, [dil for _, dil in B_GROUPS], wo, None, xp, ln_g, ln_b)
            xs = _proj_ln(o_ss, l_ss, [1] * N_B_GROUPS, wo, None, xs, ln_g, ln_b)

        ffn_g, ffn_b = row(ln_ffn_g[layer]), row(ln_ffn_b[layer])
        xp = _peer_ffn_ln(xp, wq_all, keys_all, ut_all, v_all, ffn_g, ffn_b, layer)
        xs = _peer_ffn_ln(xs, wq_all, keys_all, ut_all, v_all, ffn_g, ffn_b, layer)

    return (xp.reshape(n_batch, seq, D_MODEL), xs.reshape(db, t_len, D_MODEL),
            jnp.stack(outs_kv["a_p"]), jnp.stack(outs_kv["a_s"]),
            jnp.stack(outs_kv["b0_p"]), jnp.stack(outs_kv["b0_s"]),
            jnp.stack(outs_kv["b1_p"]), jnp.stack(outs_kv["b1_s"]),
            jnp.stack(outs_kv["b2_p"]), jnp.stack(outs_kv["b2_s"]))
```

```python
import functools
import math

import jax
import jax.numpy as jnp
from jax import lax
from jax.experimental import pallas as pl
from jax.experimental.pallas import tpu as pltpu

D_MODEL = 2048
SEQ = 2048
DEPTH = 2
DEC_SEQ = 4
PAST_LEN = 8192
HEAD_DIM = 64
ROPE_THETA = 10000.0
BLOCK = 128
A_HEADS = D_MODEL // HEAD_DIM
A_KV_HEADS = A_HEADS // 8
A_GROUP = A_HEADS // A_KV_HEADS
A_WINDOW = 128
A_QKV = (A_HEADS + 2 * A_KV_HEADS) * HEAD_DIM
B_GROUPS = ((128, 1), (512, 4), (2048, 16))
N_B_GROUPS = len(B_GROUPS)
B_HEADS = D_MODEL // (2 * HEAD_DIM)
B_GROUP_WIDTH = B_HEADS * HEAD_DIM
B_KEYS = 128
PEER_HEADS = 8
PEER_N_KEYS = 128
PEER_N_EXPERTS = PEER_N_KEYS * PEER_N_KEYS
PEER_TOPK = 16
PEER_KEY_DIM = 256
DN_ALPHA = (2.0 * DEPTH) ** 0.25
LN_EPS = 1e-5

LANES = 128
SUBLANES = 8
TOKEN_TILE = 512
VMEM_LIMIT = 56 * 1024 * 1024

_F32 = jnp.float32
_BF16 = jnp.bfloat16
_NEG_INF = float("-inf")
_NT = (((1,), (1,)), ((), ()))


def _cparams(*sem):
    return pltpu.CompilerParams(dimension_semantics=sem, vmem_limit_bytes=VMEM_LIMIT)


MXU_COLS = 256


def _rope(blk, cos, sin):
    lane = lax.broadcasted_iota(jnp.int32, blk.shape, 1)
    first_half = (lane % HEAD_DIM) < (HEAD_DIM // 2)
    partner = jnp.where(first_half, pltpu.roll(blk, LANES - HEAD_DIM // 2, 1),
                        pltpu.roll(blk, HEAD_DIM // 2, 1))
    return blk * cos + partner * sin


def _project_chunks(xb, w_ref, emit, bias_ref=None):
    for c in range(w_ref.shape[1] // MXU_COLS):
        cols = slice(c * MXU_COLS, (c + 1) * MXU_COLS)
        acc = jnp.dot(xb, w_ref[:, cols], preferred_element_type=_F32)
        if bias_ref is not None:
            acc = acc + bias_ref[:, cols]
        for h in range(MXU_COLS // LANES):
            emit(c * (MXU_COLS // LANES) + h, acc[:, h * LANES:(h + 1) * LANES])


def _qkv_rope_kernel(x_ref, w_ref, b_ref, cos_ref, sin_ref, o_ref, *, n_rope_groups):
    def emit(g, blk):
        if g < n_rope_groups:
            blk = _rope(blk, cos_ref[...], sin_ref[...])
        o_ref[:, g * LANES:(g + 1) * LANES] = blk

    _project_chunks(x_ref[...].astype(_BF16), w_ref, emit, b_ref)


def _qkv_rope(x, w_bf16, bias, cos, sin, n_rope_cols):
    n, k = x.shape
    ncols = w_bf16.shape[1]
    tm = TOKEN_TILE
    return pl.pallas_call(
        functools.partial(_qkv_rope_kernel, n_rope_groups=n_rope_cols // LANES),
        grid=(n // tm,),
        in_specs=[
            pl.BlockSpec((tm, k), lambda i: (i, 0)),
            pl.BlockSpec((k, ncols), lambda i: (0, 0)),
            pl.BlockSpec((1, ncols), lambda i: (0, 0)),
            pl.BlockSpec((tm, LANES), lambda i: (i, 0)),
            pl.BlockSpec((tm, LANES), lambda i: (i, 0)),
        ],
        out_specs=pl.BlockSpec((tm, ncols), lambda i: (i, 0)),
        out_shape=jax.ShapeDtypeStruct((n, ncols), _F32),
        compiler_params=_cparams("parallel"),
        name="qkv_rope",
    )(x, w_bf16, bias, cos, sin)


def _qkv_group_kernel(x_ref, wq_ref, wk_ref, wv_ref, cos_ref, sin_ref, *refs, dil):
    if dil == 1:
        q_ref, kv_ref = refs
    else:
        q_ref, kv_ref, kvr_ref, scr = refs
    xb = x_ref[...].astype(_BF16)
    per_residue = x_ref.shape[0] // dil
    width = wq_ref.shape[1]

    def emitter(token_ref, residue_ref, rotate, col0, slot):
        def emit(g, blk):
            lanes = slice(col0 + g * LANES, col0 + (g + 1) * LANES)
            if rotate:
                blk = _rope(blk, cos_ref[...], sin_ref[...])
            if token_ref is not None:
                token_ref[:, lanes] = blk.astype(token_ref.dtype)
            if residue_ref is not None:
                scr[slot, g] = blk
                for r in range(dil):
                    residue_ref[r, :, lanes] = (
                        scr[slot, g, pl.ds(r, per_residue, stride=dil), :].astype(residue_ref.dtype))
        return emit

    kvr = kvr_ref if dil > 1 else None
    _project_chunks(xb, wq_ref, emitter(q_ref if dil == 1 else None, q_ref if dil > 1 else None,
                                        True, 0, 0))
    _project_chunks(xb, wk_ref, emitter(kv_ref, kvr, True, 0, 1))
    _project_chunks(xb, wv_ref, emitter(kv_ref, kvr, False, width, 2))


def _qkv_group(x, w_bf16, cos, sin, dil, group):
    n, k = x.shape
    tm, tn = TOKEN_TILE, B_GROUP_WIDTH
    kv_spec = pl.BlockSpec((tm, 2 * tn), lambda i: (i, 0))
    kv_sds = jax.ShapeDtypeStruct((n, 2 * tn), _F32)
    if dil == 1:
        out_specs = [pl.BlockSpec((tm, tn), lambda i: (i, 0)), kv_spec]
        out_shape = [jax.ShapeDtypeStruct((n, tn), _BF16), kv_sds]
        scratch = []
    else:
        tiles = SEQ // tm
        res_rows = (n // SEQ * dil, SEQ // dil)
        res_spec = lambda w: pl.BlockSpec((dil, tm // dil, w), lambda i: (i // tiles, i % tiles, 0))
        out_specs = [res_spec(tn), kv_spec, res_spec(2 * tn)]
        out_shape = [jax.ShapeDtypeStruct(res_rows + (tn,), _BF16), kv_sds,
                     jax.ShapeDtypeStruct(res_rows + (2 * tn,), _BF16)]
        scratch = [pltpu.VMEM((3, tn // LANES, tm, LANES), _F32)]
    w_spec = lambda part: pl.BlockSpec((k, tn), lambda i: (0, part * N_B_GROUPS + group),
                                       pipeline_mode=pl.Buffered(1))
    return pl.pallas_call(
        functools.partial(_qkv_group_kernel, dil=dil),
        grid=(n // tm,),
        in_specs=[
            pl.BlockSpec((tm, k), lambda i: (i, 0)),
            w_spec(0), w_spec(1), w_spec(2),
            pl.BlockSpec((tm, LANES), lambda i: (i, 0)),
            pl.BlockSpec((tm, LANES), lambda i: (i, 0)),
        ],
        out_specs=out_specs,
        out_shape=out_shape,
        scratch_shapes=scratch,
        compiler_params=_cparams("parallel"),
        name="qkv_group_d%d" % dil,
    )(x, w_bf16, w_bf16, w_bf16, cos, sin)


def _band_attn_kernel(sink_ref, q_ref, kc_ref, vc_ref, *refs, has_prev, kv_share, has_sink, with_lse):
    if has_prev:
        kp_ref, vp_ref = refs[:2]
        refs = refs[2:]
    o_ref = refs[0]
    lse_ref = refs[1] if with_lse else None
    n_keys = 2 * BLOCK if has_prev else BLOCK
    j = pl.program_id(2)
    qi = lax.broadcasted_iota(jnp.int32, (BLOCK, n_keys), 0)
    kj = lax.broadcasted_iota(jnp.int32, (BLOCK, n_keys), 1)
    if has_prev:
        first_key = jnp.maximum(qi, jnp.where(j > 0, 0, BLOCK))
        valid = (kj >= first_key) & (kj <= qi + BLOCK)
    else:
        valid = kj <= qi
    lane = lax.broadcasted_iota(jnp.int32, (BLOCK, LANES), 1)
    low = lane < HEAD_DIM

    def keys(cur_ref, prev_ref, kv_head):
        grp = kv_head // 2 if kv_share else kv_head
        lanes = slice(grp * LANES, (grp + 1) * LANES)
        parts = ([prev_ref[:, lanes]] if has_prev else []) + [cur_ref[:, lanes]]
        if kv_share:
            keep = low if kv_head % 2 == 0 else ~low
            parts = [jnp.where(keep, x, pltpu.roll(x, HEAD_DIM, 1)) for x in parts]
        return (jnp.concatenate(parts, axis=0) if has_prev else parts[0]).astype(_BF16)

    for p in range(q_ref.shape[1] // LANES):
        if kv_share == 0 or p % kv_share == 0:
            kv_head = p // kv_share if kv_share else p
            kcat = keys(kc_ref, kp_ref if has_prev else None, kv_head)
            vcat = keys(vc_ref, vp_ref if has_prev else None, kv_head)
        q2 = q_ref[:, p * LANES:(p + 1) * LANES]
        outs, lses = [], []
        for h in range(2):
            qm = jnp.where(low if h == 0 else ~low, q2, 0.0).astype(_BF16)
            s = lax.dot_general(qm, kcat, _NT, preferred_element_type=_F32)
            s = jnp.where(valid, s, _NEG_INF)
            m = jnp.max(s, axis=-1, keepdims=True)
            if has_sink:
                sink = sink_ref[2 * p + h]
                m = jnp.maximum(m, sink)
            e = jnp.exp(s - m)
            den = jnp.sum(e, axis=-1, keepdims=True)
            if has_sink:
                den = den + jnp.exp(sink - m)
            pn = (e / den).astype(_BF16)
            outs.append(jnp.dot(pn, vcat, preferred_element_type=_F32))
            lses.append(m + jnp.log(den))
        o_ref[:, p * LANES:(p + 1) * LANES] = jnp.where(low, outs[0], outs[1])
        if with_lse:
            lse_ref[:, p * LANES:(p + 1) * LANES] = jnp.where(low, lses[0], lses[1])


def _band_attention(sinks, q, kv, *, q_col, k_col, v_col, q_width, kv_width, n_seq, nb, kv_share,
                    with_lse, name):
    has_prev = True
    cur = lambda b, r, j, s: b * nb + j
    prev = lambda b, r, j, s: b * nb + jnp.maximum(j - 1, 0)
    blk = lambda rows, width, col: pl.BlockSpec((BLOCK, width), lambda *a: (rows(*a), col))
    in_specs = [blk(cur, q_width, q_col), blk(cur, kv_width, k_col), blk(cur, kv_width, v_col)]
    args = [q, kv, kv]
    if has_prev:
        in_specs += [blk(prev, kv_width, k_col), blk(prev, kv_width, v_col)]
        args += [kv, kv]
    o_spec = blk(cur, q_width, 0)
    o_sds = jax.ShapeDtypeStruct((n_seq * nb * BLOCK, q_width), _F32)
    return pl.pallas_call(
        functools.partial(_band_attn_kernel, has_prev=has_prev, kv_share=kv_share,
                          has_sink=sinks is not None, with_lse=with_lse),
        grid_spec=pltpu.PrefetchScalarGridSpec(
            num_scalar_prefetch=1,
            grid=(n_seq, 1, nb),
            in_specs=in_specs,
            out_specs=[o_spec, o_spec] if with_lse else [o_spec],
        ),
        out_shape=[o_sds, o_sds] if with_lse else [o_sds],
        compiler_params=_cparams("parallel", "arbitrary", "arbitrary"),
        name=name,
    )(jnp.zeros((1,), _F32) if sinks is None else sinks, *args)


def _band_attention_a(qkv, sinks, n_batch):
    nq = A_HEADS * HEAD_DIM
    nk = A_KV_HEADS * HEAD_DIM
    (o,) = _band_attention(sinks, qkv, qkv, q_col=0, k_col=nq // nk, v_col=nq // nk + 1,
                           q_width=nq, kv_width=nk, n_seq=n_batch, nb=SEQ // BLOCK,
                           kv_share=A_GROUP // 2, with_lse=False, name="band_attn_a")
    return o


def _band_attention_b(q, kv, dil, n_batch):
    gw = B_GROUP_WIDTH
    o, lse = _band_attention(None, q.reshape(-1, gw), kv.reshape(-1, 2 * gw), q_col=0, k_col=0, v_col=1,
                             q_width=gw, kv_width=gw, n_seq=n_batch * dil, nb=SEQ // dil // BLOCK,
                             kv_share=0, with_lse=True, name="band_attn_d%d" % dil)
    return o.reshape(q.shape), lse.reshape(q.shape)


STEP_NEW_ROWS = 8
STEP_CACHE_BYTES = 8 * 1024 * 1024


STEP_A_SAMPLES = 32


def _step_a_kernel(q_ref, kn_ref, vn_ref, cache_ref, sink_ref, o_ref):
    n_b, n_h, n_rows = q_ref.shape[:3]
    length = cache_ref.shape[-1]

    def step_of(shape):
        return lax.broadcasted_iota(jnp.int32, shape, 0) // A_GROUP

    valid_c = lax.broadcasted_iota(jnp.int32, (n_rows, length), 1) >= step_of((n_rows, length))
    valid_n = (lax.broadcasted_iota(jnp.int32, (n_rows, STEP_NEW_ROWS), 1)
               <= step_of((n_rows, STEP_NEW_ROWS)))

    def slot(i, carry):
        b, h = i // n_h, i % n_h
        q = q_ref[b, h].astype(_BF16)
        s_c = jnp.dot(q, cache_ref[b, 0, h].astype(_BF16), preferred_element_type=_F32)
        s_n = lax.dot_general(q, kn_ref[b, h].astype(_BF16), _NT, preferred_element_type=_F32)
        s_c = jnp.where(valid_c, s_c, _NEG_INF)
        s_n = jnp.where(valid_n, s_n, _NEG_INF)
        sink = sink_ref[h]
        m = jnp.maximum(jnp.max(s_c, axis=-1, keepdims=True), jnp.max(s_n, axis=-1, keepdims=True))
        m = jnp.maximum(m, sink)
        e_c = jnp.exp(s_c - m)
        e_n = jnp.exp(s_n - m)
        den = (jnp.sum(e_c, axis=-1, keepdims=True) + jnp.sum(e_n, axis=-1, keepdims=True)
               + jnp.exp(sink - m))
        o = lax.dot_general((e_c / den).astype(_BF16), cache_ref[b, 1, h].astype(_BF16), _NT,
                            preferred_element_type=_F32)
        o_ref[b, h] = o + jnp.dot((e_n / den).astype(_BF16), vn_ref[b, h].astype(_BF16),
                                  preferred_element_type=_F32)
        return carry

    lax.fori_loop(0, n_b * n_h, slot, 0, unroll=8)


def _slots(x, db, n_heads):
    x = x.reshape(db, DEC_SEQ, n_heads, HEAD_DIM).transpose(0, 2, 1, 3)
    return jnp.pad(x, [(0, 0), (0, 0), (0, STEP_NEW_ROWS - DEC_SEQ), (0, 0)])


def _step_b_kernel(q_ref, kn_ref, vn_ref, cache_ref, o_ref, lse_ref, *, dil):
    n_h, length = cache_ref.shape[2], cache_ref.shape[4]
    width = n_h * HEAD_DIM
    n_rows = n_h * STEP_NEW_ROWS
    seg = lax.broadcasted_iota(jnp.int32, (STEP_NEW_ROWS, width), 1) // HEAD_DIM

    def per_head(fn):
        return [fn(h) for h in range(n_h)]

    def own_head(blocks):
        return functools.reduce(jnp.add, [jnp.where(seg == h, blk, 0.0) for h, blk in enumerate(blocks)])

    q_tok = q_ref[0]
    q_bd = jnp.concatenate(per_head(lambda h: jnp.where(seg == h, q_tok, 0.0)), axis=0).astype(_BF16)
    kt = cache_ref[0, 0].reshape(width, length).astype(_BF16)
    vt = cache_ref[0, 1].reshape(width, length).astype(_BF16)
    s_c = jnp.dot(q_bd, kt, preferred_element_type=_F32)
    s_n = lax.dot_general(q_bd, kn_ref[0].astype(_BF16), _NT, preferred_element_type=_F32)

    def step_of(shape):
        return (lax.broadcasted_iota(jnp.int32, shape, 0) % STEP_NEW_ROWS) % DEC_SEQ

    col_c = lax.broadcasted_iota(jnp.int32, (n_rows, length), 1)
    col_n = lax.broadcasted_iota(jnp.int32, (n_rows, STEP_NEW_ROWS), 1)
    if dil == 1:
        valid_c = col_c >= step_of((n_rows, length))
        valid_n = col_n <= step_of((n_rows, STEP_NEW_ROWS))
    else:
        valid_c = (col_c % dil) == step_of((n_rows, length))
        valid_n = col_n == step_of((n_rows, STEP_NEW_ROWS))
    s_c = jnp.where(valid_c, s_c, _NEG_INF)
    s_n = jnp.where(valid_n, s_n, _NEG_INF)
    m = jnp.maximum(jnp.max(s_c, axis=-1, keepdims=True), jnp.max(s_n, axis=-1, keepdims=True))
    e_c = jnp.exp(s_c - m)
    e_n = jnp.exp(s_n - m)
    den = jnp.sum(e_c, axis=-1, keepdims=True) + jnp.sum(e_n, axis=-1, keepdims=True)
    o_all = lax.dot_general((e_c / den).astype(_BF16), vt, _NT, preferred_element_type=_F32)
    o_all = o_all + jnp.dot((e_n / den).astype(_BF16), vn_ref[0].astype(_BF16),
                            preferred_element_type=_F32)
    rows = lambda x, h: x[h * STEP_NEW_ROWS:(h + 1) * STEP_NEW_ROWS]
    o_ref[0] = own_head(per_head(lambda h: rows(o_all, h)))
    lse = m + jnp.log(den)
    lse_ref[0] = own_head(per_head(lambda h: jnp.broadcast_to(rows(lse, h), (STEP_NEW_ROWS, width))))


def _step_b(q_s, kv_s, cache, dil):
    db, length = cache.shape[:2]
    gw = B_GROUP_WIDTH
    cache_t = cache.transpose(0, 2, 3, 4, 1)
    hb = min(B_HEADS, max(1, STEP_CACHE_BYTES // (2 * HEAD_DIM * length * 4)))
    pad = lambda x: jnp.pad(x.reshape(db, DEC_SEQ, gw), [(0, 0), (0, STEP_NEW_ROWS - DEC_SEQ), (0, 0)])
    tok_spec = pl.BlockSpec((1, STEP_NEW_ROWS, hb * HEAD_DIM), lambda b, j: (b, 0, j))
    tok_sds = jax.ShapeDtypeStruct((db, STEP_NEW_ROWS, gw), _F32)
    o, lse = pl.pallas_call(
        functools.partial(_step_b_kernel, dil=dil),
        grid=(db, B_HEADS // hb),
        in_specs=[tok_spec, tok_spec, tok_spec,
                  pl.BlockSpec((1, 2, hb, HEAD_DIM, length), lambda b, j: (b, 0, j, 0, 0))],
        out_specs=[tok_spec, tok_spec],
        out_shape=[tok_sds, tok_sds],
        compiler_params=_cparams("parallel", "parallel"),
        name="step_b_l%d" % length,
    )(pad(q_s.astype(_F32)), pad(kv_s[:, :gw]), pad(kv_s[:, gw:]), cache_t)
    back = lambda x: x[:, :DEC_SEQ].reshape(db * DEC_SEQ, gw)
    return back(o), back(lse)


def _step_a(qkv_s, cache, sinks, db):
    nq = A_HEADS * HEAD_DIM
    nk = A_KV_HEADS * HEAD_DIM
    q = qkv_s[:, :nq].reshape(db, DEC_SEQ, A_KV_HEADS, A_GROUP, HEAD_DIM)
    q = q.transpose(0, 2, 1, 3, 4).reshape(db, A_KV_HEADS, DEC_SEQ * A_GROUP, HEAD_DIM)
    sink_rows = jnp.tile(sinks.reshape(A_KV_HEADS, 1, A_GROUP), (1, DEC_SEQ, 1))
    sink_rows = sink_rows.reshape(A_KV_HEADS, DEC_SEQ * A_GROUP, 1)
    bb = STEP_A_SAMPLES
    q_spec = pl.BlockSpec((bb,) + q.shape[1:], lambda i: (i, 0, 0, 0))
    n_spec = pl.BlockSpec((bb, A_KV_HEADS, STEP_NEW_ROWS, HEAD_DIM), lambda i: (i, 0, 0, 0))
    o = pl.pallas_call(
        _step_a_kernel,
        grid=(db // bb,),
        in_specs=[q_spec, n_spec, n_spec,
                  pl.BlockSpec((bb, 2, A_KV_HEADS, HEAD_DIM, A_WINDOW), lambda i: (i, 0, 0, 0, 0)),
                  pl.BlockSpec(sink_rows.shape, lambda i: (0, 0, 0))],
        out_specs=q_spec,
        out_shape=jax.ShapeDtypeStruct(q.shape, _F32),
        compiler_params=_cparams("parallel"),
        name="step_a",
    )(q, _slots(qkv_s[:, nq:nq + nk], db, A_KV_HEADS), _slots(qkv_s[:, nq + nk:], db, A_KV_HEADS),
      cache.transpose(0, 2, 3, 4, 1), sink_rows)
    o = o.reshape(db, A_KV_HEADS, DEC_SEQ, A_GROUP, HEAD_DIM).transpose(0, 2, 1, 3, 4)
    return o.reshape(db * DEC_SEQ, nq)


def _layer_norm_rows(z, g, b):
    mu = jnp.mean(z, axis=-1, keepdims=True)
    zc = z - mu
    var = jnp.mean(zc * zc, axis=-1, keepdims=True)
    return zc * lax.rsqrt(var + LN_EPS) * g + b


def _proj_ln_kernel(*refs, dils, has_bias):
    n_merge = len(dils)
    n_in = 2 * n_merge if n_merge else 1
    w_ref = refs[n_in]
    pos = n_in + 1
    if has_bias:
        bias_ref = refs[pos]
        pos += 1
    x_ref, g_ref, b_ref, out_ref = refs[pos:pos + 4]
    scratch = refs[pos + 4:]
    tm = x_ref.shape[0]

    if n_merge:
        o_scr, order_scr = scratch[0], scratch[1:]

        def token_order(idx, ref, d, lanes):
            if d == 1:
                return ref[:, lanes]
            scr = order_scr[idx]
            for r in range(d):
                scr[pl.ds(r, tm // d, stride=d), :] = ref[r, :, lanes]
            return scr[...]

        reordered = [i for i, d in enumerate(dils) if d > 1]
        for g in range(w_ref.shape[0] // LANES):
            lanes = slice(g * LANES, (g + 1) * LANES)
            os = [token_order(2 * reordered.index(i) if d > 1 else 0, refs[i], d, lanes)
                  for i, d in enumerate(dils)]
            ls = [token_order(2 * reordered.index(i) + 1 if d > 1 else 0, refs[n_merge + i], d, lanes)
                  for i, d in enumerate(dils)]
            m = functools.reduce(jnp.maximum, ls)
            es = [jnp.exp(l - m) for l in ls]
            den = functools.reduce(jnp.add, es)
            o_scr[:, lanes] = functools.reduce(
                jnp.add, [(e / den) * o for e, o in zip(es, os)]).astype(_BF16)
        o = o_scr[...]
    else:
        o = refs[0][...].astype(_BF16)
    y = jnp.dot(o, w_ref[...], preferred_element_type=_F32)
    if has_bias:
        y = y + bias_ref[...]
    out_ref[...] = _layer_norm_rows(DN_ALPHA * x_ref[...] + y, g_ref[...], b_ref[...])


def _proj_ln(o_list, lse_list, dils, w_bf16, bias, x, ln_g, ln_b, tm=256):
    n = x.shape[0]
    k = w_bf16.shape[0]
    row = lambda i: (i, 0)
    fixed = lambda i: (0, 0)
    tiles = SEQ // tm

    def in_spec(d):
        if d == 1:
            return pl.BlockSpec((tm, k), row)
        return pl.BlockSpec((d, tm // d, k), lambda i: (i // tiles, i % tiles, 0))

    args = list(o_list) + list(lse_list) + [w_bf16]
    specs = [in_spec(d) for d in (dils or (1,))] * (2 if lse_list else 1)
    specs.append(pl.BlockSpec((k, D_MODEL), fixed))
    if bias is not None:
        args.append(bias)
        specs.append(pl.BlockSpec((1, D_MODEL), fixed))
    args += [x, ln_g, ln_b]
    specs += [pl.BlockSpec((tm, D_MODEL), row), pl.BlockSpec((1, D_MODEL), fixed),
              pl.BlockSpec((1, D_MODEL), fixed)]
    scratch = []
    if lse_list:
        scratch = [pltpu.VMEM((tm, k), _BF16)]
        scratch += [pltpu.VMEM((tm, LANES), _F32)] * (2 * sum(d > 1 for d in dils))
    return pl.pallas_call(
        functools.partial(_proj_ln_kernel, dils=tuple(dils), has_bias=bias is not None),
        grid=(n // tm,),
        in_specs=specs,
        out_specs=pl.BlockSpec((tm, D_MODEL), row),
        out_shape=jax.ShapeDtypeStruct((n, D_MODEL), _F32),
        scratch_shapes=scratch,
        compiler_params=_cparams("parallel"),
        name="proj_ln",
    )(*args)


N_KEY_BLOCKS = PEER_HEADS * 2
HALF_KEY_DIM = PEER_KEY_DIM // 2


def _peer_score_kernel(x_ref, wq_ref, keys_ref, sc_ref):
    q = jnp.dot(x_ref[...].astype(_BF16), wq_ref[...], preferred_element_type=_F32).astype(_BF16)
    for blk in range(N_KEY_BLOCKS):
        sc_ref[blk * PEER_N_KEYS:(blk + 1) * PEER_N_KEYS, :] = lax.dot_general(
            keys_ref[blk], q[:, blk * HALF_KEY_DIM:(blk + 1) * HALF_KEY_DIM], _NT,
            preferred_element_type=_F32)


def _peer_scores(x, wq_bf16, keys_bf16, layer):
    n = x.shape[0]
    tm = TOKEN_TILE
    return pl.pallas_call(
        _peer_score_kernel,
        grid=(n // tm,),
        in_specs=[
            pl.BlockSpec((tm, D_MODEL), lambda i: (i, 0)),
            pl.BlockSpec((None, D_MODEL, PEER_HEADS * PEER_KEY_DIM), lambda i: (layer, 0, 0)),
            pl.BlockSpec((None, N_KEY_BLOCKS, PEER_N_KEYS, HALF_KEY_DIM), lambda i: (layer, 0, 0, 0)),
        ],
        out_specs=pl.BlockSpec((N_KEY_BLOCKS * PEER_N_KEYS, tm), lambda i: (0, i)),
        out_shape=jax.ShapeDtypeStruct((N_KEY_BLOCKS * PEER_N_KEYS, n), _F32),
        compiler_params=_cparams("parallel"),
        name="peer_scores",
    )(x, wq_bf16, keys_bf16)


TOPK_TOKENS = 128


def _top1_rows(vals, ids, big):
    m = jnp.max(vals, axis=0, keepdims=True)
    sel = jnp.min(jnp.where(vals == m, ids, big), axis=0, keepdims=True)
    return m, sel, ids == sel


def _peer_topk_kernel(sc_ref, e_ref, gate_ref, top_s, top_i, sel_e, sel_g):
    tt = sc_ref.shape[1]
    key_id = lax.broadcasted_iota(jnp.int32, (PEER_N_KEYS, tt), 0).astype(_F32)

    def stage1(blk, carry):
        s = sc_ref[pl.ds(pl.multiple_of(blk * PEER_N_KEYS, PEER_N_KEYS), PEER_N_KEYS), :]
        for k in range(PEER_TOPK):
            m, sel, onehot = _top1_rows(s, key_id, float(PEER_N_KEYS))
            top_s[blk, k:k + 1, :] = m
            top_i[blk, k:k + 1, :] = sel
            s = jnp.where(onehot, _NEG_INF, s)
        return carry

    lax.fori_loop(0, N_KEY_BLOCKS, stage1, 0, unroll=4)

    sub = lax.broadcasted_iota(jnp.int32, (SUBLANES, tt), 0).astype(_F32)
    flat_id = jnp.concatenate(
        [sub, sub + SUBLANES] + [sub + a * PEER_TOPK for a in range(1, SUBLANES)]
        + [(sub + SUBLANES) * PEER_TOPK], axis=0)

    def bcast(row):
        return jnp.broadcast_to(row, (SUBLANES, tt))

    def stage2(h, carry):
        s0, s1 = top_s[2 * h], top_s[2 * h + 1]
        i0, i1 = top_i[2 * h] * PEER_N_KEYS, top_i[2 * h + 1]
        lo_s, hi_s, lo_i, hi_i = s1[:SUBLANES], s1[SUBLANES:], i1[:SUBLANES], i1[SUBLANES:]
        cand = jnp.concatenate(
            [bcast(s0[0:1]) + lo_s, bcast(s0[0:1]) + hi_s]
            + [bcast(s0[a:a + 1]) + lo_s for a in range(1, SUBLANES)]
            + [s0[SUBLANES:] + bcast(s1[0:1])], axis=0)
        cand_e = jnp.concatenate(
            [bcast(i0[0:1]) + lo_i, bcast(i0[0:1]) + hi_i]
            + [bcast(i0[a:a + 1]) + lo_i for a in range(1, SUBLANES)]
            + [i0[SUBLANES:] + bcast(i1[0:1])], axis=0)
        best = []
        for k in range(PEER_TOPK):
            m, _, onehot = _top1_rows(cand, flat_id, float(PEER_TOPK * PEER_TOPK))
            best.append(m)
            sel_e[h, k:k + 1, :] = jnp.max(jnp.where(onehot, cand_e, -1.0), axis=0, keepdims=True)
            cand = jnp.where(onehot, _NEG_INF, cand)
        ex = [jnp.exp(b - best[0]) for b in best]
        den = functools.reduce(jnp.add, ex)
        for k in range(PEER_TOPK):
            sel_g[h, k:k + 1, :] = ex[k] / den
        return carry

    lax.fori_loop(0, PEER_HEADS, stage2, 0, unroll=4)
    n_sel = PEER_HEADS * PEER_TOPK
    e_ref[...] = sel_e[...].reshape(n_sel, tt).T.astype(jnp.int32)
    gate_ref[...] = sel_g[...].reshape(n_sel, tt).T


def _peer_topk(sc_t):
    n = sc_t.shape[1]
    tt = TOPK_TOKENS
    n_sel = PEER_HEADS * PEER_TOPK
    return pl.pallas_call(
        _peer_topk_kernel,
        grid=(n // tt,),
        in_specs=[pl.BlockSpec((N_KEY_BLOCKS * PEER_N_KEYS, tt), lambda i: (0, i))],
        out_specs=[pl.BlockSpec((tt, n_sel), lambda i: (i, 0)),
                   pl.BlockSpec((tt, n_sel), lambda i: (i, 0))],
        out_shape=[jax.ShapeDtypeStruct((n, n_sel), jnp.int32),
                   jax.ShapeDtypeStruct((n, n_sel), _F32)],
        scratch_shapes=[pltpu.VMEM((N_KEY_BLOCKS, PEER_TOPK, tt), _F32),
                        pltpu.VMEM((N_KEY_BLOCKS, PEER_TOPK, tt), _F32),
                        pltpu.VMEM((PEER_HEADS, PEER_TOPK, tt), _F32),
                        pltpu.VMEM((PEER_HEADS, PEER_TOPK, tt), _F32)],
        compiler_params=_cparams("parallel"),
        name="peer_topk",
    )(sc_t)


GATE_TOKENS = 64
GATE_UNROLL = 16


def _peer_gate_kernel(e_ref, gate_ref, g_ref, scr):
    tb, _, n_sel = e_ref.shape
    row_id = lax.broadcasted_iota(jnp.int32, (PEER_N_KEYS, n_sel), 0)

    def body(n, carry):
        e = e_ref[n]
        gate = gate_ref[n]
        hi = gate.astype(_BF16).astype(_F32)
        lo = gate - hi
        is_i = row_id == (e >> 7)
        lhs = jnp.concatenate([jnp.where(is_i, hi, 0.0), jnp.where(is_i, lo, 0.0)], axis=1)
        one = jnp.where(row_id == (e & (PEER_N_KEYS - 1)), 1.0, 0.0)
        rhs = jnp.concatenate([one, one], axis=1)
        res = lax.dot_general(lhs.astype(_BF16), rhs.astype(_BF16), _NT, preferred_element_type=_F32)
        for ib in range(PEER_N_KEYS // SUBLANES):
            start = pl.multiple_of((ib * tb + n) * SUBLANES, SUBLANES)
            scr[pl.ds(start, SUBLANES), :] = res[ib * SUBLANES:(ib + 1) * SUBLANES]
        return carry

    lax.fori_loop(0, tb, body, 0, unroll=GATE_UNROLL)
    for i in range(PEER_N_KEYS):
        start = (i // SUBLANES) * tb * SUBLANES + i % SUBLANES
        g_ref[:, i * PEER_N_KEYS:(i + 1) * PEER_N_KEYS] = scr[pl.ds(start, tb, stride=SUBLANES), :]


def _peer_gates(expert, gate):
    n, n_sel = expert.shape
    tb = GATE_TOKENS
    expert = expert.reshape(n, 1, n_sel)
    gate = gate.reshape(n, 1, n_sel)
    return pl.pallas_call(
        _peer_gate_kernel,
        grid=(n // tb,),
        in_specs=[pl.BlockSpec((tb, 1, n_sel), lambda i: (i, 0, 0)),
                  pl.BlockSpec((tb, 1, n_sel), lambda i: (i, 0, 0))],
        out_specs=pl.BlockSpec((tb, PEER_N_EXPERTS), lambda i: (i, 0)),
        out_shape=jax.ShapeDtypeStruct((n, PEER_N_EXPERTS), _F32),
        scratch_shapes=[pltpu.VMEM((tb * PEER_N_KEYS, PEER_N_KEYS), _F32)],
        compiler_params=_cparams("parallel"),
        name="peer_gates",
    )(expert, gate)


EXPERT_CHUNK = 2048
PEER_MAIN_VMEM_LIMIT = 60 * 1024 * 1024
_SQRT_HALF = math.sqrt(0.5)


def _peer_main_kernel(x_ref, ut_ref, g_ref, v_ref, lng_ref, lnb_ref, o_ref, xb_scr):
    c = pl.program_id(1)

    @pl.when(c == 0)
    def _():
        xb_scr[...] = x_ref[...].astype(_BF16)
        o_ref[...] = jnp.zeros_like(o_ref)

    h_parts = []
    for k in range(ut_ref.shape[1] // MXU_COLS):
        cols = slice(k * MXU_COLS, (k + 1) * MXU_COLS)
        a = jnp.dot(xb_scr[...], ut_ref[:, cols], preferred_element_type=_F32)
        gelu = 0.5 * a * (1.0 + lax.erf(a * _SQRT_HALF))
        h_parts.append((g_ref[:, cols] * gelu).astype(_BF16))
    h = jnp.concatenate(h_parts, axis=1)
    o_ref[...] += jnp.dot(h, v_ref[...], preferred_element_type=_F32)

    @pl.when(c == pl.num_programs(1) - 1)
    def _():
        o_ref[...] = _layer_norm_rows(DN_ALPHA * x_ref[...] + o_ref[...], lng_ref[...], lnb_ref[...])


def _peer_main(x, ut_bf16, gates, v_bf16, ln_g, ln_b, layer):
    n = x.shape[0]
    tm, ce = TOKEN_TILE, EXPERT_CHUNK
    return pl.pallas_call(
        _peer_main_kernel,
        grid=(n // tm, PEER_N_EXPERTS // ce),
        in_specs=[
            pl.BlockSpec((tm, D_MODEL), lambda i, c: (i, 0), pipeline_mode=pl.Buffered(1)),
            pl.BlockSpec((None, D_MODEL, ce), lambda i, c: (layer, 0, c)),
            pl.BlockSpec((tm, ce), lambda i, c: (i, c)),
            pl.BlockSpec((None, ce, D_MODEL), lambda i, c: (layer, c, 0)),
            pl.BlockSpec((1, D_MODEL), lambda i, c: (0, 0)),
            pl.BlockSpec((1, D_MODEL), lambda i, c: (0, 0)),
        ],
        out_specs=pl.BlockSpec((tm, D_MODEL), lambda i, c: (i, 0), pipeline_mode=pl.Buffered(1)),
        out_shape=jax.ShapeDtypeStruct((n, D_MODEL), _F32),
        scratch_shapes=[pltpu.VMEM((tm, D_MODEL), _BF16)],
        compiler_params=pltpu.CompilerParams(dimension_semantics=("parallel", "arbitrary"),
                                             vmem_limit_bytes=PEER_MAIN_VMEM_LIMIT),
        name="peer_main",
    )(x, ut_bf16, gates, v_bf16, ln_g, ln_b)


def _peer_ffn_ln(x, wq_bf16, keys_bf16, ut_bf16, v_bf16, ln_g, ln_b, layer):
    sc_t = _peer_scores(x, wq_bf16, keys_bf16, layer)
    expert, gate = _peer_topk(sc_t)
    gates = _peer_gates(expert, gate)
    return _peer_main(x, ut_bf16, gates, v_bf16, ln_g, ln_b, layer)


def _rope_tables(pos):
    half = HEAD_DIM // 2
    inv = ROPE_THETA ** (-jnp.arange(half, dtype=_F32) / half)
    ang = pos.astype(_F32)[:, None] * inv[None, :]
    cos, sin = jnp.cos(ang), jnp.sin(ang)
    cos = jnp.tile(cos, (1, LANES // half))
    sin = jnp.tile(jnp.concatenate([-sin, sin], axis=1), (1, LANES // HEAD_DIM))
    return cos, sin


def kernel(x_prompt, x_sample, cache_a_kv, cache_b1_kv, cache_b2_kv, cache_b3_kv, w_qkv_a, b_qkv_a, sinks_a, w_o_a, b_o_a, w_qkv_b, w_o_b, ln_mix_g, ln_mix_b, ln_ffn_g, ln_ffn_b, peer_w_query, peer_sub_keys, peer_u, peer_v):
    n_batch, seq = x_prompt.shape[:2]
    db, t_len = x_sample.shape[:2]
    xp = x_prompt.reshape(n_batch * seq, D_MODEL)
    xs = x_sample.reshape(db * t_len, D_MODEL)
    cos_p, sin_p = _rope_tables(jnp.tile(jnp.arange(seq, dtype=jnp.int32), n_batch))
    cos_s, sin_s = _rope_tables(jnp.tile(PAST_LEN + jnp.arange(t_len, dtype=jnp.int32), db))
    scale = HEAD_DIM ** -0.5
    row = lambda v: v.reshape(1, -1)
    caches_b = (cache_b1_kv, cache_b2_kv, cache_b3_kv)
    outs_kv = {}
    wq_all = peer_w_query.astype(_BF16)
    keys_all = peer_sub_keys.reshape(DEPTH, N_KEY_BLOCKS, PEER_N_KEYS, HALF_KEY_DIM).astype(_BF16)
    ut_all = jnp.swapaxes(peer_u, 1, 2).astype(_BF16)
    v_all = peer_v.astype(_BF16)

    for layer in range(DEPTH):
        i = layer // 2
        ln_g, ln_b = row(ln_mix_g[layer]), row(ln_mix_b[layer])
        if layer % 2 == 0:
            nq = A_HEADS * HEAD_DIM
            nk = A_KV_HEADS * HEAD_DIM
            col_scale = jnp.concatenate([jnp.full((nq,), scale, _F32), jnp.ones((2 * nk,), _F32)])
            w = (w_qkv_a[i] * col_scale).astype(_BF16)
            bias = row(b_qkv_a[i] * col_scale)
            qkv_p = _qkv_rope(xp, w, bias, cos_p, sin_p, nq + nk)
            qkv_s = _qkv_rope(xs, w, bias, cos_s, sin_s, nq + nk)
            o_p = _band_attention_a(qkv_p, sinks_a[i], n_batch)
            o_s = _step_a(qkv_s, cache_a_kv[i], sinks_a[i], db)
            wo = w_o_a[i].astype(_BF16)
            xp = _proj_ln([o_p], [], (), wo, row(b_o_a[i]), xp, ln_g, ln_b)
            xs = _proj_ln([o_s], [], (), wo, row(b_o_a[i]), xs, ln_g, ln_b)
            keep = min(A_WINDOW, seq)
            kv_p = qkv_p.reshape(n_batch, seq, A_QKV)[:, seq - keep:, nq:]
            outs_kv.setdefault("a_p", []).append(kv_p.reshape(n_batch, keep, 2, A_KV_HEADS, HEAD_DIM))
            outs_kv.setdefault("a_s", []).append(qkv_s[:, nq:].reshape(db, t_len, 2, A_KV_HEADS, HEAD_DIM))
        else:
            gw = B_GROUP_WIDTH
            col_scale = jnp.concatenate([jnp.full((N_B_GROUPS * gw,), scale, _F32),
                                         jnp.ones((2 * N_B_GROUPS * gw,), _F32)])
            w = (w_qkv_b[i] * col_scale).astype(_BF16)
            o_ps, l_ps, o_ss, l_ss = [], [], [], []
            for g, (window, dil) in enumerate(B_GROUPS):
                outs = _qkv_group(xp, w, cos_p, sin_p, dil, g)
                q_p, kv_p, kvr_p = outs if dil > 1 else (outs[0], outs[1], outs[1])
                q_s, kv_s = _qkv_group(xs, w, cos_s, sin_s, 1, g)
                o, l = _band_attention_b(q_p, kvr_p, dil, n_batch)
                o_ps.append(o)
                l_ps.append(l)
                kv_s5 = kv_s.reshape(db, t_len, 2, B_HEADS, HEAD_DIM)
                o, l = _step_b(q_s, kv_s, caches_b[g][i], dil)
                o_ss.append(o)
                l_ss.append(l)
                keep = min(window, seq)
                kv_keep = kv_p.reshape(n_batch, seq, 2 * gw)[:, seq - keep:]
                outs_kv.setdefault("b%d_p" % g, []).append(
                    kv_keep.reshape(n_batch, keep, 2, B_HEADS, HEAD_DIM))
                outs_kv.setdefault("b%d_s" % g, []).append(kv_s5)
            wo = w_o_b[i].astype(_BF16)
            xp = _proj_ln(o_ps, l_ps, [dil for _, dil in B_GROUPS], wo, None, xp, ln_g, ln_b)
            xs = _proj_ln(o_ss, l_ss, [1] * N_B_GROUPS, wo, None, xs, ln_g, ln_b)

        ffn_g, ffn_b = row(ln_ffn_g[layer]), row(ln_ffn_b[layer])
        xp = _peer_ffn_ln(xp, wq_all, keys_all, ut_all, v_all, ffn_g, ffn_b, layer)
        xs = _peer_ffn_ln(xs, wq_all, keys_all, ut_all, v_all, ffn_g, ffn_b, layer)

    return (xp.reshape(n_batch, seq, D_MODEL), xs.reshape(db, t_len, D_MODEL),
            jnp.stack(outs_kv["a_p"]), jnp.stack(outs_kv["a_s"]),
            jnp.stack(outs_kv["b0_p"]), jnp.stack(outs_kv["b0_s"]),
            jnp.stack(outs_kv["b1_p"]), jnp.stack(outs_kv["b1_s"]),
            jnp.stack(outs_kv["b2_p"]), jnp.stack(outs_kv["b2_s"]))
```

```python
import functools
import math

import jax
import jax.numpy as jnp
from jax import lax
from jax.experimental import pallas as pl
from jax.experimental.pallas import tpu as pltpu

D_MODEL = 2048
SEQ = 2048
DEPTH = 2
DEC_SEQ = 4
PAST_LEN = 8192
HEAD_DIM = 64
ROPE_THETA = 10000.0
BLOCK = 128
A_HEADS = D_MODEL // HEAD_DIM
A_KV_HEADS = A_HEADS // 8
A_GROUP = A_HEADS // A_KV_HEADS
A_WINDOW = 128
A_QKV = (A_HEADS + 2 * A_KV_HEADS) * HEAD_DIM
B_GROUPS = ((128, 1), (512, 4), (2048, 16))
N_B_GROUPS = len(B_GROUPS)
B_HEADS = D_MODEL // (2 * HEAD_DIM)
B_GROUP_WIDTH = B_HEADS * HEAD_DIM
B_KEYS = 128
PEER_HEADS = 8
PEER_N_KEYS = 128
PEER_N_EXPERTS = PEER_N_KEYS * PEER_N_KEYS
PEER_TOPK = 16
PEER_KEY_DIM = 256
DN_ALPHA = (2.0 * DEPTH) ** 0.25
LN_EPS = 1e-5

LANES = 128
SUBLANES = 8
TOKEN_TILE = 512
VMEM_LIMIT = 56 * 1024 * 1024

_F32 = jnp.float32
_BF16 = jnp.bfloat16
_NEG_INF = float("-inf")
_NT = (((1,), (1,)), ((), ()))


def _cparams(*sem):
    return pltpu.CompilerParams(dimension_semantics=sem, vmem_limit_bytes=VMEM_LIMIT)


MXU_COLS = 256


def _rope(blk, cos, sin):
    lane = lax.broadcasted_iota(jnp.int32, blk.shape, 1)
    first_half = (lane % HEAD_DIM) < (HEAD_DIM // 2)
    partner = jnp.where(first_half, pltpu.roll(blk, LANES - HEAD_DIM // 2, 1),
                        pltpu.roll(blk, HEAD_DIM // 2, 1))
    return blk * cos + partner * sin


def _project_chunks(xb, w_ref, emit, bias_ref=None):
    for c in range(w_ref.shape[1] // MXU_COLS):
        cols = slice(c * MXU_COLS, (c + 1) * MXU_COLS)
        acc = jnp.dot(xb, w_ref[:, cols], preferred_element_type=_F32)
        if bias_ref is not None:
            acc = acc + bias_ref[:, cols]
        for h in range(MXU_COLS // LANES):
            emit(c * (MXU_COLS // LANES) + h, acc[:, h * LANES:(h + 1) * LANES])


def _qkv_rope_kernel(x_ref, w_ref, b_ref, cos_ref, sin_ref, o_ref, *, n_rope_groups):
    def emit(g, blk):
        if g < n_rope_groups:
            blk = _rope(blk, cos_ref[...], sin_ref[...])
        o_ref[:, g * LANES:(g + 1) * LANES] = blk

    _project_chunks(x_ref[...].astype(_BF16), w_ref, emit, b_ref)


def _qkv_rope(x, w_bf16, bias, cos, sin, n_rope_cols):
    n, k = x.shape
    ncols = w_bf16.shape[1]
    tm = TOKEN_TILE
    return pl.pallas_call(
        functools.partial(_qkv_rope_kernel, n_rope_groups=n_rope_cols // LANES),
        grid=(n // tm,),
        in_specs=[
            pl.BlockSpec((tm, k), lambda i: (i, 0)),
            pl.BlockSpec((k, ncols), lambda i: (0, 0)),
            pl.BlockSpec((1, ncols), lambda i: (0, 0)),
            pl.BlockSpec((tm, LANES), lambda i: (i, 0)),
            pl.BlockSpec((tm, LANES), lambda i: (i, 0)),
        ],
        out_specs=pl.BlockSpec((tm, ncols), lambda i: (i, 0)),
        out_shape=jax.ShapeDtypeStruct((n, ncols), _F32),
        compiler_params=_cparams("parallel"),
        name="qkv_rope",
    )(x, w_bf16, bias, cos, sin)


def _qkv_group_kernel(x_ref, wq_ref, wk_ref, wv_ref, cos_ref, sin_ref, *refs, dil):
    if dil == 1:
        q_ref, kv_ref = refs
    else:
        q_ref, kv_ref, kvr_ref, scr = refs
    xb = x_ref[...].astype(_BF16)
    per_residue = x_ref.shape[0] // dil
    width = wq_ref.shape[1]

    def emitter(token_ref, residue_ref, rotate, col0, slot):
        def emit(g, blk):
            lanes = slice(col0 + g * LANES, col0 + (g + 1) * LANES)
            if rotate:
                blk = _rope(blk, cos_ref[...], sin_ref[...])
            if token_ref is not None:
                token_ref[:, lanes] = blk.astype(token_ref.dtype)
            if residue_ref is not None:
                scr[slot, g] = blk
                for r in range(dil):
                    residue_ref[r, :, lanes] = (
                        scr[slot, g, pl.ds(r, per_residue, stride=dil), :].astype(residue_ref.dtype))
        return emit

    kvr = kvr_ref if dil > 1 else None
    _project_chunks(xb, wq_ref, emitter(q_ref if dil == 1 else None, q_ref if dil > 1 else None,
                                        True, 0, 0))
    _project_chunks(xb, wk_ref, emitter(kv_ref, kvr, True, 0, 1))
    _project_chunks(xb, wv_ref, emitter(kv_ref, kvr, False, width, 2))


def _qkv_group(x, w_bf16, cos, sin, dil, group):
    n, k = x.shape
    tm, tn = TOKEN_TILE, B_GROUP_WIDTH
    kv_spec = pl.BlockSpec((tm, 2 * tn), lambda i: (i, 0))
    kv_sds = jax.ShapeDtypeStruct((n, 2 * tn), _F32)
    if dil == 1:
        out_specs = [pl.BlockSpec((tm, tn), lambda i: (i, 0)), kv_spec]
        out_shape = [jax.ShapeDtypeStruct((n, tn), _BF16), kv_sds]
        scratch = []
    else:
        tiles = SEQ // tm
        res_rows = (n // SEQ * dil, SEQ // dil)
        res_spec = lambda w: pl.BlockSpec((dil, tm // dil, w), lambda i: (i // tiles, i % tiles, 0))
        out_specs = [res_spec(tn), kv_spec, res_spec(2 * tn)]
        out_shape = [jax.ShapeDtypeStruct(res_rows + (tn,), _BF16), kv_sds,
                     jax.ShapeDtypeStruct(res_rows + (2 * tn,), _BF16)]
        scratch = [pltpu.VMEM((3, tn // LANES, tm, LANES), _F32)]
    w_spec = lambda part: pl.BlockSpec((k, tn), lambda i: (0, part * N_B_GROUPS + group),
                                       pipeline_mode=pl.Buffered(1))
    return pl.pallas_call(
        functools.partial(_qkv_group_kernel, dil=dil),
        grid=(n // tm,),
        in_specs=[
            pl.BlockSpec((tm, k), lambda i: (i, 0)),
            w_spec(0), w_spec(1), w_spec(2),
            pl.BlockSpec((tm, LANES), lambda i: (i, 0)),
            pl.BlockSpec((tm, LANES), lambda i: (i, 0)),
        ],
        out_specs=out_specs,
        out_shape=out_shape,
        scratch_shapes=scratch,
        compiler_params=_cparams("parallel"),
        name="qkv_group_d%d" % dil,
    )(x, w_bf16, w_bf16, w_bf16, cos, sin)


def _band_attn_kernel(sink_ref, q_ref, kc_ref, vc_ref, kp_ref, vp_ref, *out_refs, kv_share,
                      has_sink, with_lse):
    o_ref = out_refs[0]
    lse_ref = out_refs[1] if with_lse else None
    j = pl.program_id(2)
    qi = lax.broadcasted_iota(jnp.int32, (BLOCK, 2 * BLOCK), 0)
    kj = lax.broadcasted_iota(jnp.int32, (BLOCK, 2 * BLOCK), 1)
    first_key = jnp.maximum(qi, jnp.where(j > 0, 0, BLOCK))
    valid = (kj >= first_key) & (kj <= qi + BLOCK)
    lane = lax.broadcasted_iota(jnp.int32, (BLOCK, LANES), 1)
    low = lane < HEAD_DIM

    def keys(cur_ref, prev_ref, kv_head):
        grp = kv_head // 2 if kv_share else kv_head
        lanes = slice(grp * LANES, (grp + 1) * LANES)
        parts = [prev_ref[:, lanes], cur_ref[:, lanes]]
        if kv_share:
            keep = low if kv_head % 2 == 0 else ~low
            parts = [jnp.where(keep, x, pltpu.roll(x, HEAD_DIM, 1)) for x in parts]
        return jnp.concatenate(parts, axis=0).astype(_BF16)

    for p in range(q_ref.shape[1] // LANES):
        if kv_share == 0 or p % kv_share == 0:
            kv_head = p // kv_share if kv_share else p
            kcat = keys(kc_ref, kp_ref, kv_head)
            vcat = keys(vc_ref, vp_ref, kv_head)
        q2 = q_ref[:, p * LANES:(p + 1) * LANES]
        outs, lses = [], []
        for h in range(2):
            qm = jnp.where(low if h == 0 else ~low, q2, 0.0).astype(_BF16)
            s = lax.dot_general(qm, kcat, _NT, preferred_element_type=_F32)
            s = jnp.where(valid, s, _NEG_INF)
            m = jnp.max(s, axis=-1, keepdims=True)
            if has_sink:
                sink = sink_ref[2 * p + h]
                m = jnp.maximum(m, sink)
            e = jnp.exp(s - m)
            den = jnp.sum(e, axis=-1, keepdims=True)
            if has_sink:
                den = den + jnp.exp(sink - m)
            pn = (e / den).astype(_BF16)
            outs.append(jnp.dot(pn, vcat, preferred_element_type=_F32))
            lses.append(m + jnp.log(den))
        o_ref[:, p * LANES:(p + 1) * LANES] = jnp.where(low, outs[0], outs[1])
        if with_lse:
            lse_ref[:, p * LANES:(p + 1) * LANES] = jnp.where(low, lses[0], lses[1])


def _band_attention(sinks, q, kv, *, q_col, k_col, v_col, q_width, kv_width, n_seq, nb, kv_share,
                    with_lse, name):
    cur = lambda b, r, j, s: b * nb + j
    prev = lambda b, r, j, s: b * nb + jnp.maximum(j - 1, 0)
    blk = lambda rows, width, col: pl.BlockSpec((BLOCK, width), lambda *a: (rows(*a), col))
    in_specs = [blk(cur, q_width, q_col), blk(cur, kv_width, k_col), blk(cur, kv_width, v_col),
                blk(prev, kv_width, k_col), blk(prev, kv_width, v_col)]
    args = [q, kv, kv, kv, kv]
    o_spec = blk(cur, q_width, 0)
    o_sds = jax.ShapeDtypeStruct((n_seq * nb * BLOCK, q_width), _F32)
    return pl.pallas_call(
        functools.partial(_band_attn_kernel, kv_share=kv_share,
                          has_sink=sinks is not None, with_lse=with_lse),
        grid_spec=pltpu.PrefetchScalarGridSpec(
            num_scalar_prefetch=1,
            grid=(n_seq, 1, nb),
            in_specs=in_specs,
            out_specs=[o_spec, o_spec] if with_lse else [o_spec],
        ),
        out_shape=[o_sds, o_sds] if with_lse else [o_sds],
        compiler_params=_cparams("parallel", "arbitrary", "arbitrary"),
        name=name,
    )(jnp.zeros((1,), _F32) if sinks is None else sinks, *args)


def _band_attention_a(qkv, sinks, n_batch):
    nq = A_HEADS * HEAD_DIM
    nk = A_KV_HEADS * HEAD_DIM
    (o,) = _band_attention(sinks, qkv, qkv, q_col=0, k_col=nq // nk, v_col=nq // nk + 1,
                           q_width=nq, kv_width=nk, n_seq=n_batch, nb=SEQ // BLOCK,
                           kv_share=A_GROUP // 2, with_lse=False, name="band_attn_a")
    return o


def _band_attention_b(q, kv, dil, n_batch):
    gw = B_GROUP_WIDTH
    o, lse = _band_attention(None, q.reshape(-1, gw), kv.reshape(-1, 2 * gw), q_col=0, k_col=0, v_col=1,
                             q_width=gw, kv_width=gw, n_seq=n_batch * dil, nb=SEQ // dil // BLOCK,
                             kv_share=0, with_lse=True, name="band_attn_d%d" % dil)
    return o.reshape(q.shape), lse.reshape(q.shape)


STEP_NEW_ROWS = 8
STEP_CACHE_BYTES = 8 * 1024 * 1024


STEP_A_SAMPLES = 32


def _step_a_kernel(q_ref, kn_ref, vn_ref, cache_ref, sink_ref, o_ref):
    n_b, n_h, n_rows = q_ref.shape[:3]
    length = cache_ref.shape[-1]

    def step_of(shape):
        return lax.broadcasted_iota(jnp.int32, shape, 0) // A_GROUP

    valid_c = lax.broadcasted_iota(jnp.int32, (n_rows, length), 1) >= step_of((n_rows, length))
    valid_n = (lax.broadcasted_iota(jnp.int32, (n_rows, STEP_NEW_ROWS), 1)
               <= step_of((n_rows, STEP_NEW_ROWS)))

    def slot(i, carry):
        b, h = i // n_h, i % n_h
        q = q_ref[b, h].astype(_BF16)
        s_c = jnp.dot(q, cache_ref[b, 0, h].astype(_BF16), preferred_element_type=_F32)
        s_n = lax.dot_general(q, kn_ref[b, h].astype(_BF16), _NT, preferred_element_type=_F32)
        s_c = jnp.where(valid_c, s_c, _NEG_INF)
        s_n = jnp.where(valid_n, s_n, _NEG_INF)
        sink = sink_ref[h]
        m = jnp.maximum(jnp.max(s_c, axis=-1, keepdims=True), jnp.max(s_n, axis=-1, keepdims=True))
        m = jnp.maximum(m, sink)
        e_c = jnp.exp(s_c - m)
        e_n = jnp.exp(s_n - m)
        den = (jnp.sum(e_c, axis=-1, keepdims=True) + jnp.sum(e_n, axis=-1, keepdims=True)
               + jnp.exp(sink - m))
        o = lax.dot_general((e_c / den).astype(_BF16), cache_ref[b, 1, h].astype(_BF16), _NT,
                            preferred_element_type=_F32)
        o_ref[b, h] = o + jnp.dot((e_n / den).astype(_BF16), vn_ref[b, h].astype(_BF16),
                                  preferred_element_type=_F32)
        return carry

    lax.fori_loop(0, n_b * n_h, slot, 0, unroll=8)


def _slots(x, db, n_heads):
    x = x.reshape(db, DEC_SEQ, n_heads, HEAD_DIM).transpose(0, 2, 1, 3)
    return jnp.pad(x, [(0, 0), (0, 0), (0, STEP_NEW_ROWS - DEC_SEQ), (0, 0)])


def _step_b_kernel(q_ref, kn_ref, vn_ref, cache_ref, o_ref, lse_ref, *, dil):
    n_h, length = cache_ref.shape[2], cache_ref.shape[4]
    width = n_h * HEAD_DIM
    n_rows = n_h * STEP_NEW_ROWS
    seg = lax.broadcasted_iota(jnp.int32, (STEP_NEW_ROWS, width), 1) // HEAD_DIM

    def per_head(fn):
        return [fn(h) for h in range(n_h)]

    def own_head(blocks):
        return functools.reduce(jnp.add, [jnp.where(seg == h, blk, 0.0) for h, blk in enumerate(blocks)])

    def step_of(shape):
        return (lax.broadcasted_iota(jnp.int32, shape, 0) % STEP_NEW_ROWS) % DEC_SEQ

    col_c = lax.broadcasted_iota(jnp.int32, (n_rows, length), 1)
    col_n = lax.broadcasted_iota(jnp.int32, (n_rows, STEP_NEW_ROWS), 1)
    if dil == 1:
        valid_c = col_c >= step_of((n_rows, length))
        valid_n = col_n <= step_of((n_rows, STEP_NEW_ROWS))
    else:
        valid_c = (col_c % dil) == step_of((n_rows, length))
        valid_n = col_n == step_of((n_rows, STEP_NEW_ROWS))
    rows = lambda x, h: x[h * STEP_NEW_ROWS:(h + 1) * STEP_NEW_ROWS]

    for b in range(q_ref.shape[0]):
        q_tok = q_ref[b]
        q_bd = jnp.concatenate(per_head(lambda h: jnp.where(seg == h, q_tok, 0.0)), axis=0).astype(_BF16)
        kt = cache_ref[b, 0].reshape(width, length).astype(_BF16)
        vt = cache_ref[b, 1].reshape(width, length).astype(_BF16)
        s_c = jnp.dot(q_bd, kt, preferred_element_type=_F32)
        s_n = lax.dot_general(q_bd, kn_ref[b].astype(_BF16), _NT, preferred_element_type=_F32)
        s_c = jnp.where(valid_c, s_c, _NEG_INF)
        s_n = jnp.where(valid_n, s_n, _NEG_INF)
        m = jnp.maximum(jnp.max(s_c, axis=-1, keepdims=True), jnp.max(s_n, axis=-1, keepdims=True))
        e_c = jnp.exp(s_c - m)
        e_n = jnp.exp(s_n - m)
        den = jnp.sum(e_c, axis=-1, keepdims=True) + jnp.sum(e_n, axis=-1, keepdims=True)
        o_all = lax.dot_general((e_c / den).astype(_BF16), vt, _NT, preferred_element_type=_F32)
        o_all = o_all + jnp.dot((e_n / den).astype(_BF16), vn_ref[b].astype(_BF16),
                                preferred_element_type=_F32)
        o_ref[b] = own_head(per_head(lambda h: rows(o_all, h)))
        lse = m + jnp.log(den)
        lse_ref[b] = own_head(per_head(lambda h: jnp.broadcast_to(rows(lse, h), (STEP_NEW_ROWS, width))))


def _step_b(q_s, kv_s, cache, dil):
    db, length = cache.shape[:2]
    gw = B_GROUP_WIDTH
    cache_t = cache.transpose(0, 2, 3, 4, 1)
    head_bytes = 2 * HEAD_DIM * length * 4
    hb = min(B_HEADS, max(1, STEP_CACHE_BYTES // head_bytes))
    bb = min(db, max(1, STEP_CACHE_BYTES // (head_bytes * hb)))
    pad = lambda x: jnp.pad(x.reshape(db, DEC_SEQ, gw), [(0, 0), (0, STEP_NEW_ROWS - DEC_SEQ), (0, 0)])
    tok_spec = pl.BlockSpec((bb, STEP_NEW_ROWS, hb * HEAD_DIM), lambda b, j: (b, 0, j))
    tok_sds = jax.ShapeDtypeStruct((db, STEP_NEW_ROWS, gw), _F32)
    o, lse = pl.pallas_call(
        functools.partial(_step_b_kernel, dil=dil),
        grid=(db // bb, B_HEADS // hb),
        in_specs=[tok_spec, tok_spec, tok_spec,
                  pl.BlockSpec((bb, 2, hb, HEAD_DIM, length), lambda b, j: (b, 0, j, 0, 0))],
        out_specs=[tok_spec, tok_spec],
        out_shape=[tok_sds, tok_sds],
        compiler_params=_cparams("parallel", "parallel"),
        name="step_b_l%d" % length,
    )(pad(q_s.astype(_F32)), pad(kv_s[:, :gw]), pad(kv_s[:, gw:]), cache_t)
    back = lambda x: x[:, :DEC_SEQ].reshape(db * DEC_SEQ, gw)
    return back(o), back(lse)


def _step_a(qkv_s, cache, sinks, db):
    nq = A_HEADS * HEAD_DIM
    nk = A_KV_HEADS * HEAD_DIM
    q = qkv_s[:, :nq].reshape(db, DEC_SEQ, A_KV_HEADS, A_GROUP, HEAD_DIM)
    q = q.transpose(0, 2, 1, 3, 4).reshape(db, A_KV_HEADS, DEC_SEQ * A_GROUP, HEAD_DIM)
    sink_rows = jnp.tile(sinks.reshape(A_KV_HEADS, 1, A_GROUP), (1, DEC_SEQ, 1))
    sink_rows = sink_rows.reshape(A_KV_HEADS, DEC_SEQ * A_GROUP, 1)
    bb = STEP_A_SAMPLES
    q_spec = pl.BlockSpec((bb,) + q.shape[1:], lambda i: (i, 0, 0, 0))
    n_spec = pl.BlockSpec((bb, A_KV_HEADS, STEP_NEW_ROWS, HEAD_DIM), lambda i: (i, 0, 0, 0))
    o = pl.pallas_call(
        _step_a_kernel,
        grid=(db // bb,),
        in_specs=[q_spec, n_spec, n_spec,
                  pl.BlockSpec((bb, 2, A_KV_HEADS, HEAD_DIM, A_WINDOW), lambda i: (i, 0, 0, 0, 0)),
                  pl.BlockSpec(sink_rows.shape, lambda i: (0, 0, 0))],
        out_specs=q_spec,
        out_shape=jax.ShapeDtypeStruct(q.shape, _F32),
        compiler_params=_cparams("parallel"),
        name="step_a",
    )(q, _slots(qkv_s[:, nq:nq + nk], db, A_KV_HEADS), _slots(qkv_s[:, nq + nk:], db, A_KV_HEADS),
      cache.transpose(0, 2, 3, 4, 1), sink_rows)
    o = o.reshape(db, A_KV_HEADS, DEC_SEQ, A_GROUP, HEAD_DIM).transpose(0, 2, 1, 3, 4)
    return o.reshape(db * DEC_SEQ, nq)


def _layer_norm_rows(z, g, b):
    mu = jnp.mean(z, axis=-1, keepdims=True)
    zc = z - mu
    var = jnp.mean(zc * zc, axis=-1, keepdims=True)
    return zc * lax.rsqrt(var + LN_EPS) * g + b


def _proj_ln_kernel(*refs, dils, has_bias):
    n_merge = len(dils)
    n_in = 2 * n_merge if n_merge else 1
    w_ref = refs[n_in]
    pos = n_in + 1
    if has_bias:
        bias_ref = refs[pos]
        pos += 1
    x_ref, g_ref, b_ref, out_ref = refs[pos:pos + 4]
    scratch = refs[pos + 4:]
    tm = x_ref.shape[0]

    if n_merge:
        o_scr, order_scr = scratch[0], scratch[1:]

        def token_order(idx, ref, d, lanes):
            if d == 1:
                return ref[:, lanes]
            scr = order_scr[idx]
            for r in range(d):
                scr[pl.ds(r, tm // d, stride=d), :] = ref[r, :, lanes]
            return scr[...]

        reordered = [i for i, d in enumerate(dils) if d > 1]
        for g in range(w_ref.shape[0] // LANES):
            lanes = slice(g * LANES, (g + 1) * LANES)
            os = [token_order(2 * reordered.index(i) if d > 1 else 0, refs[i], d, lanes)
                  for i, d in enumerate(dils)]
            ls = [token_order(2 * reordered.index(i) + 1 if d > 1 else 0, refs[n_merge + i], d, lanes)
                  for i, d in enumerate(dils)]
            m = functools.reduce(jnp.maximum, ls)
            es = [jnp.exp(l - m) for l in ls]
            den = functools.reduce(jnp.add, es)
            o_scr[:, lanes] = functools.reduce(
                jnp.add, [(e / den) * o for e, o in zip(es, os)]).astype(_BF16)
        o = o_scr[...]
    else:
        o = refs[0][...].astype(_BF16)
    y = jnp.dot(o, w_ref[...], preferred_element_type=_F32)
    if has_bias:
        y = y + bias_ref[...]
    out_ref[...] = _layer_norm_rows(DN_ALPHA * x_ref[...] + y, g_ref[...], b_ref[...])


def _proj_ln(o_list, lse_list, dils, w_bf16, bias, x, ln_g, ln_b, tm=256):
    n = x.shape[0]
    k = w_bf16.shape[0]
    row = lambda i: (i, 0)
    fixed = lambda i: (0, 0)
    tiles = SEQ // tm

    def in_spec(d):
        if d == 1:
            return pl.BlockSpec((tm, k), row)
        return pl.BlockSpec((d, tm // d, k), lambda i: (i // tiles, i % tiles, 0))

    args = list(o_list) + list(lse_list) + [w_bf16]
    specs = [in_spec(d) for d in (dils or (1,))] * (2 if lse_list else 1)
    specs.append(pl.BlockSpec((k, D_MODEL), fixed))
    if bias is not None:
        args.append(bias)
        specs.append(pl.BlockSpec((1, D_MODEL), fixed))
    args += [x, ln_g, ln_b]
    specs += [pl.BlockSpec((tm, D_MODEL), row), pl.BlockSpec((1, D_MODEL), fixed),
              pl.BlockSpec((1, D_MODEL), fixed)]
    scratch = []
    if lse_list:
        scratch = [pltpu.VMEM((tm, k), _BF16)]
        scratch += [pltpu.VMEM((tm, LANES), _F32)] * (2 * sum(d > 1 for d in dils))
    return pl.pallas_call(
        functools.partial(_proj_ln_kernel, dils=tuple(dils), has_bias=bias is not None),
        grid=(n // tm,),
        in_specs=specs,
        out_specs=pl.BlockSpec((tm, D_MODEL), row),
        out_shape=jax.ShapeDtypeStruct((n, D_MODEL), _F32),
        scratch_shapes=scratch,
        compiler_params=_cparams("parallel"),
        name="proj_ln",
    )(*args)


N_KEY_BLOCKS = PEER_HEADS * 2
HALF_KEY_DIM = PEER_KEY_DIM // 2


def _peer_score_kernel(x_ref, wq_ref, keys_ref, sc_ref):
    q = jnp.dot(x_ref[...].astype(_BF16), wq_ref[...], preferred_element_type=_F32).astype(_BF16)
    for blk in range(N_KEY_BLOCKS):
        sc_ref[blk * PEER_N_KEYS:(blk + 1) * PEER_N_KEYS, :] = lax.dot_general(
            keys_ref[blk], q[:, blk * HALF_KEY_DIM:(blk + 1) * HALF_KEY_DIM], _NT,
            preferred_element_type=_F32)


def _peer_scores(x, wq_bf16, keys_bf16, layer):
    n = x.shape[0]
    tm = TOKEN_TILE
    return pl.pallas_call(
        _peer_score_kernel,
        grid=(n // tm,),
        in_specs=[
            pl.BlockSpec((tm, D_MODEL), lambda i: (i, 0)),
            pl.BlockSpec((None, D_MODEL, PEER_HEADS * PEER_KEY_DIM), lambda i: (layer, 0, 0)),
            pl.BlockSpec((None, N_KEY_BLOCKS, PEER_N_KEYS, HALF_KEY_DIM), lambda i: (layer, 0, 0, 0)),
        ],
        out_specs=pl.BlockSpec((N_KEY_BLOCKS * PEER_N_KEYS, tm), lambda i: (0, i)),
        out_shape=jax.ShapeDtypeStruct((N_KEY_BLOCKS * PEER_N_KEYS, n), _F32),
        compiler_params=_cparams("parallel"),
        name="peer_scores",
    )(x, wq_bf16, keys_bf16)


TOPK_TOKENS = 128


def _top1_rows(vals, ids, big):
    m = jnp.max(vals, axis=0, keepdims=True)
    sel = jnp.min(jnp.where(vals == m, ids, big), axis=0, keepdims=True)
    return m, sel, ids == sel


def _peer_topk_kernel(sc_ref, e_ref, gate_ref, top_s, top_i, sel_e, sel_g):
    tt = sc_ref.shape[1]
    key_id = lax.broadcasted_iota(jnp.int32, (PEER_N_KEYS, tt), 0).astype(_F32)

    def stage1(blk, carry):
        s = sc_ref[pl.ds(pl.multiple_of(blk * PEER_N_KEYS, PEER_N_KEYS), PEER_N_KEYS), :]
        for k in range(PEER_TOPK):
            m, sel, onehot = _top1_rows(s, key_id, float(PEER_N_KEYS))
            top_s[blk, k:k + 1, :] = m
            top_i[blk, k:k + 1, :] = sel
            s = jnp.where(onehot, _NEG_INF, s)
        return carry

    lax.fori_loop(0, N_KEY_BLOCKS, stage1, 0, unroll=4)

    sub = lax.broadcasted_iota(jnp.int32, (SUBLANES, tt), 0).astype(_F32)
    flat_id = jnp.concatenate(
        [sub, sub + SUBLANES] + [sub + a * PEER_TOPK for a in range(1, SUBLANES)]
        + [(sub + SUBLANES) * PEER_TOPK], axis=0)

    def bcast(row):
        return jnp.broadcast_to(row, (SUBLANES, tt))

    def stage2(h, carry):
        s0, s1 = top_s[2 * h], top_s[2 * h + 1]
        i0, i1 = top_i[2 * h] * PEER_N_KEYS, top_i[2 * h + 1]
        lo_s, hi_s, lo_i, hi_i = s1[:SUBLANES], s1[SUBLANES:], i1[:SUBLANES], i1[SUBLANES:]
        cand = jnp.concatenate(
            [bcast(s0[0:1]) + lo_s, bcast(s0[0:1]) + hi_s]
            + [bcast(s0[a:a + 1]) + lo_s for a in range(1, SUBLANES)]
            + [s0[SUBLANES:] + bcast(s1[0:1])], axis=0)
        cand_e = jnp.concatenate(
            [bcast(i0[0:1]) + lo_i, bcast(i0[0:1]) + hi_i]
            + [bcast(i0[a:a + 1]) + lo_i for a in range(1, SUBLANES)]
            + [i0[SUBLANES:] + bcast(i1[0:1])], axis=0)
        best = []
        for k in range(PEER_TOPK):
            m, _, onehot = _top1_rows(cand, flat_id, float(PEER_TOPK * PEER_TOPK))
            best.append(m)
            sel_e[h, k:k + 1, :] = jnp.max(jnp.where(onehot, cand_e, -1.0), axis=0, keepdims=True)
            cand = jnp.where(onehot, _NEG_INF, cand)
        ex = [jnp.exp(b - best[0]) for b in best]
        den = functools.reduce(jnp.add, ex)
        for k in range(PEER_TOPK):
            sel_g[h, k:k + 1, :] = ex[k] / den
        return carry

    lax.fori_loop(0, PEER_HEADS, stage2, 0, unroll=4)
    n_sel = PEER_HEADS * PEER_TOPK
    e_ref[...] = sel_e[...].reshape(n_sel, tt).T.astype(jnp.int32)
    gate_ref[...] = sel_g[...].reshape(n_sel, tt).T


def _peer_topk(sc_t):
    n = sc_t.shape[1]
    tt = TOPK_TOKENS
    n_sel = PEER_HEADS * PEER_TOPK
    return pl.pallas_call(
        _peer_topk_kernel,
        grid=(n // tt,),
        in_specs=[pl.BlockSpec((N_KEY_BLOCKS * PEER_N_KEYS, tt), lambda i: (0, i))],
        out_specs=[pl.BlockSpec((tt, n_sel), lambda i: (i, 0)),
                   pl.BlockSpec((tt, n_sel), lambda i: (i, 0))],
        out_shape=[jax.ShapeDtypeStruct((n, n_sel), jnp.int32),
                   jax.ShapeDtypeStruct((n, n_sel), _F32)],
        scratch_shapes=[pltpu.VMEM((N_KEY_BLOCKS, PEER_TOPK, tt), _F32),
                        pltpu.VMEM((N_KEY_BLOCKS, PEER_TOPK, tt), _F32),
                        pltpu.VMEM((PEER_HEADS, PEER_TOPK, tt), _F32),
                        pltpu.VMEM((PEER_HEADS, PEER_TOPK, tt), _F32)],
        compiler_params=_cparams("parallel"),
        name="peer_topk",
    )(sc_t)


GATE_TOKENS = 64
GATE_UNROLL = 16


def _peer_gate_kernel(e_ref, gate_ref, g_ref, scr):
    tb, _, n_sel = e_ref.shape
    row_id = lax.broadcasted_iota(jnp.int32, (PEER_N_KEYS, n_sel), 0)

    def body(n, carry):
        e = e_ref[n]
        gate = gate_ref[n]
        hi = gate.astype(_BF16).astype(_F32)
        lo = gate - hi
        is_i = row_id == (e >> 7)
        lhs = jnp.concatenate([jnp.where(is_i, hi, 0.0), jnp.where(is_i, lo, 0.0)], axis=1)
        one = jnp.where(row_id == (e & (PEER_N_KEYS - 1)), 1.0, 0.0)
        rhs = jnp.concatenate([one, one], axis=1)
        res = lax.dot_general(lhs.astype(_BF16), rhs.astype(_BF16), _NT, preferred_element_type=_F32)
        for ib in range(PEER_N_KEYS // SUBLANES):
            start = pl.multiple_of((ib * tb + n) * SUBLANES, SUBLANES)
            scr[pl.ds(start, SUBLANES), :] = res[ib * SUBLANES:(ib + 1) * SUBLANES]
        return carry

    lax.fori_loop(0, tb, body, 0, unroll=GATE_UNROLL)
    for i in range(PEER_N_KEYS):
        start = (i // SUBLANES) * tb * SUBLANES + i % SUBLANES
        g_ref[:, i * PEER_N_KEYS:(i + 1) * PEER_N_KEYS] = scr[pl.ds(start, tb, stride=SUBLANES), :]


def _peer_gates(expert, gate):
    n, n_sel = expert.shape
    tb = GATE_TOKENS
    expert = expert.reshape(n, 1, n_sel)
    gate = gate.reshape(n, 1, n_sel)
    return pl.pallas_call(
        _peer_gate_kernel,
        grid=(n // tb,),
        in_specs=[pl.BlockSpec((tb, 1, n_sel), lambda i: (i, 0, 0)),
                  pl.BlockSpec((tb, 1, n_sel), lambda i: (i, 0, 0))],
        out_specs=pl.BlockSpec((tb, PEER_N_EXPERTS), lambda i: (i, 0)),
        out_shape=jax.ShapeDtypeStruct((n, PEER_N_EXPERTS), _F32),
        scratch_shapes=[pltpu.VMEM((tb * PEER_N_KEYS, PEER_N_KEYS), _F32)],
        compiler_params=_cparams("parallel"),
        name="peer_gates",
    )(expert, gate)


EXPERT_CHUNK = 1024
_SQRT_HALF = math.sqrt(0.5)


def _peer_main_kernel(x_ref, ut_ref, g_ref, v_ref, lng_ref, lnb_ref, o_ref, xb_scr, acc_scr):
    c = pl.program_id(1)

    @pl.when(c == 0)
    def _():
        xb_scr[...] = x_ref[...].astype(_BF16)
        acc_scr[...] = jnp.zeros_like(acc_scr)

    h_parts = []
    for k in range(ut_ref.shape[1] // MXU_COLS):
        cols = slice(k * MXU_COLS, (k + 1) * MXU_COLS)
        a = jnp.dot(xb_scr[...], ut_ref[:, cols], preferred_element_type=_F32)
        gelu = 0.5 * a * (1.0 + lax.erf(a * _SQRT_HALF))
        h_parts.append((g_ref[:, cols] * gelu).astype(_BF16))
    h = jnp.concatenate(h_parts, axis=1)
    acc_scr[...] += jnp.dot(h, v_ref[...], preferred_element_type=_F32)

    @pl.when(c == pl.num_programs(1) - 1)
    def _():
        o_ref[...] = _layer_norm_rows(DN_ALPHA * x_ref[...] + acc_scr[...], lng_ref[...], lnb_ref[...])


def _peer_main(x, ut_bf16, gates, v_bf16, ln_g, ln_b, layer):
    n = x.shape[0]
    tm, ce = TOKEN_TILE, EXPERT_CHUNK
    return pl.pallas_call(
        _peer_main_kernel,
        grid=(n // tm, PEER_N_EXPERTS // ce),
        in_specs=[
            pl.BlockSpec((tm, D_MODEL), lambda i, c: (i, 0)),
            pl.BlockSpec((None, D_MODEL, ce), lambda i, c: (layer, 0, c)),
            pl.BlockSpec((tm, ce), lambda i, c: (i, c)),
            pl.BlockSpec((None, ce, D_MODEL), lambda i, c: (layer, c, 0)),
            pl.BlockSpec((1, D_MODEL), lambda i, c: (0, 0)),
            pl.BlockSpec((1, D_MODEL), lambda i, c: (0, 0)),
        ],
        out_specs=pl.BlockSpec((tm, D_MODEL), lambda i, c: (i, 0)),
        out_shape=jax.ShapeDtypeStruct((n, D_MODEL), _F32),
        scratch_shapes=[pltpu.VMEM((tm, D_MODEL), _BF16), pltpu.VMEM((tm, D_MODEL), _F32)],
        compiler_params=_cparams("parallel", "arbitrary"),
        name="peer_main",
    )(x, ut_bf16, gates, v_bf16, ln_g, ln_b)


def _peer_ffn_ln(x, wq_bf16, keys_bf16, ut_bf16, v_bf16, ln_g, ln_b, layer):
    sc_t = _peer_scores(x, wq_bf16, keys_bf16, layer)
    expert, gate = _peer_topk(sc_t)
    gates = _peer_gates(expert, gate)
    return _peer_main(x, ut_bf16, gates, v_bf16, ln_g, ln_b, layer)


def _rope_tables(pos):
    half = HEAD_DIM // 2
    inv = ROPE_THETA ** (-jnp.arange(half, dtype=_F32) / half)
    ang = pos.astype(_F32)[:, None] * inv[None, :]
    cos, sin = jnp.cos(ang), jnp.sin(ang)
    cos = jnp.tile(cos, (1, LANES // half))
    sin = jnp.tile(jnp.concatenate([-sin, sin], axis=1), (1, LANES // HEAD_DIM))
    return cos, sin


def kernel(x_prompt, x_sample, cache_a_kv, cache_b1_kv, cache_b2_kv, cache_b3_kv, w_qkv_a, b_qkv_a, sinks_a, w_o_a, b_o_a, w_qkv_b, w_o_b, ln_mix_g, ln_mix_b, ln_ffn_g, ln_ffn_b, peer_w_query, peer_sub_keys, peer_u, peer_v):
    n_batch, seq = x_prompt.shape[:2]
    db, t_len = x_sample.shape[:2]
    xp = x_prompt.reshape(n_batch * seq, D_MODEL)
    xs = x_sample.reshape(db * t_len, D_MODEL)
    cos_p, sin_p = _rope_tables(jnp.tile(jnp.arange(seq, dtype=jnp.int32), n_batch))
    cos_s, sin_s = _rope_tables(jnp.tile(PAST_LEN + jnp.arange(t_len, dtype=jnp.int32), db))
    scale = HEAD_DIM ** -0.5
    row = lambda v: v.reshape(1, -1)
    caches_b = (cache_b1_kv, cache_b2_kv, cache_b3_kv)
    outs_kv = {}
    wq_all = peer_w_query.astype(_BF16)
    keys_all = peer_sub_keys.reshape(DEPTH, N_KEY_BLOCKS, PEER_N_KEYS, HALF_KEY_DIM).astype(_BF16)
    ut_all = jnp.swapaxes(peer_u, 1, 2).astype(_BF16)
    v_all = peer_v.astype(_BF16)

    for layer in range(DEPTH):
        i = layer // 2
        ln_g, ln_b = row(ln_mix_g[layer]), row(ln_mix_b[layer])
        if layer % 2 == 0:
            nq = A_HEADS * HEAD_DIM
            nk = A_KV_HEADS * HEAD_DIM
            col_scale = jnp.concatenate([jnp.full((nq,), scale, _F32), jnp.ones((2 * nk,), _F32)])
            w = (w_qkv_a[i] * col_scale).astype(_BF16)
            bias = row(b_qkv_a[i] * col_scale)
            qkv_p = _qkv_rope(xp, w, bias, cos_p, sin_p, nq + nk)
            qkv_s = _qkv_rope(xs, w, bias, cos_s, sin_s, nq + nk)
            o_p = _band_attention_a(qkv_p, sinks_a[i], n_batch)
            o_s = _step_a(qkv_s, cache_a_kv[i], sinks_a[i], db)
            wo = w_o_a[i].astype(_BF16)
            xp = _proj_ln([o_p], [], (), wo, row(b_o_a[i]), xp, ln_g, ln_b)
            xs = _proj_ln([o_s], [], (), wo, row(b_o_a[i]), xs, ln_g, ln_b)
            keep = min(A_WINDOW, seq)
            kv_p = qkv_p.reshape(n_batch, seq, A_QKV)[:, seq - keep:, nq:]
            outs_kv.setdefault("a_p", []).append(kv_p.reshape(n_batch, keep, 2, A_KV_HEADS, HEAD_DIM))
            outs_kv.setdefault("a_s", []).append(qkv_s[:, nq:].reshape(db, t_len, 2, A_KV_HEADS, HEAD_DIM))
        else:
            gw = B_GROUP_WIDTH
            col_scale = jnp.concatenate([jnp.full((N_B_GROUPS * gw,), scale, _F32),
                                         jnp.ones((2 * N_B_GROUPS * gw,), _F32)])
            w = (w_qkv_b[i] * col_scale).astype(_BF16)
            o_ps, l_ps, o_ss, l_ss = [], [], [], []
            for g, (window, dil) in enumerate(B_GROUPS):
                outs = _qkv_group(xp, w, cos_p, sin_p, dil, g)
                q_p, kv_p, kvr_p = outs if dil > 1 else (outs[0], outs[1], outs[1])
                q_s, kv_s = _qkv_group(xs, w, cos_s, sin_s, 1, g)
                o, l = _band_attention_b(q_p, kvr_p, dil, n_batch)
                o_ps.append(o)
                l_ps.append(l)
                kv_s5 = kv_s.reshape(db, t_len, 2, B_HEADS, HEAD_DIM)
                o, l = _step_b(q_s, kv_s, caches_b[g][i], dil)
                o_ss.append(o)
                l_ss.append(l)
                keep = min(window, seq)
                kv_keep = kv_p.reshape(n_batch, seq, 2 * gw)[:, seq - keep:]
                outs_kv.setdefault("b%d_p" % g, []).append(
                    kv_keep.reshape(n_batch, keep, 2, B_HEADS, HEAD_DIM))
                outs_kv.setdefault("b%d_s" % g, []).append(kv_s5)
            wo = w_o_b[i].astype(_BF16)
            xp = _proj_ln(o_ps, l_ps, [dil for _, dil in B_GROUPS], wo, None, xp, ln_g, ln_b)
            xs = _proj_ln(o_ss, l_ss, [1] * N_B_GROUPS, wo, None, xs, ln_g, ln_b)

        ffn_g, ffn_b = row(ln_ffn_g[layer]), row(ln_ffn_b[layer])
        xp = _peer_ffn_ln(xp, wq_all, keys_all, ut_all, v_all, ffn_g, ffn_b, layer)
        xs = _peer_ffn_ln(xs, wq_all, keys_all, ut_all, v_all, ffn_g, ffn_b, layer)

    return (xp.reshape(n_batch, seq, D_MODEL), xs.reshape(db, t_len, D_MODEL),
            jnp.stack(outs_kv["a_p"]), jnp.stack(outs_kv["a_s"]),
            jnp.stack(outs_kv["b0_p"]), jnp.stack(outs_kv["b0_s"]),
            jnp.stack(outs_kv["b1_p"]), jnp.stack(outs_kv["b1_s"]),
            jnp.stack(outs_kv["b2_p"]), jnp.stack(outs_kv["b2_s"]))
```

```python
import functools
import math

import jax
import jax.numpy as jnp
from jax import lax
from jax.experimental import pallas as pl
from jax.experimental.pallas import tpu as pltpu

D_MODEL = 2048
SEQ = 2048
DEPTH = 2
DEC_SEQ = 4
PAST_LEN = 8192
HEAD_DIM = 64
ROPE_THETA = 10000.0
BLOCK = 128
A_HEADS = D_MODEL // HEAD_DIM
A_KV_HEADS = A_HEADS // 8
A_GROUP = A_HEADS // A_KV_HEADS
A_WINDOW = 128
A_QKV = (A_HEADS + 2 * A_KV_HEADS) * HEAD_DIM
B_GROUPS = ((128, 1), (512, 4), (2048, 16))
N_B_GROUPS = len(B_GROUPS)
B_HEADS = D_MODEL // (2 * HEAD_DIM)
B_GROUP_WIDTH = B_HEADS * HEAD_DIM
B_KEYS = 128
PEER_HEADS = 8
PEER_N_KEYS = 128
PEER_N_EXPERTS = PEER_N_KEYS * PEER_N_KEYS
PEER_TOPK = 16
PEER_KEY_DIM = 256
DN_ALPHA = (2.0 * DEPTH) ** 0.25
LN_EPS = 1e-5

LANES = 128
SUBLANES = 8
TOKEN_TILE = 512
VMEM_LIMIT = 56 * 1024 * 1024

_F32 = jnp.float32
_BF16 = jnp.bfloat16
_NEG_INF = float("-inf")
_NT = (((1,), (1,)), ((), ()))


def _cparams(*sem):
    return pltpu.CompilerParams(dimension_semantics=sem, vmem_limit_bytes=VMEM_LIMIT)


MXU_COLS = 256


def _rope(blk, cos, sin):
    lane = lax.broadcasted_iota(jnp.int32, blk.shape, 1)
    first_half = (lane % HEAD_DIM) < (HEAD_DIM // 2)
    partner = jnp.where(first_half, pltpu.roll(blk, LANES - HEAD_DIM // 2, 1),
                        pltpu.roll(blk, HEAD_DIM // 2, 1))
    return blk * cos + partner * sin


def _project_chunks(xb, w_ref, emit, bias_ref=None):
    for c in range(w_ref.shape[1] // MXU_COLS):
        cols = slice(c * MXU_COLS, (c + 1) * MXU_COLS)
        acc = jnp.dot(xb, w_ref[:, cols], preferred_element_type=_F32)
        if bias_ref is not None:
            acc = acc + bias_ref[:, cols]
        for h in range(MXU_COLS // LANES):
            emit(c * (MXU_COLS // LANES) + h, acc[:, h * LANES:(h + 1) * LANES])


def _qkv_rope_kernel(x_ref, w_ref, b_ref, cos_ref, sin_ref, o_ref, *, n_rope_groups):
    def emit(g, blk):
        if g < n_rope_groups:
            blk = _rope(blk, cos_ref[...], sin_ref[...])
        o_ref[:, g * LANES:(g + 1) * LANES] = blk

    _project_chunks(x_ref[...].astype(_BF16), w_ref, emit, b_ref)


def _qkv_rope(x, w_bf16, bias, cos, sin, n_rope_cols):
    n, k = x.shape
    ncols = w_bf16.shape[1]
    tm = TOKEN_TILE
    return pl.pallas_call(
        functools.partial(_qkv_rope_kernel, n_rope_groups=n_rope_cols // LANES),
        grid=(n // tm,),
        in_specs=[
            pl.BlockSpec((tm, k), lambda i: (i, 0)),
            pl.BlockSpec((k, ncols), lambda i: (0, 0)),
            pl.BlockSpec((1, ncols), lambda i: (0, 0)),
            pl.BlockSpec((tm, LANES), lambda i: (i, 0)),
            pl.BlockSpec((tm, LANES), lambda i: (i, 0)),
        ],
        out_specs=pl.BlockSpec((tm, ncols), lambda i: (i, 0)),
        out_shape=jax.ShapeDtypeStruct((n, ncols), _F32),
        compiler_params=_cparams("parallel"),
        name="qkv_rope",
    )(x, w_bf16, bias, cos, sin)


def _qkv_group_kernel(x_ref, wq_ref, wk_ref, wv_ref, cos_ref, sin_ref, *refs, dil):
    if dil == 1:
        q_ref, kv_ref = refs
    else:
        q_ref, kv_ref, kvr_ref, scr = refs
    xb = x_ref[...].astype(_BF16)
    per_residue = x_ref.shape[0] // dil
    width = wq_ref.shape[1]

    def emitter(token_ref, residue_ref, rotate, col0, slot):
        def emit(g, blk):
            lanes = slice(col0 + g * LANES, col0 + (g + 1) * LANES)
            if rotate:
                blk = _rope(blk, cos_ref[...], sin_ref[...])
            if token_ref is not None:
                token_ref[:, lanes] = blk.astype(token_ref.dtype)
            if residue_ref is not None:
                scr[slot, g] = blk
                for r in range(dil):
                    residue_ref[r, :, lanes] = (
                        scr[slot, g, pl.ds(r, per_residue, stride=dil), :].astype(residue_ref.dtype))
        return emit

    kvr = kvr_ref if dil > 1 else None
    _project_chunks(xb, wq_ref, emitter(q_ref if dil == 1 else None, q_ref if dil > 1 else None,
                                        True, 0, 0))
    _project_chunks(xb, wk_ref, emitter(kv_ref, kvr, True, 0, 1))
    _project_chunks(xb, wv_ref, emitter(kv_ref, kvr, False, width, 2))


def _qkv_group(x, w_bf16, cos, sin, dil, group):
    n, k = x.shape
    tm, tn = TOKEN_TILE, B_GROUP_WIDTH
    kv_spec = pl.BlockSpec((tm, 2 * tn), lambda i: (i, 0))
    kv_sds = jax.ShapeDtypeStruct((n, 2 * tn), _F32)
    if dil == 1:
        out_specs = [pl.BlockSpec((tm, tn), lambda i: (i, 0)), kv_spec]
        out_shape = [jax.ShapeDtypeStruct((n, tn), _BF16), kv_sds]
        scratch = []
    else:
        tiles = SEQ // tm
        res_rows = (n // SEQ * dil, SEQ // dil)
        res_spec = lambda w: pl.BlockSpec((dil, tm // dil, w), lambda i: (i // tiles, i % tiles, 0))
        out_specs = [res_spec(tn), kv_spec, res_spec(2 * tn)]
        out_shape = [jax.ShapeDtypeStruct(res_rows + (tn,), _BF16), kv_sds,
                     jax.ShapeDtypeStruct(res_rows + (2 * tn,), _BF16)]
        scratch = [pltpu.VMEM((3, tn // LANES, tm, LANES), _F32)]
    w_spec = lambda part: pl.BlockSpec((k, tn), lambda i: (0, part * N_B_GROUPS + group),
                                       pipeline_mode=pl.Buffered(1))
    return pl.pallas_call(
        functools.partial(_qkv_group_kernel, dil=dil),
        grid=(n // tm,),
        in_specs=[
            pl.BlockSpec((tm, k), lambda i: (i, 0)),
            w_spec(0), w_spec(1), w_spec(2),
            pl.BlockSpec((tm, LANES), lambda i: (i, 0)),
            pl.BlockSpec((tm, LANES), lambda i: (i, 0)),
        ],
        out_specs=out_specs,
        out_shape=out_shape,
        scratch_shapes=scratch,
        compiler_params=_cparams("parallel"),
        name="qkv_group_d%d" % dil,
    )(x, w_bf16, w_bf16, w_bf16, cos, sin)


def _band_attn_kernel(sink_ref, q_ref, kc_ref, vc_ref, kp_ref, vp_ref, *out_refs, kv_share,
                      has_sink, with_lse):
    o_ref = out_refs[0]
    lse_ref = out_refs[1] if with_lse else None
    j = pl.program_id(2)
    qi = lax.broadcasted_iota(jnp.int32, (BLOCK, 2 * BLOCK), 0)
    kj = lax.broadcasted_iota(jnp.int32, (BLOCK, 2 * BLOCK), 1)
    first_key = jnp.maximum(qi, jnp.where(j > 0, 0, BLOCK))
    valid = (kj >= first_key) & (kj <= qi + BLOCK)
    lane = lax.broadcasted_iota(jnp.int32, (BLOCK, LANES), 1)
    low = lane < HEAD_DIM

    def keys(cur_ref, prev_ref, kv_head):
        grp = kv_head // 2 if kv_share else kv_head
        lanes = slice(grp * LANES, (grp + 1) * LANES)
        parts = [prev_ref[:, lanes], cur_ref[:, lanes]]
        if kv_share:
            keep = low if kv_head % 2 == 0 else ~low
            parts = [jnp.where(keep, x, pltpu.roll(x, HEAD_DIM, 1)) for x in parts]
        return jnp.concatenate(parts, axis=0).astype(_BF16)

    for p in range(q_ref.shape[1] // LANES):
        if kv_share == 0 or p % kv_share == 0:
            kv_head = p // kv_share if kv_share else p
            kcat = keys(kc_ref, kp_ref, kv_head)
            vcat = keys(vc_ref, vp_ref, kv_head)
        q2 = q_ref[:, p * LANES:(p + 1) * LANES]
        outs, lses = [], []
        for h in range(2):
            qm = jnp.where(low if h == 0 else ~low, q2, 0.0).astype(_BF16)
            s = lax.dot_general(qm, kcat, _NT, preferred_element_type=_F32)
            s = jnp.where(valid, s, _NEG_INF)
            m = jnp.max(s, axis=-1, keepdims=True)
            if has_sink:
                sink = sink_ref[2 * p + h]
                m = jnp.maximum(m, sink)
            e = jnp.exp(s - m)
            den = jnp.sum(e, axis=-1, keepdims=True)
            if has_sink:
                den = den + jnp.exp(sink - m)
            pn = (e / den).astype(_BF16)
            outs.append(jnp.dot(pn, vcat, preferred_element_type=_F32))
            lses.append(m + jnp.log(den))
        o_ref[:, p * LANES:(p + 1) * LANES] = jnp.where(low, outs[0], outs[1])
        if with_lse:
            lse_ref[:, p * LANES:(p + 1) * LANES] = jnp.where(low, lses[0], lses[1])


def _band_attention(sinks, q, kv, *, q_col, k_col, v_col, q_width, kv_width, n_seq, nb, kv_share,
                    with_lse, name):
    cur = lambda b, r, j, s: b * nb + j
    prev = lambda b, r, j, s: b * nb + jnp.maximum(j - 1, 0)
    blk = lambda rows, width, col: pl.BlockSpec((BLOCK, width), lambda *a: (rows(*a), col))
    in_specs = [blk(cur, q_width, q_col), blk(cur, kv_width, k_col), blk(cur, kv_width, v_col),
                blk(prev, kv_width, k_col), blk(prev, kv_width, v_col)]
    args = [q, kv, kv, kv, kv]
    o_spec = blk(cur, q_width, 0)
    o_sds = jax.ShapeDtypeStruct((n_seq * nb * BLOCK, q_width), _F32)
    return pl.pallas_call(
        functools.partial(_band_attn_kernel, kv_share=kv_share,
                          has_sink=sinks is not None, with_lse=with_lse),
        grid_spec=pltpu.PrefetchScalarGridSpec(
            num_scalar_prefetch=1,
            grid=(n_seq, 1, nb),
            in_specs=in_specs,
            out_specs=[o_spec, o_spec] if with_lse else [o_spec],
        ),
        out_shape=[o_sds, o_sds] if with_lse else [o_sds],
        compiler_params=_cparams("parallel", "arbitrary", "arbitrary"),
        name=name,
    )(jnp.zeros((1,), _F32) if sinks is None else sinks, *args)


def _band_attention_a(qkv, sinks, n_batch):
    nq = A_HEADS * HEAD_DIM
    nk = A_KV_HEADS * HEAD_DIM
    (o,) = _band_attention(sinks, qkv, qkv, q_col=0, k_col=nq // nk, v_col=nq // nk + 1,
                           q_width=nq, kv_width=nk, n_seq=n_batch, nb=SEQ // BLOCK,
                           kv_share=A_GROUP // 2, with_lse=False, name="band_attn_a")
    return o


def _band_attention_b(q, kv, dil, n_batch):
    gw = B_GROUP_WIDTH
    o, lse = _band_attention(None, q.reshape(-1, gw), kv.reshape(-1, 2 * gw), q_col=0, k_col=0, v_col=1,
                             q_width=gw, kv_width=gw, n_seq=n_batch * dil, nb=SEQ // dil // BLOCK,
                             kv_share=0, with_lse=True, name="band_attn_d%d" % dil)
    return o.reshape(q.shape), lse.reshape(q.shape)


STEP_NEW_ROWS = 8
STEP_CACHE_BYTES = 8 * 1024 * 1024


STEP_A_SAMPLES = 32


def _step_a_kernel(q_ref, kn_ref, vn_ref, cache_ref, sink_ref, o_ref):
    n_b, n_h, n_rows = q_ref.shape[:3]
    length = cache_ref.shape[-1]

    def step_of(shape):
        return lax.broadcasted_iota(jnp.int32, shape, 0) // A_GROUP

    valid_c = lax.broadcasted_iota(jnp.int32, (n_rows, length), 1) >= step_of((n_rows, length))
    valid_n = (lax.broadcasted_iota(jnp.int32, (n_rows, STEP_NEW_ROWS), 1)
               <= step_of((n_rows, STEP_NEW_ROWS)))

    def slot(i, carry):
        b, h = i // n_h, i % n_h
        q = q_ref[b, h].astype(_BF16)
        s_c = jnp.dot(q, cache_ref[b, 0, h].astype(_BF16), preferred_element_type=_F32)
        s_n = lax.dot_general(q, kn_ref[b, h].astype(_BF16), _NT, preferred_element_type=_F32)
        s_c = jnp.where(valid_c, s_c, _NEG_INF)
        s_n = jnp.where(valid_n, s_n, _NEG_INF)
        sink = sink_ref[h]
        m = jnp.maximum(jnp.max(s_c, axis=-1, keepdims=True), jnp.max(s_n, axis=-1, keepdims=True))
        m = jnp.maximum(m, sink)
        e_c = jnp.exp(s_c - m)
        e_n = jnp.exp(s_n - m)
        den = (jnp.sum(e_c, axis=-1, keepdims=True) + jnp.sum(e_n, axis=-1, keepdims=True)
               + jnp.exp(sink - m))
        o = lax.dot_general((e_c / den).astype(_BF16), cache_ref[b, 1, h].astype(_BF16), _NT,
                            preferred_element_type=_F32)
        o_ref[b, h] = o + jnp.dot((e_n / den).astype(_BF16), vn_ref[b, h].astype(_BF16),
                                  preferred_element_type=_F32)
        return carry

    lax.fori_loop(0, n_b * n_h, slot, 0, unroll=8)


def _slots(x, db, n_heads):
    x = x.reshape(db, DEC_SEQ, n_heads, HEAD_DIM).transpose(0, 2, 1, 3)
    return jnp.pad(x, [(0, 0), (0, 0), (0, STEP_NEW_ROWS - DEC_SEQ), (0, 0)])


def _step_b_kernel(q_ref, kn_ref, vn_ref, cache_ref, o_ref, lse_ref, *, dil):
    n_h, length = cache_ref.shape[2], cache_ref.shape[4]
    width = n_h * HEAD_DIM
    n_rows = n_h * STEP_NEW_ROWS
    seg = lax.broadcasted_iota(jnp.int32, (STEP_NEW_ROWS, width), 1) // HEAD_DIM

    def per_head(fn):
        return [fn(h) for h in range(n_h)]

    def own_head(blocks):
        return functools.reduce(jnp.add, [jnp.where(seg == h, blk, 0.0) for h, blk in enumerate(blocks)])

    def step_of(shape):
        return (lax.broadcasted_iota(jnp.int32, shape, 0) % STEP_NEW_ROWS) % DEC_SEQ

    col_c = lax.broadcasted_iota(jnp.int32, (n_rows, length), 1)
    col_n = lax.broadcasted_iota(jnp.int32, (n_rows, STEP_NEW_ROWS), 1)
    if dil == 1:
        valid_c = col_c >= step_of((n_rows, length))
        valid_n = col_n <= step_of((n_rows, STEP_NEW_ROWS))
    else:
        valid_c = (col_c % dil) == step_of((n_rows, length))
        valid_n = col_n == step_of((n_rows, STEP_NEW_ROWS))
    rows = lambda x, h: x[h * STEP_NEW_ROWS:(h + 1) * STEP_NEW_ROWS]

    for b in range(q_ref.shape[0]):
        q_tok = q_ref[b]
        q_bd = jnp.concatenate(per_head(lambda h: jnp.where(seg == h, q_tok, 0.0)), axis=0).astype(_BF16)
        kt = cache_ref[b, 0].reshape(width, length).astype(_BF16)
        vt = cache_ref[b, 1].reshape(width, length).astype(_BF16)
        s_c = jnp.dot(q_bd, kt, preferred_element_type=_F32)
        s_n = lax.dot_general(q_bd, kn_ref[b].astype(_BF16), _NT, preferred_element_type=_F32)
        s_c = jnp.where(valid_c, s_c, _NEG_INF)
        s_n = jnp.where(valid_n, s_n, _NEG_INF)
        m = jnp.maximum(jnp.max(s_c, axis=-1, keepdims=True), jnp.max(s_n, axis=-1, keepdims=True))
        e_c = jnp.exp(s_c - m)
        e_n = jnp.exp(s_n - m)
        den = jnp.sum(e_c, axis=-1, keepdims=True) + jnp.sum(e_n, axis=-1, keepdims=True)
        o_all = lax.dot_general((e_c / den).astype(_BF16), vt, _NT, preferred_element_type=_F32)
        o_all = o_all + jnp.dot((e_n / den).astype(_BF16), vn_ref[b].astype(_BF16),
                                preferred_element_type=_F32)
        o_ref[b] = own_head(per_head(lambda h: rows(o_all, h)))
        lse = m + jnp.log(den)
        lse_ref[b] = own_head(per_head(lambda h: jnp.broadcast_to(rows(lse, h), (STEP_NEW_ROWS, width))))


def _step_b(q_s, kv_s, cache, dil):
    db, length = cache.shape[:2]
    gw = B_GROUP_WIDTH
    cache_t = cache.transpose(0, 2, 3, 4, 1)
    head_bytes = 2 * HEAD_DIM * length * 4
    hb = min(B_HEADS, max(1, STEP_CACHE_BYTES // head_bytes))
    bb = min(db, max(1, STEP_CACHE_BYTES // (head_bytes * hb)))
    pad = lambda x: jnp.pad(x.reshape(db, DEC_SEQ, gw), [(0, 0), (0, STEP_NEW_ROWS - DEC_SEQ), (0, 0)])
    tok_spec = pl.BlockSpec((bb, STEP_NEW_ROWS, hb * HEAD_DIM), lambda b, j: (b, 0, j))
    tok_sds = jax.ShapeDtypeStruct((db, STEP_NEW_ROWS, gw), _F32)
    o, lse = pl.pallas_call(
        functools.partial(_step_b_kernel, dil=dil),
        grid=(db // bb, B_HEADS // hb),
        in_specs=[tok_spec, tok_spec, tok_spec,
                  pl.BlockSpec((bb, 2, hb, HEAD_DIM, length), lambda b, j: (b, 0, j, 0, 0))],
        out_specs=[tok_spec, tok_spec],
        out_shape=[tok_sds, tok_sds],
        compiler_params=_cparams("parallel", "parallel"),
        name="step_b_l%d" % length,
    )(pad(q_s.astype(_F32)), pad(kv_s[:, :gw]), pad(kv_s[:, gw:]), cache_t)
    back = lambda x: x[:, :DEC_SEQ].reshape(db * DEC_SEQ, gw)
    return back(o), back(lse)


def _step_a(qkv_s, cache, sinks, db):
    nq = A_HEADS * HEAD_DIM
    nk = A_KV_HEADS * HEAD_DIM
    q = qkv_s[:, :nq].reshape(db, DEC_SEQ, A_KV_HEADS, A_GROUP, HEAD_DIM)
    q = q.transpose(0, 2, 1, 3, 4).reshape(db, A_KV_HEADS, DEC_SEQ * A_GROUP, HEAD_DIM)
    sink_rows = jnp.tile(sinks.reshape(A_KV_HEADS, 1, A_GROUP), (1, DEC_SEQ, 1))
    sink_rows = sink_rows.reshape(A_KV_HEADS, DEC_SEQ * A_GROUP, 1)
    bb = STEP_A_SAMPLES
    q_spec = pl.BlockSpec((bb,) + q.shape[1:], lambda i: (i, 0, 0, 0))
    n_spec = pl.BlockSpec((bb, A_KV_HEADS, STEP_NEW_ROWS, HEAD_DIM), lambda i: (i, 0, 0, 0))
    o = pl.pallas_call(
        _step_a_kernel,
        grid=(db // bb,),
        in_specs=[q_spec, n_spec, n_spec,
                  pl.BlockSpec((bb, 2, A_KV_HEADS, HEAD_DIM, A_WINDOW), lambda i: (i, 0, 0, 0, 0)),
                  pl.BlockSpec(sink_rows.shape, lambda i: (0, 0, 0))],
        out_specs=q_spec,
        out_shape=jax.ShapeDtypeStruct(q.shape, _F32),
        compiler_params=_cparams("parallel"),
        name="step_a",
    )(q, _slots(qkv_s[:, nq:nq + nk], db, A_KV_HEADS), _slots(qkv_s[:, nq + nk:], db, A_KV_HEADS),
      cache.transpose(0, 2, 3, 4, 1), sink_rows)
    o = o.reshape(db, A_KV_HEADS, DEC_SEQ, A_GROUP, HEAD_DIM).transpose(0, 2, 1, 3, 4)
    return o.reshape(db * DEC_SEQ, nq)


def _layer_norm_rows(z, g, b):
    mu = jnp.mean(z, axis=-1, keepdims=True)
    zc = z - mu
    var = jnp.mean(zc * zc, axis=-1, keepdims=True)
    return zc * lax.rsqrt(var + LN_EPS) * g + b


def _proj_ln_kernel(*refs, dils, has_bias):
    n_merge = len(dils)
    n_in = 2 * n_merge if n_merge else 1
    w_ref = refs[n_in]
    pos = n_in + 1
    if has_bias:
        bias_ref = refs[pos]
        pos += 1
    x_ref, g_ref, b_ref, out_ref = refs[pos:pos + 4]
    scratch = refs[pos + 4:]
    tm = x_ref.shape[0]

    if n_merge:
        o_scr, order_scr = scratch[0], scratch[1:]

        def token_order(idx, ref, d, lanes):
            if d == 1:
                return ref[:, lanes]
            scr = order_scr[idx]
            for r in range(d):
                scr[pl.ds(r, tm // d, stride=d), :] = ref[r, :, lanes]
            return scr[...]

        reordered = [i for i, d in enumerate(dils) if d > 1]
        for g in range(w_ref.shape[0] // LANES):
            lanes = slice(g * LANES, (g + 1) * LANES)
            os = [token_order(2 * reordered.index(i) if d > 1 else 0, refs[i], d, lanes)
                  for i, d in enumerate(dils)]
            ls = [token_order(2 * reordered.index(i) + 1 if d > 1 else 0, refs[n_merge + i], d, lanes)
                  for i, d in enumerate(dils)]
            m = functools.reduce(jnp.maximum, ls)
            es = [jnp.exp(l - m) for l in ls]
            den = functools.reduce(jnp.add, es)
            o_scr[:, lanes] = functools.reduce(
                jnp.add, [(e / den) * o for e, o in zip(es, os)]).astype(_BF16)
        o = o_scr[...]
    else:
        o = refs[0][...].astype(_BF16)
    y = jnp.dot(o, w_ref[...], preferred_element_type=_F32)
    if has_bias:
        y = y + bias_ref[...]
    out_ref[...] = _layer_norm_rows(DN_ALPHA * x_ref[...] + y, g_ref[...], b_ref[...])


def _proj_ln(o_list, lse_list, dils, w_bf16, bias, x, ln_g, ln_b, tm=256):
    n = x.shape[0]
    k = w_bf16.shape[0]
    row = lambda i: (i, 0)
    fixed = lambda i: (0, 0)
    tiles = SEQ // tm

    def in_spec(d):
        if d == 1:
            return pl.BlockSpec((tm, k), row)
        return pl.BlockSpec((d, tm // d, k), lambda i: (i // tiles, i % tiles, 0))

    args = list(o_list) + list(lse_list) + [w_bf16]
    specs = [in_spec(d) for d in (dils or (1,))] * (2 if lse_list else 1)
    specs.append(pl.BlockSpec((k, D_MODEL), fixed))
    if bias is not None:
        args.append(bias)
        specs.append(pl.BlockSpec((1, D_MODEL), fixed))
    args += [x, ln_g, ln_b]
    specs += [pl.BlockSpec((tm, D_MODEL), row), pl.BlockSpec((1, D_MODEL), fixed),
              pl.BlockSpec((1, D_MODEL), fixed)]
    scratch = []
    if lse_list:
        scratch = [pltpu.VMEM((tm, k), _BF16)]
        scratch += [pltpu.VMEM((tm, LANES), _F32)] * (2 * sum(d > 1 for d in dils))
    return pl.pallas_call(
        functools.partial(_proj_ln_kernel, dils=tuple(dils), has_bias=bias is not None),
        grid=(n // tm,),
        in_specs=specs,
        out_specs=pl.BlockSpec((tm, D_MODEL), row),
        out_shape=jax.ShapeDtypeStruct((n, D_MODEL), _F32),
        scratch_shapes=scratch,
        compiler_params=_cparams("parallel"),
        name="proj_ln",
    )(*args)


N_KEY_BLOCKS = PEER_HEADS * 2
HALF_KEY_DIM = PEER_KEY_DIM // 2


def _peer_score_kernel(x_ref, wq_ref, keys_ref, sc_ref):
    q = jnp.dot(x_ref[...].astype(_BF16), wq_ref[...], preferred_element_type=_F32).astype(_BF16)
    for blk in range(N_KEY_BLOCKS):
        sc_ref[blk * PEER_N_KEYS:(blk + 1) * PEER_N_KEYS, :] = lax.dot_general(
            keys_ref[blk], q[:, blk * HALF_KEY_DIM:(blk + 1) * HALF_KEY_DIM], _NT,
            preferred_element_type=_F32)


def _peer_scores(x, wq_bf16, keys_bf16, layer):
    n = x.shape[0]
    tm = TOKEN_TILE
    return pl.pallas_call(
        _peer_score_kernel,
        grid=(n // tm,),
        in_specs=[
            pl.BlockSpec((tm, D_MODEL), lambda i: (i, 0)),
            pl.BlockSpec((None, D_MODEL, PEER_HEADS * PEER_KEY_DIM), lambda i: (layer, 0, 0)),
            pl.BlockSpec((None, N_KEY_BLOCKS, PEER_N_KEYS, HALF_KEY_DIM), lambda i: (layer, 0, 0, 0)),
        ],
        out_specs=pl.BlockSpec((N_KEY_BLOCKS * PEER_N_KEYS, tm), lambda i: (0, i)),
        out_shape=jax.ShapeDtypeStruct((N_KEY_BLOCKS * PEER_N_KEYS, n), _F32),
        compiler_params=_cparams("parallel"),
        name="peer_scores",
    )(x, wq_bf16, keys_bf16)


TOPK_TOKENS = 128


def _top1_rows(vals, ids, big):
    m = jnp.max(vals, axis=0, keepdims=True)
    sel = jnp.min(jnp.where(vals == m, ids, big), axis=0, keepdims=True)
    return m, sel, ids == sel


def _sort16_network():
    pairs = []

    def merge(lo, n, r):
        step = r * 2
        if step < n:
            merge(lo, n, step)
            merge(lo + r, n, step)
            pairs.extend((i, i + r) for i in range(lo + r, lo + n - r, step))
        else:
            pairs.append((lo, lo + r))

    def sort(lo, n):
        if n > 1:
            sort(lo, n // 2)
            sort(lo + n // 2, n // 2)
            merge(lo, n, 1)

    sort(0, PEER_TOPK)
    return pairs


def _peer_topk_kernel(sc_ref, e_ref, gate_ref, top_s, top_i, sel_e, sel_g):
    tt = sc_ref.shape[1]
    sub = lax.broadcasted_iota(jnp.int32, (SUBLANES, tt), 0).astype(_F32)
    neg_inf = jnp.full((SUBLANES, tt), _NEG_INF, _F32)

    def bcast(row):
        return jnp.broadcast_to(row, (SUBLANES, tt))

    def sub_max(x):
        return jnp.max(x, axis=0, keepdims=True)

    def head_lists(h):
        s0, s1 = top_s[2 * h], top_s[2 * h + 1]
        i0, i1 = top_i[2 * h] * PEER_N_KEYS, top_i[2 * h + 1]
        return s0, s1, i0, i1

    def write_gates(h, best):
        ex = [jnp.exp(b - best[0]) for b in best]
        den = functools.reduce(jnp.add, ex)
        for k in range(PEER_TOPK):
            sel_g[h, k:k + 1, :] = ex[k] / den

    network = _sort16_network()

    def pop_heads(vals, ids, hit, depth):
        for p in range(depth):
            last = p + 1 == len(vals)
            vals[p] = jnp.where(hit, neg_inf if last else vals[p + 1], vals[p])
            if not last:
                ids[p] = jnp.where(hit, ids[p + 1], ids[p])

    def fast_stage1(blk, tie):
        start = pl.multiple_of(blk * PEER_N_KEYS, PEER_N_KEYS)
        vals = [sc_ref[pl.ds(start + SUBLANES * k, SUBLANES), :] for k in range(PEER_TOPK)]
        ids = [sub + float(SUBLANES * k) for k in range(PEER_TOPK)]
        for i, j in network:
            keep = vals[i] >= vals[j]
            vals[i], vals[j] = jnp.maximum(vals[i], vals[j]), jnp.minimum(vals[i], vals[j])
            ids[i], ids[j] = jnp.where(keep, ids[i], ids[j]), jnp.where(keep, ids[j], ids[i])
        prev = None
        for k in range(PEER_TOPK + 1):
            m = sub_max(vals[0])
            hit = vals[0] == m
            n_hit = jnp.sum(jnp.where(hit, 1.0, 0.0), axis=0, keepdims=True)
            tie = jnp.maximum(tie, jnp.where(n_hit > 1.5, 1.0, 0.0))
            if prev is not None:
                tie = jnp.maximum(tie, jnp.where(m == prev, 1.0, 0.0))
            prev = m
            if k == PEER_TOPK:
                break
            top_s[blk, k:k + 1, :] = m
            top_i[blk, k:k + 1, :] = sub_max(jnp.where(hit, ids[0], -1.0))
            pop_heads(vals, ids, hit, PEER_TOPK - k)
        return tie

    def fast_stage2(h, tie):
        s0, s1, i0, i1 = head_lists(h)
        lo_s, hi_s, lo_i, hi_i = s1[:SUBLANES], s1[SUBLANES:], i1[:SUBLANES], i1[SUBLANES:]
        chain = [bcast(s0[a:a + 1]) + lo_s for a in range(SUBLANES)]
        chain_e = [bcast(i0[a:a + 1]) + lo_i for a in range(SUBLANES)]
        single = [bcast(s0[0:1]) + hi_s, s0[SUBLANES:] + bcast(s1[0:1])]
        single_e = [bcast(i0[0:1]) + hi_i, i0[SUBLANES:] + bcast(i1[0:1])]
        best, prev = [], None
        for k in range(PEER_TOPK + 1):
            m = sub_max(jnp.maximum(chain[0], jnp.maximum(single[0], single[1])))
            hits = [chain[0] == m, single[0] == m, single[1] == m]
            n_hit = jnp.sum(functools.reduce(jnp.add, [jnp.where(x, 1.0, 0.0) for x in hits]),
                            axis=0, keepdims=True)
            tie = jnp.maximum(tie, jnp.where(n_hit > 1.5, 1.0, 0.0))
            if prev is not None:
                tie = jnp.maximum(tie, jnp.where(m == prev, 1.0, 0.0))
            prev = m
            if k == PEER_TOPK:
                break
            best.append(m)
            picked = [jnp.where(x, e, -1.0) for x, e in zip(hits, [chain_e[0]] + single_e)]
            sel_e[h, k:k + 1, :] = sub_max(functools.reduce(jnp.maximum, picked))
            pop_heads(chain, chain_e, hits[0], min(SUBLANES, PEER_TOPK - k))
            single = [jnp.where(x, neg_inf, v) for x, v in zip(hits[1:], single)]
        write_gates(h, best)
        return tie

    tie = lax.fori_loop(0, N_KEY_BLOCKS, fast_stage1, jnp.zeros((1, tt), _F32), unroll=2)
    tie = lax.fori_loop(0, PEER_HEADS, fast_stage2, tie, unroll=2)

    @pl.when(jnp.max(tie) > 0.5)
    def _():
        key_id = lax.broadcasted_iota(jnp.int32, (PEER_N_KEYS, tt), 0).astype(_F32)

        def stage1(blk, carry):
            s = sc_ref[pl.ds(pl.multiple_of(blk * PEER_N_KEYS, PEER_N_KEYS), PEER_N_KEYS), :]
            for k in range(PEER_TOPK):
                m, sel, onehot = _top1_rows(s, key_id, float(PEER_N_KEYS))
                top_s[blk, k:k + 1, :] = m
                top_i[blk, k:k + 1, :] = sel
                s = jnp.where(onehot, _NEG_INF, s)
            return carry

        lax.fori_loop(0, N_KEY_BLOCKS, stage1, 0, unroll=2)

        flat_id = jnp.concatenate(
            [sub, sub + SUBLANES] + [sub + a * PEER_TOPK for a in range(1, SUBLANES)]
            + [(sub + SUBLANES) * PEER_TOPK], axis=0)

        def stage2(h, carry):
            s0, s1, i0, i1 = head_lists(h)
            lo_s, hi_s, lo_i, hi_i = s1[:SUBLANES], s1[SUBLANES:], i1[:SUBLANES], i1[SUBLANES:]
            cand = jnp.concatenate(
                [bcast(s0[0:1]) + lo_s, bcast(s0[0:1]) + hi_s]
                + [bcast(s0[a:a + 1]) + lo_s for a in range(1, SUBLANES)]
                + [s0[SUBLANES:] + bcast(s1[0:1])], axis=0)
            cand_e = jnp.concatenate(
                [bcast(i0[0:1]) + lo_i, bcast(i0[0:1]) + hi_i]
                + [bcast(i0[a:a + 1]) + lo_i for a in range(1, SUBLANES)]
                + [i0[SUBLANES:] + bcast(i1[0:1])], axis=0)
            best = []
            for k in range(PEER_TOPK):
                m, _, onehot = _top1_rows(cand, flat_id, float(PEER_TOPK * PEER_TOPK))
                best.append(m)
                sel_e[h, k:k + 1, :] = sub_max(jnp.where(onehot, cand_e, -1.0))
                cand = jnp.where(onehot, _NEG_INF, cand)
            write_gates(h, best)
            return carry

        lax.fori_loop(0, PEER_HEADS, stage2, 0, unroll=2)

    n_sel = PEER_HEADS * PEER_TOPK
    e_ref[...] = sel_e[...].reshape(n_sel, tt).T.astype(jnp.int32)
    gate_ref[...] = sel_g[...].reshape(n_sel, tt).T


def _peer_topk(sc_t):
    n = sc_t.shape[1]
    tt = TOPK_TOKENS
    n_sel = PEER_HEADS * PEER_TOPK
    return pl.pallas_call(
        _peer_topk_kernel,
        grid=(n // tt,),
        in_specs=[pl.BlockSpec((N_KEY_BLOCKS * PEER_N_KEYS, tt), lambda i: (0, i))],
        out_specs=[pl.BlockSpec((tt, n_sel), lambda i: (i, 0)),
                   pl.BlockSpec((tt, n_sel), lambda i: (i, 0))],
        out_shape=[jax.ShapeDtypeStruct((n, n_sel), jnp.int32),
                   jax.ShapeDtypeStruct((n, n_sel), _F32)],
        scratch_shapes=[pltpu.VMEM((N_KEY_BLOCKS, PEER_TOPK, tt), _F32),
                        pltpu.VMEM((N_KEY_BLOCKS, PEER_TOPK, tt), _F32),
                        pltpu.VMEM((PEER_HEADS, PEER_TOPK, tt), _F32),
                        pltpu.VMEM((PEER_HEADS, PEER_TOPK, tt), _F32)],
        compiler_params=_cparams("parallel"),
        name="peer_topk",
    )(sc_t)


GATE_TOKENS = 64
GATE_UNROLL = 16


def _peer_gate_kernel(e_ref, gate_ref, g_ref, scr):
    tb, _, n_sel = e_ref.shape
    row_id = lax.broadcasted_iota(jnp.int32, (PEER_N_KEYS, n_sel), 0)

    def body(n, carry):
        e = e_ref[n]
        gate = gate_ref[n]
        hi = gate.astype(_BF16).astype(_F32)
        lo = gate - hi
        is_i = row_id == (e >> 7)
        lhs = jnp.concatenate([jnp.where(is_i, hi, 0.0), jnp.where(is_i, lo, 0.0)], axis=1)
        one = jnp.where(row_id == (e & (PEER_N_KEYS - 1)), 1.0, 0.0)
        rhs = jnp.concatenate([one, one], axis=1)
        res = lax.dot_general(lhs.astype(_BF16), rhs.astype(_BF16), _NT, preferred_element_type=_F32)
        for ib in range(PEER_N_KEYS // SUBLANES):
            start = pl.multiple_of((ib * tb + n) * SUBLANES, SUBLANES)
            scr[pl.ds(start, SUBLANES), :] = res[ib * SUBLANES:(ib + 1) * SUBLANES]
        return carry

    lax.fori_loop(0, tb, body, 0, unroll=GATE_UNROLL)
    for i in range(PEER_N_KEYS):
        start = (i // SUBLANES) * tb * SUBLANES + i % SUBLANES
        g_ref[:, i * PEER_N_KEYS:(i + 1) * PEER_N_KEYS] = scr[pl.ds(start, tb, stride=SUBLANES), :]


def _peer_gates(expert, gate):
    n, n_sel = expert.shape
    tb = GATE_TOKENS
    expert = expert.reshape(n, 1, n_sel)
    gate = gate.reshape(n, 1, n_sel)
    return pl.pallas_call(
        _peer_gate_kernel,
        grid=(n // tb,),
        in_specs=[pl.BlockSpec((tb, 1, n_sel), lambda i: (i, 0, 0)),
                  pl.BlockSpec((tb, 1, n_sel), lambda i: (i, 0, 0))],
        out_specs=pl.BlockSpec((tb, PEER_N_EXPERTS), lambda i: (i, 0)),
        out_shape=jax.ShapeDtypeStruct((n, PEER_N_EXPERTS), _F32),
        scratch_shapes=[pltpu.VMEM((tb * PEER_N_KEYS, PEER_N_KEYS), _F32)],
        compiler_params=_cparams("parallel"),
        name="peer_gates",
    )(expert, gate)


EXPERT_CHUNK = 1024
_SQRT_HALF = math.sqrt(0.5)


def _peer_main_kernel(x_ref, ut_ref, g_ref, v_ref, lng_ref, lnb_ref, o_ref, xb_scr, acc_scr):
    c = pl.program_id(1)

    @pl.when(c == 0)
    def _():
        xb_scr[...] = x_ref[...].astype(_BF16)
        acc_scr[...] = jnp.zeros_like(acc_scr)

    h_parts = []
    for k in range(ut_ref.shape[1] // MXU_COLS):
        cols = slice(k * MXU_COLS, (k + 1) * MXU_COLS)
        a = jnp.dot(xb_scr[...], ut_ref[:, cols], preferred_element_type=_F32)
        gelu = 0.5 * a * (1.0 + lax.erf(a * _SQRT_HALF))
        h_parts.append((g_ref[:, cols] * gelu).astype(_BF16))
    h = jnp.concatenate(h_parts, axis=1)
    acc_scr[...] += jnp.dot(h, v_ref[...], preferred_element_type=_F32)

    @pl.when(c == pl.num_programs(1) - 1)
    def _():
        o_ref[...] = _layer_norm_rows(DN_ALPHA * x_ref[...] + acc_scr[...], lng_ref[...], lnb_ref[...])


def _peer_main(x, ut_bf16, gates, v_bf16, ln_g, ln_b, layer):
    n = x.shape[0]
    tm, ce = TOKEN_TILE, EXPERT_CHUNK
    return pl.pallas_call(
        _peer_main_kernel,
        grid=(n // tm, PEER_N_EXPERTS // ce),
        in_specs=[
            pl.BlockSpec((tm, D_MODEL), lambda i, c: (i, 0)),
            pl.BlockSpec((None, D_MODEL, ce), lambda i, c: (layer, 0, c)),
            pl.BlockSpec((tm, ce), lambda i, c: (i, c)),
            pl.BlockSpec((None, ce, D_MODEL), lambda i, c: (layer, c, 0)),
            pl.BlockSpec((1, D_MODEL), lambda i, c: (0, 0)),
            pl.BlockSpec((1, D_MODEL), lambda i, c: (0, 0)),
        ],
        out_specs=pl.BlockSpec((tm, D_MODEL), lambda i, c: (i, 0)),
        out_shape=jax.ShapeDtypeStruct((n, D_MODEL), _F32),
        scratch_shapes=[pltpu.VMEM((tm, D_MODEL), _BF16), pltpu.VMEM((tm, D_MODEL), _F32)],
        compiler_params=_cparams("parallel", "arbitrary"),
        name="peer_main",
    )(x, ut_bf16, gates, v_bf16, ln_g, ln_b)


def _peer_ffn_ln(x, wq_bf16, keys_bf16, ut_bf16, v_bf16, ln_g, ln_b, layer):
    sc_t = _peer_scores(x, wq_bf16, keys_bf16, layer)
    expert, gate = _peer_topk(sc_t)
    gates = _peer_gates(expert, gate)
    return _peer_main(x, ut_bf16, gates, v_bf16, ln_g, ln_b, layer)


def _rope_tables(pos):
    half = HEAD_DIM // 2
    inv = ROPE_THETA ** (-jnp.arange(half, dtype=_F32) / half)
    ang = pos.astype(_F32)[:, None] * inv[None, :]
    cos, sin = jnp.cos(ang), jnp.sin(ang)
    cos = jnp.tile(cos, (1, LANES // half))
    sin = jnp.tile(jnp.concatenate([-sin, sin], axis=1), (1, LANES // HEAD_DIM))
    return cos, sin


def kernel(x_prompt, x_sample, cache_a_kv, cache_b1_kv, cache_b2_kv, cache_b3_kv, w_qkv_a, b_qkv_a, sinks_a, w_o_a, b_o_a, w_qkv_b, w_o_b, ln_mix_g, ln_mix_b, ln_ffn_g, ln_ffn_b, peer_w_query, peer_sub_keys, peer_u, peer_v):
    n_batch, seq = x_prompt.shape[:2]
    db, t_len = x_sample.shape[:2]
    xp = x_prompt.reshape(n_batch * seq, D_MODEL)
    xs = x_sample.reshape(db * t_len, D_MODEL)
    cos_p, sin_p = _rope_tables(jnp.tile(jnp.arange(seq, dtype=jnp.int32), n_batch))
    cos_s, sin_s = _rope_tables(jnp.tile(PAST_LEN + jnp.arange(t_len, dtype=jnp.int32), db))
    scale = HEAD_DIM ** -0.5
    row = lambda v: v.reshape(1, -1)
    caches_b = (cache_b1_kv, cache_b2_kv, cache_b3_kv)
    outs_kv = {}
    wq_all = peer_w_query.astype(_BF16)
    keys_all = peer_sub_keys.reshape(DEPTH, N_KEY_BLOCKS, PEER_N_KEYS, HALF_KEY_DIM).astype(_BF16)
    ut_all = jnp.swapaxes(peer_u, 1, 2).astype(_BF16)
    v_all = peer_v.astype(_BF16)

    for layer in range(DEPTH):
        i = layer // 2
        ln_g, ln_b = row(ln_mix_g[layer]), row(ln_mix_b[layer])
        if layer % 2 == 0:
            nq = A_HEADS * HEAD_DIM
            nk = A_KV_HEADS * HEAD_DIM
            col_scale = jnp.concatenate([jnp.full((nq,), scale, _F32), jnp.ones((2 * nk,), _F32)])
            w = (w_qkv_a[i] * col_scale).astype(_BF16)
            bias = row(b_qkv_a[i] * col_scale)
            qkv_p = _qkv_rope(xp, w, bias, cos_p, sin_p, nq + nk)
            qkv_s = _qkv_rope(xs, w, bias, cos_s, sin_s, nq + nk)
            o_p = _band_attention_a(qkv_p, sinks_a[i], n_batch)
            o_s = _step_a(qkv_s, cache_a_kv[i], sinks_a[i], db)
            wo = w_o_a[i].astype(_BF16)
            xp = _proj_ln([o_p], [], (), wo, row(b_o_a[i]), xp, ln_g, ln_b)
            xs = _proj_ln([o_s], [], (), wo, row(b_o_a[i]), xs, ln_g, ln_b)
            keep = min(A_WINDOW, seq)
            kv_p = qkv_p.reshape(n_batch, seq, A_QKV)[:, seq - keep:, nq:]
            outs_kv.setdefault("a_p", []).append(kv_p.reshape(n_batch, keep, 2, A_KV_HEADS, HEAD_DIM))
            outs_kv.setdefault("a_s", []).append(qkv_s[:, nq:].reshape(db, t_len, 2, A_KV_HEADS, HEAD_DIM))
        else:
            gw = B_GROUP_WIDTH
            col_scale = jnp.concatenate([jnp.full((N_B_GROUPS * gw,), scale, _F32),
                                         jnp.ones((2 * N_B_GROUPS * gw,), _F32)])
            w = (w_qkv_b[i] * col_scale).astype(_BF16)
            o_ps, l_ps, o_ss, l_ss = [], [], [], []
            for g, (window, dil) in enumerate(B_GROUPS):
                outs = _qkv_group(xp, w, cos_p, sin_p, dil, g)
                q_p, kv_p, kvr_p = outs if dil > 1 else (outs[0], outs[1], outs[1])
                q_s, kv_s = _qkv_group(xs, w, cos_s, sin_s, 1, g)
                o, l = _band_attention_b(q_p, kvr_p, dil, n_batch)
                o_ps.append(o)
                l_ps.append(l)
                kv_s5 = kv_s.reshape(db, t_len, 2, B_HEADS, HEAD_DIM)
                o, l = _step_b(q_s, kv_s, caches_b[g][i], dil)
                o_ss.append(o)
                l_ss.append(l)
                keep = min(window, seq)
                kv_keep = kv_p.reshape(n_batch, seq, 2 * gw)[:, seq - keep:]
                outs_kv.setdefault("b%d_p" % g, []).append(
                    kv_keep.reshape(n_batch, keep, 2, B_HEADS, HEAD_DIM))
                outs_kv.setdefault("b%d_s" % g, []).append(kv_s5)
            wo = w_o_b[i].astype(_BF16)
            xp = _proj_ln(o_ps, l_ps, [dil for _, dil in B_GROUPS], wo, None, xp, ln_g, ln_b)
            xs = _proj_ln(o_ss, l_ss, [1] * N_B_GROUPS, wo, None, xs, ln_g, ln_b)

        ffn_g, ffn_b = row(ln_ffn_g[layer]), row(ln_ffn_b[layer])
        xp = _peer_ffn_ln(xp, wq_all, keys_all, ut_all, v_all, ffn_g, ffn_b, layer)
        xs = _peer_ffn_ln(xs, wq_all, keys_all, ut_all, v_all, ffn_g, ffn_b, layer)

    return (xp.reshape(n_batch, seq, D_MODEL), xs.reshape(db, t_len, D_MODEL),
            jnp.stack(outs_kv["a_p"]), jnp.stack(outs_kv["a_s"]),
            jnp.stack(outs_kv["b0_p"]), jnp.stack(outs_kv["b0_s"]),
            jnp.stack(outs_kv["b1_p"]), jnp.stack(outs_kv["b1_s"]),
            jnp.stack(outs_kv["b2_p"]), jnp.stack(outs_kv["b2_s"]))
```

```python
import functools
import math

import jax
import jax.numpy as jnp
from jax import lax
from jax.experimental import pallas as pl
from jax.experimental.pallas import tpu as pltpu

D_MODEL = 2048
SEQ = 2048
DEPTH = 2
DEC_SEQ = 4
PAST_LEN = 8192
HEAD_DIM = 64
ROPE_THETA = 10000.0
BLOCK = 128
A_HEADS = D_MODEL // HEAD_DIM
A_KV_HEADS = A_HEADS // 8
A_GROUP = A_HEADS // A_KV_HEADS
A_WINDOW = 128
A_QKV = (A_HEADS + 2 * A_KV_HEADS) * HEAD_DIM
B_GROUPS = ((128, 1), (512, 4), (2048, 16))
N_B_GROUPS = len(B_GROUPS)
B_HEADS = D_MODEL // (2 * HEAD_DIM)
B_GROUP_WIDTH = B_HEADS * HEAD_DIM
B_KEYS = 128
PEER_HEADS = 8
PEER_N_KEYS = 128
PEER_N_EXPERTS = PEER_N_KEYS * PEER_N_KEYS
PEER_TOPK = 16
PEER_KEY_DIM = 256
DN_ALPHA = (2.0 * DEPTH) ** 0.25
LN_EPS = 1e-5

LANES = 128
SUBLANES = 8
TOKEN_TILE = 512
VMEM_LIMIT = 56 * 1024 * 1024

_F32 = jnp.float32
_BF16 = jnp.bfloat16
_NEG_INF = float("-inf")
_NT = (((1,), (1,)), ((), ()))


def _cparams(*sem):
    return pltpu.CompilerParams(dimension_semantics=sem, vmem_limit_bytes=VMEM_LIMIT)


MXU_COLS = 256


def _rope(blk, cos, sin):
    lane = lax.broadcasted_iota(jnp.int32, blk.shape, 1)
    first_half = (lane % HEAD_DIM) < (HEAD_DIM // 2)
    partner = jnp.where(first_half, pltpu.roll(blk, LANES - HEAD_DIM // 2, 1),
                        pltpu.roll(blk, HEAD_DIM // 2, 1))
    return blk * cos + partner * sin


def _project_chunks(xb, w_ref, emit, bias_ref=None):
    for c in range(w_ref.shape[1] // MXU_COLS):
        cols = slice(c * MXU_COLS, (c + 1) * MXU_COLS)
        acc = jnp.dot(xb, w_ref[:, cols], preferred_element_type=_F32)
        if bias_ref is not None:
            acc = acc + bias_ref[:, cols]
        for h in range(MXU_COLS // LANES):
            emit(c * (MXU_COLS // LANES) + h, acc[:, h * LANES:(h + 1) * LANES])


def _qkv_rope_kernel(x_ref, w_ref, b_ref, cos_ref, sin_ref, o_ref, *, n_rope_groups):
    def emit(g, blk):
        if g < n_rope_groups:
            blk = _rope(blk, cos_ref[...], sin_ref[...])
        o_ref[:, g * LANES:(g + 1) * LANES] = blk

    _project_chunks(x_ref[...].astype(_BF16), w_ref, emit, b_ref)


def _qkv_rope(x, w_bf16, bias, cos, sin, n_rope_cols):
    n, k = x.shape
    ncols = w_bf16.shape[1]
    tm = TOKEN_TILE
    return pl.pallas_call(
        functools.partial(_qkv_rope_kernel, n_rope_groups=n_rope_cols // LANES),
        grid=(n // tm,),
        in_specs=[
            pl.BlockSpec((tm, k), lambda i: (i, 0)),
            pl.BlockSpec((k, ncols), lambda i: (0, 0)),
            pl.BlockSpec((1, ncols), lambda i: (0, 0)),
            pl.BlockSpec((tm, LANES), lambda i: (i, 0)),
            pl.BlockSpec((tm, LANES), lambda i: (i, 0)),
        ],
        out_specs=pl.BlockSpec((tm, ncols), lambda i: (i, 0)),
        out_shape=jax.ShapeDtypeStruct((n, ncols), _F32),
        compiler_params=_cparams("parallel"),
        name="qkv_rope",
    )(x, w_bf16, bias, cos, sin)


def _qkv_group_kernel(x_ref, wq_ref, wk_ref, wv_ref, cos_ref, sin_ref, *refs, dil):
    if dil == 1:
        q_ref, kv_ref = refs
    else:
        q_ref, kv_ref, kvr_ref, scr = refs
    xb = x_ref[...].astype(_BF16)
    per_residue = x_ref.shape[0] // dil
    width = wq_ref.shape[1]

    def emitter(token_ref, residue_ref, rotate, col0, slot):
        def emit(g, blk):
            lanes = slice(col0 + g * LANES, col0 + (g + 1) * LANES)
            if rotate:
                blk = _rope(blk, cos_ref[...], sin_ref[...])
            if token_ref is not None:
                token_ref[:, lanes] = blk.astype(token_ref.dtype)
            if residue_ref is not None:
                scr[slot, g] = blk
                for r in range(dil):
                    residue_ref[r, :, lanes] = (
                        scr[slot, g, pl.ds(r, per_residue, stride=dil), :].astype(residue_ref.dtype))
        return emit

    kvr = kvr_ref if dil > 1 else None
    _project_chunks(xb, wq_ref, emitter(q_ref if dil == 1 else None, q_ref if dil > 1 else None,
                                        True, 0, 0))
    _project_chunks(xb, wk_ref, emitter(kv_ref, kvr, True, 0, 1))
    _project_chunks(xb, wv_ref, emitter(kv_ref, kvr, False, width, 2))


def _qkv_group(x, w_bf16, cos, sin, dil, group):
    n, k = x.shape
    tm, tn = TOKEN_TILE, B_GROUP_WIDTH
    kv_spec = pl.BlockSpec((tm, 2 * tn), lambda i: (i, 0))
    kv_sds = jax.ShapeDtypeStruct((n, 2 * tn), _F32)
    if dil == 1:
        out_specs = [pl.BlockSpec((tm, tn), lambda i: (i, 0)), kv_spec]
        out_shape = [jax.ShapeDtypeStruct((n, tn), _BF16), kv_sds]
        scratch = []
    else:
        tiles = SEQ // tm
        res_rows = (n // SEQ * dil, SEQ // dil)
        res_spec = lambda w: pl.BlockSpec((dil, tm // dil, w), lambda i: (i // tiles, i % tiles, 0))
        out_specs = [res_spec(tn), kv_spec, res_spec(2 * tn)]
        out_shape = [jax.ShapeDtypeStruct(res_rows + (tn,), _BF16), kv_sds,
                     jax.ShapeDtypeStruct(res_rows + (2 * tn,), _BF16)]
        scratch = [pltpu.VMEM((3, tn // LANES, tm, LANES), _F32)]
    w_spec = lambda part: pl.BlockSpec((k, tn), lambda i: (0, part * N_B_GROUPS + group),
                                       pipeline_mode=pl.Buffered(1))
    return pl.pallas_call(
        functools.partial(_qkv_group_kernel, dil=dil),
        grid=(n // tm,),
        in_specs=[
            pl.BlockSpec((tm, k), lambda i: (i, 0)),
            w_spec(0), w_spec(1), w_spec(2),
            pl.BlockSpec((tm, LANES), lambda i: (i, 0)),
            pl.BlockSpec((tm, LANES), lambda i: (i, 0)),
        ],
        out_specs=out_specs,
        out_shape=out_shape,
        scratch_shapes=scratch,
        compiler_params=_cparams("parallel"),
        name="qkv_group_d%d" % dil,
    )(x, w_bf16, w_bf16, w_bf16, cos, sin)


def _band_attn_kernel(sink_ref, q_ref, kc_ref, vc_ref, kp_ref, vp_ref, *out_refs, kv_share,
                      has_sink, with_lse):
    o_ref = out_refs[0]
    lse_ref = out_refs[1] if with_lse else None
    j = pl.program_id(2)
    qi = lax.broadcasted_iota(jnp.int32, (BLOCK, 2 * BLOCK), 0)
    kj = lax.broadcasted_iota(jnp.int32, (BLOCK, 2 * BLOCK), 1)
    first_key = jnp.maximum(qi, jnp.where(j > 0, 0, BLOCK))
    valid = (kj >= first_key) & (kj <= qi + BLOCK)
    lane = lax.broadcasted_iota(jnp.int32, (BLOCK, LANES), 1)
    low = lane < HEAD_DIM

    def keys(cur_ref, prev_ref, kv_head):
        grp = kv_head // 2 if kv_share else kv_head
        lanes = slice(grp * LANES, (grp + 1) * LANES)
        parts = [prev_ref[:, lanes], cur_ref[:, lanes]]
        if kv_share:
            keep = low if kv_head % 2 == 0 else ~low
            parts = [jnp.where(keep, x, pltpu.roll(x, HEAD_DIM, 1)) for x in parts]
        return jnp.concatenate(parts, axis=0).astype(_BF16)

    for p in range(q_ref.shape[1] // LANES):
        if kv_share == 0 or p % kv_share == 0:
            kv_head = p // kv_share if kv_share else p
            kcat = keys(kc_ref, kp_ref, kv_head)
            vcat = keys(vc_ref, vp_ref, kv_head)
        q2 = q_ref[:, p * LANES:(p + 1) * LANES]
        outs, lses = [], []
        for h in range(2):
            qm = jnp.where(low if h == 0 else ~low, q2, 0.0).astype(_BF16)
            s = lax.dot_general(qm, kcat, _NT, preferred_element_type=_F32)
            s = jnp.where(valid, s, _NEG_INF)
            m = jnp.max(s, axis=-1, keepdims=True)
            if has_sink:
                sink = sink_ref[2 * p + h]
                m = jnp.maximum(m, sink)
            e = jnp.exp(s - m)
            den = jnp.sum(e, axis=-1, keepdims=True)
            if has_sink:
                den = den + jnp.exp(sink - m)
            pn = (e / den).astype(_BF16)
            outs.append(jnp.dot(pn, vcat, preferred_element_type=_F32))
            lses.append(m + jnp.log(den))
        o_ref[:, p * LANES:(p + 1) * LANES] = jnp.where(low, outs[0], outs[1])
        if with_lse:
            lse_ref[:, p * LANES:(p + 1) * LANES] = jnp.where(low, lses[0], lses[1])


def _band_attention(sinks, q, kv, *, q_col, k_col, v_col, q_width, kv_width, n_seq, nb, kv_share,
                    with_lse, name):
    cur = lambda b, r, j, s: b * nb + j
    prev = lambda b, r, j, s: b * nb + jnp.maximum(j - 1, 0)
    blk = lambda rows, width, col: pl.BlockSpec((BLOCK, width), lambda *a: (rows(*a), col))
    in_specs = [blk(cur, q_width, q_col), blk(cur, kv_width, k_col), blk(cur, kv_width, v_col),
                blk(prev, kv_width, k_col), blk(prev, kv_width, v_col)]
    args = [q, kv, kv, kv, kv]
    o_spec = blk(cur, q_width, 0)
    o_sds = jax.ShapeDtypeStruct((n_seq * nb * BLOCK, q_width), _F32)
    return pl.pallas_call(
        functools.partial(_band_attn_kernel, kv_share=kv_share,
                          has_sink=sinks is not None, with_lse=with_lse),
        grid_spec=pltpu.PrefetchScalarGridSpec(
            num_scalar_prefetch=1,
            grid=(n_seq, 1, nb),
            in_specs=in_specs,
            out_specs=[o_spec, o_spec] if with_lse else [o_spec],
        ),
        out_shape=[o_sds, o_sds] if with_lse else [o_sds],
        compiler_params=_cparams("parallel", "arbitrary", "arbitrary"),
        name=name,
    )(jnp.zeros((1,), _F32) if sinks is None else sinks, *args)


def _band_attention_a(qkv, sinks, n_batch):
    nq = A_HEADS * HEAD_DIM
    nk = A_KV_HEADS * HEAD_DIM
    (o,) = _band_attention(sinks, qkv, qkv, q_col=0, k_col=nq // nk, v_col=nq // nk + 1,
                           q_width=nq, kv_width=nk, n_seq=n_batch, nb=SEQ // BLOCK,
                           kv_share=A_GROUP // 2, with_lse=False, name="band_attn_a")
    return o


def _band_attention_b(q, kv, dil, n_batch):
    gw = B_GROUP_WIDTH
    o, lse = _band_attention(None, q.reshape(-1, gw), kv.reshape(-1, 2 * gw), q_col=0, k_col=0, v_col=1,
                             q_width=gw, kv_width=gw, n_seq=n_batch * dil, nb=SEQ // dil // BLOCK,
                             kv_share=0, with_lse=True, name="band_attn_d%d" % dil)
    return o.reshape(q.shape), lse.reshape(q.shape)


STEP_NEW_ROWS = 8
STEP_CACHE_BYTES = 8 * 1024 * 1024


STEP_A_SAMPLES = 32


def _step_a_kernel(q_ref, kn_ref, vn_ref, cache_ref, sink_ref, o_ref):
    n_b, n_h, n_rows = q_ref.shape[:3]
    length = cache_ref.shape[-1]

    def step_of(shape):
        return lax.broadcasted_iota(jnp.int32, shape, 0) // A_GROUP

    valid_c = lax.broadcasted_iota(jnp.int32, (n_rows, length), 1) >= step_of((n_rows, length))
    valid_n = (lax.broadcasted_iota(jnp.int32, (n_rows, STEP_NEW_ROWS), 1)
               <= step_of((n_rows, STEP_NEW_ROWS)))

    def slot(i, carry):
        b, h = i // n_h, i % n_h
        q = q_ref[b, h].astype(_BF16)
        s_c = jnp.dot(q, cache_ref[b, 0, h].astype(_BF16), preferred_element_type=_F32)
        s_n = lax.dot_general(q, kn_ref[b, h].astype(_BF16), _NT, preferred_element_type=_F32)
        s_c = jnp.where(valid_c, s_c, _NEG_INF)
        s_n = jnp.where(valid_n, s_n, _NEG_INF)
        sink = sink_ref[h]
        m = jnp.maximum(jnp.max(s_c, axis=-1, keepdims=True), jnp.max(s_n, axis=-1, keepdims=True))
        m = jnp.maximum(m, sink)
        e_c = jnp.exp(s_c - m)
        e_n = jnp.exp(s_n - m)
        den = (jnp.sum(e_c, axis=-1, keepdims=True) + jnp.sum(e_n, axis=-1, keepdims=True)
               + jnp.exp(sink - m))
        o = lax.dot_general((e_c / den).astype(_BF16), cache_ref[b, 1, h].astype(_BF16), _NT,
                            preferred_element_type=_F32)
        o_ref[b, h] = o + jnp.dot((e_n / den).astype(_BF16), vn_ref[b, h].astype(_BF16),
                                  preferred_element_type=_F32)
        return carry

    lax.fori_loop(0, n_b * n_h, slot, 0, unroll=8)


def _slots(x, db, n_heads):
    x = x.reshape(db, DEC_SEQ, n_heads, HEAD_DIM).transpose(0, 2, 1, 3)
    return jnp.pad(x, [(0, 0), (0, 0), (0, STEP_NEW_ROWS - DEC_SEQ), (0, 0)])


def _step_b_kernel(q_ref, kn_ref, vn_ref, cache_ref, o_ref, lse_ref, *, dil):
    n_h, length = cache_ref.shape[2], cache_ref.shape[4]
    width = n_h * HEAD_DIM
    n_rows = n_h * STEP_NEW_ROWS
    seg = lax.broadcasted_iota(jnp.int32, (STEP_NEW_ROWS, width), 1) // HEAD_DIM

    def per_head(fn):
        return [fn(h) for h in range(n_h)]

    def own_head(blocks):
        return functools.reduce(jnp.add, [jnp.where(seg == h, blk, 0.0) for h, blk in enumerate(blocks)])

    def step_of(shape):
        return (lax.broadcasted_iota(jnp.int32, shape, 0) % STEP_NEW_ROWS) % DEC_SEQ

    col_c = lax.broadcasted_iota(jnp.int32, (n_rows, length), 1)
    col_n = lax.broadcasted_iota(jnp.int32, (n_rows, STEP_NEW_ROWS), 1)
    if dil == 1:
        valid_c = col_c >= step_of((n_rows, length))
        valid_n = col_n <= step_of((n_rows, STEP_NEW_ROWS))
    else:
        valid_c = (col_c % dil) == step_of((n_rows, length))
        valid_n = col_n == step_of((n_rows, STEP_NEW_ROWS))
    rows = lambda x, h: x[h * STEP_NEW_ROWS:(h + 1) * STEP_NEW_ROWS]

    for b in range(q_ref.shape[0]):
        q_tok = q_ref[b]
        q_bd = jnp.concatenate(per_head(lambda h: jnp.where(seg == h, q_tok, 0.0)), axis=0).astype(_BF16)
        kt = cache_ref[b, 0].reshape(width, length).astype(_BF16)
        vt = cache_ref[b, 1].reshape(width, length).astype(_BF16)
        s_c = jnp.dot(q_bd, kt, preferred_element_type=_F32)
        s_n = lax.dot_general(q_bd, kn_ref[b].astype(_BF16), _NT, preferred_element_type=_F32)
        s_c = jnp.where(valid_c, s_c, _NEG_INF)
        s_n = jnp.where(valid_n, s_n, _NEG_INF)
        m = jnp.maximum(jnp.max(s_c, axis=-1, keepdims=True), jnp.max(s_n, axis=-1, keepdims=True))
        e_c = jnp.exp(s_c - m)
        e_n = jnp.exp(s_n - m)
        den = jnp.sum(e_c, axis=-1, keepdims=True) + jnp.sum(e_n, axis=-1, keepdims=True)
        o_all = lax.dot_general((e_c / den).astype(_BF16), vt, _NT, preferred_element_type=_F32)
        o_all = o_all + jnp.dot((e_n / den).astype(_BF16), vn_ref[b].astype(_BF16),
                                preferred_element_type=_F32)
        o_ref[b] = own_head(per_head(lambda h: rows(o_all, h)))
        lse = m + jnp.log(den)
        lse_ref[b] = own_head(per_head(lambda h: jnp.broadcast_to(rows(lse, h), (STEP_NEW_ROWS, width))))


def _step_b(q_s, kv_s, cache, dil):
    db, length = cache.shape[:2]
    gw = B_GROUP_WIDTH
    cache_t = cache.transpose(0, 2, 3, 4, 1)
    head_bytes = 2 * HEAD_DIM * length * 4
    hb = min(B_HEADS, max(1, STEP_CACHE_BYTES // head_bytes))
    bb = min(db, max(1, STEP_CACHE_BYTES // (head_bytes * hb)))
    pad = lambda x: jnp.pad(x.reshape(db, DEC_SEQ, gw), [(0, 0), (0, STEP_NEW_ROWS - DEC_SEQ), (0, 0)])
    tok_spec = pl.BlockSpec((bb, STEP_NEW_ROWS, hb * HEAD_DIM), lambda b, j: (b, 0, j))
    tok_sds = jax.ShapeDtypeStruct((db, STEP_NEW_ROWS, gw), _F32)
    o, lse = pl.pallas_call(
        functools.partial(_step_b_kernel, dil=dil),
        grid=(db // bb, B_HEADS // hb),
        in_specs=[tok_spec, tok_spec, tok_spec,
                  pl.BlockSpec((bb, 2, hb, HEAD_DIM, length), lambda b, j: (b, 0, j, 0, 0))],
        out_specs=[tok_spec, tok_spec],
        out_shape=[tok_sds, tok_sds],
        compiler_params=_cparams("parallel", "parallel"),
        name="step_b_l%d" % length,
    )(pad(q_s.astype(_F32)), pad(kv_s[:, :gw]), pad(kv_s[:, gw:]), cache_t)
    back = lambda x: x[:, :DEC_SEQ].reshape(db * DEC_SEQ, gw)
    return back(o), back(lse)


def _step_a(qkv_s, cache, sinks, db):
    nq = A_HEADS * HEAD_DIM
    nk = A_KV_HEADS * HEAD_DIM
    q = qkv_s[:, :nq].reshape(db, DEC_SEQ, A_KV_HEADS, A_GROUP, HEAD_DIM)
    q = q.transpose(0, 2, 1, 3, 4).reshape(db, A_KV_HEADS, DEC_SEQ * A_GROUP, HEAD_DIM)
    sink_rows = jnp.tile(sinks.reshape(A_KV_HEADS, 1, A_GROUP), (1, DEC_SEQ, 1))
    sink_rows = sink_rows.reshape(A_KV_HEADS, DEC_SEQ * A_GROUP, 1)
    bb = STEP_A_SAMPLES
    q_spec = pl.BlockSpec((bb,) + q.shape[1:], lambda i: (i, 0, 0, 0))
    n_spec = pl.BlockSpec((bb, A_KV_HEADS, STEP_NEW_ROWS, HEAD_DIM), lambda i: (i, 0, 0, 0))
    o = pl.pallas_call(
        _step_a_kernel,
        grid=(db // bb,),
        in_specs=[q_spec, n_spec, n_spec,
                  pl.BlockSpec((bb, 2, A_KV_HEADS, HEAD_DIM, A_WINDOW), lambda i: (i, 0, 0, 0, 0)),
                  pl.BlockSpec(sink_rows.shape, lambda i: (0, 0, 0))],
        out_specs=q_spec,
        out_shape=jax.ShapeDtypeStruct(q.shape, _F32),
        compiler_params=_cparams("parallel"),
        name="step_a",
    )(q, _slots(qkv_s[:, nq:nq + nk], db, A_KV_HEADS), _slots(qkv_s[:, nq + nk:], db, A_KV_HEADS),
      cache.transpose(0, 2, 3, 4, 1), sink_rows)
    o = o.reshape(db, A_KV_HEADS, DEC_SEQ, A_GROUP, HEAD_DIM).transpose(0, 2, 1, 3, 4)
    return o.reshape(db * DEC_SEQ, nq)


def _layer_norm_rows(z, g, b):
    mu = jnp.mean(z, axis=-1, keepdims=True)
    zc = z - mu
    var = jnp.mean(zc * zc, axis=-1, keepdims=True)
    return zc * lax.rsqrt(var + LN_EPS) * g + b


def _proj_ln_kernel(*refs, dils, has_bias):
    n_merge = len(dils)
    n_in = 2 * n_merge if n_merge else 1
    w_ref = refs[n_in]
    pos = n_in + 1
    if has_bias:
        bias_ref = refs[pos]
        pos += 1
    x_ref, g_ref, b_ref, out_ref = refs[pos:pos + 4]
    scratch = refs[pos + 4:]
    tm = x_ref.shape[0]

    if n_merge:
        o_scr, order_scr = scratch[0], scratch[1:]

        def token_order(idx, ref, d, lanes):
            if d == 1:
                return ref[:, lanes]
            scr = order_scr[idx]
            for r in range(d):
                scr[pl.ds(r, tm // d, stride=d), :] = ref[r, :, lanes]
            return scr[...]

        reordered = [i for i, d in enumerate(dils) if d > 1]
        for g in range(w_ref.shape[0] // LANES):
            lanes = slice(g * LANES, (g + 1) * LANES)
            os = [token_order(2 * reordered.index(i) if d > 1 else 0, refs[i], d, lanes)
                  for i, d in enumerate(dils)]
            ls = [token_order(2 * reordered.index(i) + 1 if d > 1 else 0, refs[n_merge + i], d, lanes)
                  for i, d in enumerate(dils)]
            m = functools.reduce(jnp.maximum, ls)
            es = [jnp.exp(l - m) for l in ls]
            den = functools.reduce(jnp.add, es)
            o_scr[:, lanes] = functools.reduce(
                jnp.add, [(e / den) * o for e, o in zip(es, os)]).astype(_BF16)
        o = o_scr[...]
    else:
        o = refs[0][...].astype(_BF16)
    y = jnp.dot(o, w_ref[...], preferred_element_type=_F32)
    if has_bias:
        y = y + bias_ref[...]
    out_ref[...] = _layer_norm_rows(DN_ALPHA * x_ref[...] + y, g_ref[...], b_ref[...])


def _proj_ln(o_list, lse_list, dils, w_bf16, bias, x, ln_g, ln_b, tm=256):
    n = x.shape[0]
    k = w_bf16.shape[0]
    row = lambda i: (i, 0)
    fixed = lambda i: (0, 0)
    tiles = SEQ // tm

    def in_spec(d):
        if d == 1:
            return pl.BlockSpec((tm, k), row)
        return pl.BlockSpec((d, tm // d, k), lambda i: (i // tiles, i % tiles, 0))

    args = list(o_list) + list(lse_list) + [w_bf16]
    specs = [in_spec(d) for d in (dils or (1,))] * (2 if lse_list else 1)
    specs.append(pl.BlockSpec((k, D_MODEL), fixed))
    if bias is not None:
        args.append(bias)
        specs.append(pl.BlockSpec((1, D_MODEL), fixed))
    args += [x, ln_g, ln_b]
    specs += [pl.BlockSpec((tm, D_MODEL), row), pl.BlockSpec((1, D_MODEL), fixed),
              pl.BlockSpec((1, D_MODEL), fixed)]
    scratch = []
    if lse_list:
        scratch = [pltpu.VMEM((tm, k), _BF16)]
        scratch += [pltpu.VMEM((tm, LANES), _F32)] * (2 * sum(d > 1 for d in dils))
    return pl.pallas_call(
        functools.partial(_proj_ln_kernel, dils=tuple(dils), has_bias=bias is not None),
        grid=(n // tm,),
        in_specs=specs,
        out_specs=pl.BlockSpec((tm, D_MODEL), row),
        out_shape=jax.ShapeDtypeStruct((n, D_MODEL), _F32),
        scratch_shapes=scratch,
        compiler_params=_cparams("parallel"),
        name="proj_ln",
    )(*args)


N_KEY_BLOCKS = PEER_HEADS * 2
HALF_KEY_DIM = PEER_KEY_DIM // 2


def _peer_score_kernel(x_ref, wq_ref, keys_ref, sc_ref):
    q = jnp.dot(x_ref[...].astype(_BF16), wq_ref[...], preferred_element_type=_F32).astype(_BF16)
    for blk in range(N_KEY_BLOCKS):
        sc_ref[blk * PEER_N_KEYS:(blk + 1) * PEER_N_KEYS, :] = lax.dot_general(
            keys_ref[blk], q[:, blk * HALF_KEY_DIM:(blk + 1) * HALF_KEY_DIM], _NT,
            preferred_element_type=_F32)


def _peer_scores(x, wq_bf16, keys_bf16, layer):
    n = x.shape[0]
    tm = TOKEN_TILE
    return pl.pallas_call(
        _peer_score_kernel,
        grid=(n // tm,),
        in_specs=[
            pl.BlockSpec((tm, D_MODEL), lambda i: (i, 0)),
            pl.BlockSpec((None, D_MODEL, PEER_HEADS * PEER_KEY_DIM), lambda i: (layer, 0, 0)),
            pl.BlockSpec((None, N_KEY_BLOCKS, PEER_N_KEYS, HALF_KEY_DIM), lambda i: (layer, 0, 0, 0)),
        ],
        out_specs=pl.BlockSpec((N_KEY_BLOCKS * PEER_N_KEYS, tm), lambda i: (0, i)),
        out_shape=jax.ShapeDtypeStruct((N_KEY_BLOCKS * PEER_N_KEYS, n), _F32),
        compiler_params=_cparams("parallel"),
        name="peer_scores",
    )(x, wq_bf16, keys_bf16)


TOPK_TOKENS = 128


def _top1_rows(vals, ids, big):
    m = jnp.max(vals, axis=0, keepdims=True)
    sel = jnp.min(jnp.where(vals == m, ids, big), axis=0, keepdims=True)
    return m, sel, ids == sel


def _sort16_network():
    pairs = []

    def merge(lo, n, r):
        step = r * 2
        if step < n:
            merge(lo, n, step)
            merge(lo + r, n, step)
            pairs.extend((i, i + r) for i in range(lo + r, lo + n - r, step))
        else:
            pairs.append((lo, lo + r))

    def sort(lo, n):
        if n > 1:
            sort(lo, n // 2)
            sort(lo + n // 2, n // 2)
            merge(lo, n, 1)

    sort(0, PEER_TOPK)
    return pairs


def _peer_topk_kernel(sc_ref, e_ref, gate_ref, top_s, top_i, sel_e, sel_g):
    tt = sc_ref.shape[1]
    sub = lax.broadcasted_iota(jnp.int32, (SUBLANES, tt), 0).astype(_F32)
    neg_inf = jnp.full((SUBLANES, tt), _NEG_INF, _F32)

    def bcast(row):
        return jnp.broadcast_to(row, (SUBLANES, tt))

    def sub_max(x):
        return jnp.max(x, axis=0, keepdims=True)

    def head_lists(h):
        s0, s1 = top_s[2 * h], top_s[2 * h + 1]
        i0, i1 = top_i[2 * h] * PEER_N_KEYS, top_i[2 * h + 1]
        return s0, s1, i0, i1

    def write_gates(h, best):
        ex = [jnp.exp(b - best[0]) for b in best]
        den = functools.reduce(jnp.add, ex)
        for k in range(PEER_TOPK):
            sel_g[h, k:k + 1, :] = ex[k] / den

    network = _sort16_network()

    def pop_heads(vals, ids, hit, depth):
        for p in range(depth):
            last = p + 1 == len(vals)
            vals[p] = jnp.where(hit, neg_inf if last else vals[p + 1], vals[p])
            if not last:
                ids[p] = jnp.where(hit, ids[p + 1], ids[p])

    def fast_stage1(blk, tie):
        start = pl.multiple_of(blk * PEER_N_KEYS, PEER_N_KEYS)
        vals = [sc_ref[pl.ds(start + SUBLANES * k, SUBLANES), :] for k in range(PEER_TOPK)]
        ids = [sub + float(SUBLANES * k) for k in range(PEER_TOPK)]
        for i, j in network:
            keep = vals[i] >= vals[j]
            vals[i], vals[j] = jnp.maximum(vals[i], vals[j]), jnp.minimum(vals[i], vals[j])
            ids[i], ids[j] = jnp.where(keep, ids[i], ids[j]), jnp.where(keep, ids[j], ids[i])
        prev = None
        for k in range(PEER_TOPK + 1):
            m = sub_max(vals[0])
            hit = vals[0] == m
            n_hit = jnp.sum(jnp.where(hit, 1.0, 0.0), axis=0, keepdims=True)
            tie = jnp.maximum(tie, jnp.where(n_hit > 1.5, 1.0, 0.0))
            if prev is not None:
                tie = jnp.maximum(tie, jnp.where(m == prev, 1.0, 0.0))
            prev = m
            if k == PEER_TOPK:
                break
            top_s[blk, k:k + 1, :] = m
            top_i[blk, k:k + 1, :] = sub_max(jnp.where(hit, ids[0], -1.0))
            pop_heads(vals, ids, hit, PEER_TOPK - k)
        return tie

    def fast_stage2(h, tie):
        s0, s1, i0, i1 = head_lists(h)
        lo_s, hi_s, lo_i, hi_i = s1[:SUBLANES], s1[SUBLANES:], i1[:SUBLANES], i1[SUBLANES:]
        chain = [bcast(s0[a:a + 1]) + lo_s for a in range(SUBLANES)]
        chain_e = [bcast(i0[a:a + 1]) + lo_i for a in range(SUBLANES)]
        single = [bcast(s0[0:1]) + hi_s, s0[SUBLANES:] + bcast(s1[0:1])]
        single_e = [bcast(i0[0:1]) + hi_i, i0[SUBLANES:] + bcast(i1[0:1])]
        best, prev = [], None
        for k in range(PEER_TOPK + 1):
            m = sub_max(jnp.maximum(chain[0], jnp.maximum(single[0], single[1])))
            hits = [chain[0] == m, single[0] == m, single[1] == m]
            n_hit = jnp.sum(functools.reduce(jnp.add, [jnp.where(x, 1.0, 0.0) for x in hits]),
                            axis=0, keepdims=True)
            tie = jnp.maximum(tie, jnp.where(n_hit > 1.5, 1.0, 0.0))
            if prev is not None:
                tie = jnp.maximum(tie, jnp.where(m == prev, 1.0, 0.0))
            prev = m
            if k == PEER_TOPK:
                break
            best.append(m)
            picked = [jnp.where(x, e, -1.0) for x, e in zip(hits, [chain_e[0]] + single_e)]
            sel_e[h, k:k + 1, :] = sub_max(functools.reduce(jnp.maximum, picked))
            pop_heads(chain, chain_e, hits[0], min(SUBLANES, PEER_TOPK - k))
            single = [jnp.where(x, neg_inf, v) for x, v in zip(hits[1:], single)]
        write_gates(h, best)
        return tie

    tie = lax.fori_loop(0, N_KEY_BLOCKS, fast_stage1, jnp.zeros((1, tt), _F32), unroll=2)
    tie = lax.fori_loop(0, PEER_HEADS, fast_stage2, tie, unroll=2)

    @pl.when(jnp.max(tie) > 0.5)
    def _():
        key_id = lax.broadcasted_iota(jnp.int32, (PEER_N_KEYS, tt), 0).astype(_F32)

        def stage1(blk, carry):
            s = sc_ref[pl.ds(pl.multiple_of(blk * PEER_N_KEYS, PEER_N_KEYS), PEER_N_KEYS), :]
            for k in range(PEER_TOPK):
                m, sel, onehot = _top1_rows(s, key_id, float(PEER_N_KEYS))
                top_s[blk, k:k + 1, :] = m
                top_i[blk, k:k + 1, :] = sel
                s = jnp.where(onehot, _NEG_INF, s)
            return carry

        lax.fori_loop(0, N_KEY_BLOCKS, stage1, 0, unroll=2)

        flat_id = jnp.concatenate(
            [sub, sub + SUBLANES] + [sub + a * PEER_TOPK for a in range(1, SUBLANES)]
            + [(sub + SUBLANES) * PEER_TOPK], axis=0)

        def stage2(h, carry):
            s0, s1, i0, i1 = head_lists(h)
            lo_s, hi_s, lo_i, hi_i = s1[:SUBLANES], s1[SUBLANES:], i1[:SUBLANES], i1[SUBLANES:]
            cand = jnp.concatenate(
                [bcast(s0[0:1]) + lo_s, bcast(s0[0:1]) + hi_s]
                + [bcast(s0[a:a + 1]) + lo_s for a in range(1, SUBLANES)]
                + [s0[SUBLANES:] + bcast(s1[0:1])], axis=0)
            cand_e = jnp.concatenate(
                [bcast(i0[0:1]) + lo_i, bcast(i0[0:1]) + hi_i]
                + [bcast(i0[a:a + 1]) + lo_i for a in range(1, SUBLANES)]
                + [i0[SUBLANES:] + bcast(i1[0:1])], axis=0)
            best = []
            for k in range(PEER_TOPK):
                m, _, onehot = _top1_rows(cand, flat_id, float(PEER_TOPK * PEER_TOPK))
                best.append(m)
                sel_e[h, k:k + 1, :] = sub_max(jnp.where(onehot, cand_e, -1.0))
                cand = jnp.where(onehot, _NEG_INF, cand)
            write_gates(h, best)
            return carry

        lax.fori_loop(0, PEER_HEADS, stage2, 0, unroll=2)

    n_sel = PEER_HEADS * PEER_TOPK
    e_ref[...] = sel_e[...].reshape(n_sel, tt).T.astype(jnp.int32)
    gate_ref[...] = sel_g[...].reshape(n_sel, tt).T


def _peer_topk(sc_t):
    n = sc_t.shape[1]
    tt = TOPK_TOKENS
    n_sel = PEER_HEADS * PEER_TOPK
    return pl.pallas_call(
        _peer_topk_kernel,
        grid=(n // tt,),
        in_specs=[pl.BlockSpec((N_KEY_BLOCKS * PEER_N_KEYS, tt), lambda i: (0, i))],
        out_specs=[pl.BlockSpec((tt, n_sel), lambda i: (i, 0)),
                   pl.BlockSpec((tt, n_sel), lambda i: (i, 0))],
        out_shape=[jax.ShapeDtypeStruct((n, n_sel), jnp.int32),
                   jax.ShapeDtypeStruct((n, n_sel), _F32)],
        scratch_shapes=[pltpu.VMEM((N_KEY_BLOCKS, PEER_TOPK, tt), _F32),
                        pltpu.VMEM((N_KEY_BLOCKS, PEER_TOPK, tt), _F32),
                        pltpu.VMEM((PEER_HEADS, PEER_TOPK, tt), _F32),
                        pltpu.VMEM((PEER_HEADS, PEER_TOPK, tt), _F32)],
        compiler_params=_cparams("parallel"),
        name="peer_topk",
    )(sc_t)


GATE_TOKENS = 64
GATE_GROUP = 16


def _peer_gate_kernel(e_ref, gate_ref, g_ref, scr):
    tb, _, n_sel = e_ref.shape
    row_id = lax.broadcasted_iota(jnp.int32, (PEER_N_KEYS, n_sel), 0)

    def token(n):
        e = e_ref[n]
        gate = gate_ref[n]
        hi = gate.astype(_BF16).astype(_F32)
        lo = gate - hi
        is_i = row_id == (e >> 7)
        lhs = jnp.concatenate([jnp.where(is_i, hi, 0.0), jnp.where(is_i, lo, 0.0)], axis=1)
        one = jnp.where(row_id == (e & (PEER_N_KEYS - 1)), 1.0, 0.0)
        rhs = jnp.concatenate([one, one], axis=1)
        res = lax.dot_general(lhs.astype(_BF16), rhs.astype(_BF16), _NT, preferred_element_type=_F32)
        for ib in range(PEER_N_KEYS // SUBLANES):
            start = (ib * tb + n) * SUBLANES
            scr[start:start + SUBLANES, :] = res[ib * SUBLANES:(ib + 1) * SUBLANES]

    for grp in range(tb // GATE_GROUP):
        first = grp * GATE_GROUP
        for n in range(first, first + GATE_GROUP):
            token(n)
        for i in range(PEER_N_KEYS):
            start = ((i // SUBLANES) * tb + first) * SUBLANES + i % SUBLANES
            g_ref[first:first + GATE_GROUP, i * PEER_N_KEYS:(i + 1) * PEER_N_KEYS] = (
                scr[pl.ds(start, GATE_GROUP, stride=SUBLANES), :])


def _peer_gates(expert, gate):
    n, n_sel = expert.shape
    tb = GATE_TOKENS
    expert = expert.reshape(n, 1, n_sel)
    gate = gate.reshape(n, 1, n_sel)
    return pl.pallas_call(
        _peer_gate_kernel,
        grid=(n // tb,),
        in_specs=[pl.BlockSpec((tb, 1, n_sel), lambda i: (i, 0, 0)),
                  pl.BlockSpec((tb, 1, n_sel), lambda i: (i, 0, 0))],
        out_specs=pl.BlockSpec((tb, PEER_N_EXPERTS), lambda i: (i, 0)),
        out_shape=jax.ShapeDtypeStruct((n, PEER_N_EXPERTS), _F32),
        scratch_shapes=[pltpu.VMEM((tb * PEER_N_KEYS, PEER_N_KEYS), _F32)],
        compiler_params=_cparams("parallel"),
        name="peer_gates",
    )(expert, gate)


EXPERT_CHUNK = 1024
_SQRT_HALF = math.sqrt(0.5)


def _peer_main_kernel(x_ref, ut_ref, g_ref, v_ref, lng_ref, lnb_ref, o_ref, xb_scr, acc_scr):
    c = pl.program_id(1)

    @pl.when(c == 0)
    def _():
        xb_scr[...] = x_ref[...].astype(_BF16)
        acc_scr[...] = jnp.zeros_like(acc_scr)

    h_parts = []
    for k in range(ut_ref.shape[1] // MXU_COLS):
        cols = slice(k * MXU_COLS, (k + 1) * MXU_COLS)
        a = jnp.dot(xb_scr[...], ut_ref[:, cols], preferred_element_type=_F32)
        gelu = 0.5 * a * (1.0 + lax.erf(a * _SQRT_HALF))
        h_parts.append((g_ref[:, cols] * gelu).astype(_BF16))
    h = jnp.concatenate(h_parts, axis=1)
    acc_scr[...] += jnp.dot(h, v_ref[...], preferred_element_type=_F32)

    @pl.when(c == pl.num_programs(1) - 1)
    def _():
        o_ref[...] = _layer_norm_rows(DN_ALPHA * x_ref[...] + acc_scr[...], lng_ref[...], lnb_ref[...])


def _peer_main(x, ut_bf16, gates, v_bf16, ln_g, ln_b, layer):
    n = x.shape[0]
    tm, ce = TOKEN_TILE, EXPERT_CHUNK
    return pl.pallas_call(
        _peer_main_kernel,
        grid=(n // tm, PEER_N_EXPERTS // ce),
        in_specs=[
            pl.BlockSpec((tm, D_MODEL), lambda i, c: (i, 0)),
            pl.BlockSpec((None, D_MODEL, ce), lambda i, c: (layer, 0, c)),
            pl.BlockSpec((tm, ce), lambda i, c: (i, c)),
            pl.BlockSpec((None, ce, D_MODEL), lambda i, c: (layer, c, 0)),
            pl.BlockSpec((1, D_MODEL), lambda i, c: (0, 0)),
            pl.BlockSpec((1, D_MODEL), lambda i, c: (0, 0)),
        ],
        out_specs=pl.BlockSpec((tm, D_MODEL), lambda i, c: (i, 0)),
        out_shape=jax.ShapeDtypeStruct((n, D_MODEL), _F32),
        scratch_shapes=[pltpu.VMEM((tm, D_MODEL), _BF16), pltpu.VMEM((tm, D_MODEL), _F32)],
        compiler_params=_cparams("parallel", "arbitrary"),
        name="peer_main",
    )(x, ut_bf16, gates, v_bf16, ln_g, ln_b)


def _peer_ffn_ln(x, wq_bf16, keys_bf16, ut_bf16, v_bf16, ln_g, ln_b, layer):
    sc_t = _peer_scores(x, wq_bf16, keys_bf16, layer)
    expert, gate = _peer_topk(sc_t)
    gates = _peer_gates(expert, gate)
    return _peer_main(x, ut_bf16, gates, v_bf16, ln_g, ln_b, layer)


def _rope_tables(pos):
    half = HEAD_DIM // 2
    inv = ROPE_THETA ** (-jnp.arange(half, dtype=_F32) / half)
    ang = pos.astype(_F32)[:, None] * inv[None, :]
    cos, sin = jnp.cos(ang), jnp.sin(ang)
    cos = jnp.tile(cos, (1, LANES // half))
    sin = jnp.tile(jnp.concatenate([-sin, sin], axis=1), (1, LANES // HEAD_DIM))
    return cos, sin


def kernel(x_prompt, x_sample, cache_a_kv, cache_b1_kv, cache_b2_kv, cache_b3_kv, w_qkv_a, b_qkv_a, sinks_a, w_o_a, b_o_a, w_qkv_b, w_o_b, ln_mix_g, ln_mix_b, ln_ffn_g, ln_ffn_b, peer_w_query, peer_sub_keys, peer_u, peer_v):
    n_batch, seq = x_prompt.shape[:2]
    db, t_len = x_sample.shape[:2]
    xp = x_prompt.reshape(n_batch * seq, D_MODEL)
    xs = x_sample.reshape(db * t_len, D_MODEL)
    cos_p, sin_p = _rope_tables(jnp.tile(jnp.arange(seq, dtype=jnp.int32), n_batch))
    cos_s, sin_s = _rope_tables(jnp.tile(PAST_LEN + jnp.arange(t_len, dtype=jnp.int32), db))
    scale = HEAD_DIM ** -0.5
    row = lambda v: v.reshape(1, -1)
    caches_b = (cache_b1_kv, cache_b2_kv, cache_b3_kv)
    outs_kv = {}
    wq_all = peer_w_query.astype(_BF16)
    keys_all = peer_sub_keys.reshape(DEPTH, N_KEY_BLOCKS, PEER_N_KEYS, HALF_KEY_DIM).astype(_BF16)
    ut_all = jnp.swapaxes(peer_u, 1, 2).astype(_BF16)
    v_all = peer_v.astype(_BF16)

    for layer in range(DEPTH):
        i = layer // 2
        ln_g, ln_b = row(ln_mix_g[layer]), row(ln_mix_b[layer])
        if layer % 2 == 0:
            nq = A_HEADS * HEAD_DIM
            nk = A_KV_HEADS * HEAD_DIM
            col_scale = jnp.concatenate([jnp.full((nq,), scale, _F32), jnp.ones((2 * nk,), _F32)])
            w = (w_qkv_a[i] * col_scale).astype(_BF16)
            bias = row(b_qkv_a[i] * col_scale)
            qkv_p = _qkv_rope(xp, w, bias, cos_p, sin_p, nq + nk)
            qkv_s = _qkv_rope(xs, w, bias, cos_s, sin_s, nq + nk)
            o_p = _band_attention_a(qkv_p, sinks_a[i], n_batch)
            o_s = _step_a(qkv_s, cache_a_kv[i], sinks_a[i], db)
            wo = w_o_a[i].astype(_BF16)
            xp = _proj_ln([o_p], [], (), wo, row(b_o_a[i]), xp, ln_g, ln_b)
            xs = _proj_ln([o_s], [], (), wo, row(b_o_a[i]), xs, ln_g, ln_b)
            keep = min(A_WINDOW, seq)
            kv_p = qkv_p.reshape(n_batch, seq, A_QKV)[:, seq - keep:, nq:]
            outs_kv.setdefault("a_p", []).append(kv_p.reshape(n_batch, keep, 2, A_KV_HEADS, HEAD_DIM))
            outs_kv.setdefault("a_s", []).append(qkv_s[:, nq:].reshape(db, t_len, 2, A_KV_HEADS, HEAD_DIM))
        else:
            gw = B_GROUP_WIDTH
            col_scale = jnp.concatenate([jnp.full((N_B_GROUPS * gw,), scale, _F32),
                                         jnp.ones((2 * N_B_GROUPS * gw,), _F32)])
            w = (w_qkv_b[i] * col_scale).astype(_BF16)
            o_ps, l_ps, o_ss, l_ss = [], [], [], []
            for g, (window, dil) in enumerate(B_GROUPS):
                outs = _qkv_group(xp, w, cos_p, sin_p, dil, g)
                q_p, kv_p, kvr_p = outs if dil > 1 else (outs[0], outs[1], outs[1])
                q_s, kv_s = _qkv_group(xs, w, cos_s, sin_s, 1, g)
                o, l = _band_attention_b(q_p, kvr_p, dil, n_batch)
                o_ps.append(o)
                l_ps.append(l)
                kv_s5 = kv_s.reshape(db, t_len, 2, B_HEADS, HEAD_DIM)
                o, l = _step_b(q_s, kv_s, caches_b[g][i], dil)
                o_ss.append(o)
                l_ss.append(l)
                keep = min(window, seq)
                kv_keep = kv_p.reshape(n_batch, seq, 2 * gw)[:, seq - keep:]
                outs_kv.setdefault("b%d_p" % g, []).append(
                    kv_keep.reshape(n_batch, keep, 2, B_HEADS, HEAD_DIM))
                outs_kv.setdefault("b%d_s" % g, []).append(kv_s5)
            wo = w_o_b[i].astype(_BF16)
            xp = _proj_ln(o_ps, l_ps, [dil for _, dil in B_GROUPS], wo, None, xp, ln_g, ln_b)
            xs = _proj_ln(o_ss, l_ss, [1] * N_B_GROUPS, wo, None, xs, ln_g, ln_b)

        ffn_g, ffn_b = row(ln_ffn_g[layer]), row(ln_ffn_b[layer])
        xp = _peer_ffn_ln(xp, wq_all, keys_all, ut_all, v_all, ffn_g, ffn_b, layer)
        xs = _peer_ffn_ln(xs, wq_all, keys_all, ut_all, v_all, ffn_g, ffn_b, layer)

    return (xp.reshape(n_batch, seq, D_MODEL), xs.reshape(db, t_len, D_MODEL),
            jnp.stack(outs_kv["a_p"]), jnp.stack(outs_kv["a_s"]),
            jnp.stack(outs_kv["b0_p"]), jnp.stack(outs_kv["b0_s"]),
            jnp.stack(outs_kv["b1_p"]), jnp.stack(outs_kv["b1_s"]),
            jnp.stack(outs_kv["b2_p"]), jnp.stack(outs_kv["b2_s"]))
```

```python
import functools
import math

import jax
import jax.numpy as jnp
from jax import lax
from jax.experimental import pallas as pl
from jax.experimental.pallas import tpu as pltpu

D_MODEL = 2048
SEQ = 2048
DEPTH = 2
DEC_SEQ = 4
PAST_LEN = 8192
HEAD_DIM = 64
ROPE_THETA = 10000.0
BLOCK = 128
A_HEADS = D_MODEL // HEAD_DIM
A_KV_HEADS = A_HEADS // 8
A_GROUP = A_HEADS // A_KV_HEADS
A_WINDOW = 128
A_QKV = (A_HEADS + 2 * A_KV_HEADS) * HEAD_DIM
B_GROUPS = ((128, 1), (512, 4), (2048, 16))
N_B_GROUPS = len(B_GROUPS)
B_HEADS = D_MODEL // (2 * HEAD_DIM)
B_GROUP_WIDTH = B_HEADS * HEAD_DIM
B_KEYS = 128
PEER_HEADS = 8
PEER_N_KEYS = 128
PEER_N_EXPERTS = PEER_N_KEYS * PEER_N_KEYS
PEER_TOPK = 16
PEER_KEY_DIM = 256
DN_ALPHA = (2.0 * DEPTH) ** 0.25
LN_EPS = 1e-5

LANES = 128
SUBLANES = 8
TOKEN_TILE = 512
VMEM_LIMIT = 56 * 1024 * 1024

_F32 = jnp.float32
_BF16 = jnp.bfloat16
_NEG_INF = float("-inf")
_NT = (((1,), (1,)), ((), ()))


def _cparams(*sem):
    return pltpu.CompilerParams(dimension_semantics=sem, vmem_limit_bytes=VMEM_LIMIT)


MXU_COLS = 256


def _rope(blk, cos, sin):
    lane = lax.broadcasted_iota(jnp.int32, blk.shape, 1)
    first_half = (lane % HEAD_DIM) < (HEAD_DIM // 2)
    partner = jnp.where(first_half, pltpu.roll(blk, LANES - HEAD_DIM // 2, 1),
                        pltpu.roll(blk, HEAD_DIM // 2, 1))
    return blk * cos + partner * sin


def _project_chunks(xb, w_ref, emit, bias_ref=None):
    for c in range(w_ref.shape[1] // MXU_COLS):
        cols = slice(c * MXU_COLS, (c + 1) * MXU_COLS)
        acc = jnp.dot(xb, w_ref[:, cols], preferred_element_type=_F32)
        if bias_ref is not None:
            acc = acc + bias_ref[:, cols]
        for h in range(MXU_COLS // LANES):
            emit(c * (MXU_COLS // LANES) + h, acc[:, h * LANES:(h + 1) * LANES])


def _qkv_rope_kernel(x_ref, w_ref, b_ref, cos_ref, sin_ref, o_ref, *, n_rope_groups):
    def emit(g, blk):
        if g < n_rope_groups:
            blk = _rope(blk, cos_ref[...], sin_ref[...])
        o_ref[:, g * LANES:(g + 1) * LANES] = blk

    _project_chunks(x_ref[...].astype(_BF16), w_ref, emit, b_ref)


def _qkv_rope(x, w_bf16, bias, cos, sin, n_rope_cols):
    n, k = x.shape
    ncols = w_bf16.shape[1]
    tm = TOKEN_TILE
    return pl.pallas_call(
        functools.partial(_qkv_rope_kernel, n_rope_groups=n_rope_cols // LANES),
        grid=(n // tm,),
        in_specs=[
            pl.BlockSpec((tm, k), lambda i: (i, 0)),
            pl.BlockSpec((k, ncols), lambda i: (0, 0)),
            pl.BlockSpec((1, ncols), lambda i: (0, 0)),
            pl.BlockSpec((tm, LANES), lambda i: (i, 0)),
            pl.BlockSpec((tm, LANES), lambda i: (i, 0)),
        ],
        out_specs=pl.BlockSpec((tm, ncols), lambda i: (i, 0)),
        out_shape=jax.ShapeDtypeStruct((n, ncols), _F32),
        compiler_params=_cparams("parallel"),
        name="qkv_rope",
    )(x, w_bf16, bias, cos, sin)


def _qkv_group_kernel(x_ref, wq_ref, wk_ref, wv_ref, cos_ref, sin_ref, *refs, dil):
    if dil == 1:
        q_ref, kv_ref = refs
    else:
        q_ref, kv_ref, kvr_ref, scr = refs
    xb = x_ref[...].astype(_BF16)
    per_residue = x_ref.shape[0] // dil
    width = wq_ref.shape[1]

    def emitter(token_ref, residue_ref, rotate, col0, slot):
        def emit(g, blk):
            lanes = slice(col0 + g * LANES, col0 + (g + 1) * LANES)
            if rotate:
                blk = _rope(blk, cos_ref[...], sin_ref[...])
            if token_ref is not None:
                token_ref[:, lanes] = blk.astype(token_ref.dtype)
            if residue_ref is not None:
                scr[slot, g] = blk
                for r in range(dil):
                    residue_ref[r, :, lanes] = (
                        scr[slot, g, pl.ds(r, per_residue, stride=dil), :].astype(residue_ref.dtype))
        return emit

    kvr = kvr_ref if dil > 1 else None
    _project_chunks(xb, wq_ref, emitter(q_ref if dil == 1 else None, q_ref if dil > 1 else None,
                                        True, 0, 0))
    _project_chunks(xb, wk_ref, emitter(kv_ref, kvr, True, 0, 1))
    _project_chunks(xb, wv_ref, emitter(kv_ref, kvr, False, width, 2))


def _qkv_group(x, w_bf16, cos, sin, dil, group):
    n, k = x.shape
    tm, tn = TOKEN_TILE, B_GROUP_WIDTH
    kv_spec = pl.BlockSpec((tm, 2 * tn), lambda i: (i, 0))
    kv_sds = jax.ShapeDtypeStruct((n, 2 * tn), _F32)
    if dil == 1:
        out_specs = [pl.BlockSpec((tm, tn), lambda i: (i, 0)), kv_spec]
        out_shape = [jax.ShapeDtypeStruct((n, tn), _BF16), kv_sds]
        scratch = []
    else:
        tiles = SEQ // tm
        res_rows = (n // SEQ * dil, SEQ // dil)
        res_spec = lambda w: pl.BlockSpec((dil, tm // dil, w), lambda i: (i // tiles, i % tiles, 0))
        out_specs = [res_spec(tn), kv_spec, res_spec(2 * tn)]
        out_shape = [jax.ShapeDtypeStruct(res_rows + (tn,), _BF16), kv_sds,
                     jax.ShapeDtypeStruct(res_rows + (2 * tn,), _BF16)]
        scratch = [pltpu.VMEM((3, tn // LANES, tm, LANES), _F32)]
    w_spec = lambda part: pl.BlockSpec((k, tn), lambda i: (0, part * N_B_GROUPS + group),
                                       pipeline_mode=pl.Buffered(1))
    return pl.pallas_call(
        functools.partial(_qkv_group_kernel, dil=dil),
        grid=(n // tm,),
        in_specs=[
            pl.BlockSpec((tm, k), lambda i: (i, 0)),
            w_spec(0), w_spec(1), w_spec(2),
            pl.BlockSpec((tm, LANES), lambda i: (i, 0)),
            pl.BlockSpec((tm, LANES), lambda i: (i, 0)),
        ],
        out_specs=out_specs,
        out_shape=out_shape,
        scratch_shapes=scratch,
        compiler_params=_cparams("parallel"),
        name="qkv_group_d%d" % dil,
    )(x, w_bf16, w_bf16, w_bf16, cos, sin)


def _band_attn_kernel(sink_ref, q_ref, kc_ref, vc_ref, kp_ref, vp_ref, *out_refs, kv_share,
                      has_sink, with_lse):
    o_ref = out_refs[0]
    lse_ref = out_refs[1] if with_lse else None
    j = pl.program_id(2)
    qi = lax.broadcasted_iota(jnp.int32, (BLOCK, 2 * BLOCK), 0)
    kj = lax.broadcasted_iota(jnp.int32, (BLOCK, 2 * BLOCK), 1)
    first_key = jnp.maximum(qi, jnp.where(j > 0, 0, BLOCK))
    valid = (kj >= first_key) & (kj <= qi + BLOCK)
    lane = lax.broadcasted_iota(jnp.int32, (BLOCK, LANES), 1)
    low = lane < HEAD_DIM

    def keys(cur_ref, prev_ref, kv_head):
        grp = kv_head // 2 if kv_share else kv_head
        lanes = slice(grp * LANES, (grp + 1) * LANES)
        parts = [prev_ref[:, lanes], cur_ref[:, lanes]]
        if kv_share:
            keep = low if kv_head % 2 == 0 else ~low
            parts = [jnp.where(keep, x, pltpu.roll(x, HEAD_DIM, 1)) for x in parts]
        return jnp.concatenate(parts, axis=0).astype(_BF16)

    for p in range(q_ref.shape[1] // LANES):
        if kv_share == 0 or p % kv_share == 0:
            kv_head = p // kv_share if kv_share else p
            kcat = keys(kc_ref, kp_ref, kv_head)
            vcat = keys(vc_ref, vp_ref, kv_head)
        q2 = q_ref[:, p * LANES:(p + 1) * LANES]
        outs, lses = [], []
        for h in range(2):
            qm = jnp.where(low if h == 0 else ~low, q2, 0.0).astype(_BF16)
            s = lax.dot_general(qm, kcat, _NT, preferred_element_type=_F32)
            s = jnp.where(valid, s, _NEG_INF)
            m = jnp.max(s, axis=-1, keepdims=True)
            if has_sink:
                sink = sink_ref[2 * p + h]
                m = jnp.maximum(m, sink)
            e = jnp.exp(s - m)
            den = jnp.sum(e, axis=-1, keepdims=True)
            if has_sink:
                den = den + jnp.exp(sink - m)
            pn = (e / den).astype(_BF16)
            outs.append(jnp.dot(pn, vcat, preferred_element_type=_F32))
            lses.append(m + jnp.log(den))
        o_ref[:, p * LANES:(p + 1) * LANES] = jnp.where(low, outs[0], outs[1])
        if with_lse:
            lse_ref[:, p * LANES:(p + 1) * LANES] = jnp.where(low, lses[0], lses[1])


def _band_attention(sinks, q, kv, *, q_col, k_col, v_col, q_width, kv_width, n_seq, nb, kv_share,
                    with_lse, name):
    cur = lambda b, r, j, s: b * nb + j
    prev = lambda b, r, j, s: b * nb + jnp.maximum(j - 1, 0)
    blk = lambda rows, width, col: pl.BlockSpec((BLOCK, width), lambda *a: (rows(*a), col))
    in_specs = [blk(cur, q_width, q_col), blk(cur, kv_width, k_col), blk(cur, kv_width, v_col),
                blk(prev, kv_width, k_col), blk(prev, kv_width, v_col)]
    args = [q, kv, kv, kv, kv]
    o_spec = blk(cur, q_width, 0)
    o_sds = jax.ShapeDtypeStruct((n_seq * nb * BLOCK, q_width), _F32)
    return pl.pallas_call(
        functools.partial(_band_attn_kernel, kv_share=kv_share,
                          has_sink=sinks is not None, with_lse=with_lse),
        grid_spec=pltpu.PrefetchScalarGridSpec(
            num_scalar_prefetch=1,
            grid=(n_seq, 1, nb),
            in_specs=in_specs,
            out_specs=[o_spec, o_spec] if with_lse else [o_spec],
        ),
        out_shape=[o_sds, o_sds] if with_lse else [o_sds],
        compiler_params=_cparams("parallel", "arbitrary", "arbitrary"),
        name=name,
    )(jnp.zeros((1,), _F32) if sinks is None else sinks, *args)


def _band_attention_a(qkv, sinks, n_batch):
    nq = A_HEADS * HEAD_DIM
    nk = A_KV_HEADS * HEAD_DIM
    (o,) = _band_attention(sinks, qkv, qkv, q_col=0, k_col=nq // nk, v_col=nq // nk + 1,
                           q_width=nq, kv_width=nk, n_seq=n_batch, nb=SEQ // BLOCK,
                           kv_share=A_GROUP // 2, with_lse=False, name="band_attn_a")
    return o


def _band_attention_b(q, kv, dil, n_batch):
    gw = B_GROUP_WIDTH
    o, lse = _band_attention(None, q.reshape(-1, gw), kv.reshape(-1, 2 * gw), q_col=0, k_col=0, v_col=1,
                             q_width=gw, kv_width=gw, n_seq=n_batch * dil, nb=SEQ // dil // BLOCK,
                             kv_share=0, with_lse=True, name="band_attn_d%d" % dil)
    return o.reshape(q.shape), lse.reshape(q.shape)


STEP_NEW_ROWS = 8
STEP_CACHE_BYTES = 8 * 1024 * 1024


STEP_A_SAMPLES = 8


def _step_a_kernel(q_ref, kn_ref, vn_ref, cache_ref, sink_ref, o_ref):
    length = cache_ref.shape[-1]
    kv_width = A_KV_HEADS * HEAD_DIM
    n_rows = A_HEADS * STEP_NEW_ROWS
    lane = lax.broadcasted_iota(jnp.int32, (STEP_NEW_ROWS, LANES), 1)
    low = lane < HEAD_DIM
    zeros = jnp.zeros((STEP_NEW_ROWS, LANES), _F32)

    def step_of(shape):
        return (lax.broadcasted_iota(jnp.int32, shape, 0) % STEP_NEW_ROWS) % DEC_SEQ

    valid_c = lax.broadcasted_iota(jnp.int32, (n_rows, length), 1) >= step_of((n_rows, length))
    valid_n = (lax.broadcasted_iota(jnp.int32, (n_rows, STEP_NEW_ROWS), 1)
               <= step_of((n_rows, STEP_NEW_ROWS)))
    sink = sink_ref[...]

    def to_half(x, src_half, dst_half):
        return x if src_half == dst_half else pltpu.roll(x, HEAD_DIM, 1)

    for b in range(q_ref.shape[0]):
        q_tok = q_ref[b]
        blocks = []
        for h in range(A_HEADS):
            kvh = h // A_GROUP
            piece = to_half(q_tok[:, (h // 2) * LANES:(h // 2 + 1) * LANES], h % 2, kvh % 2)
            piece = jnp.where(low if kvh % 2 == 0 else ~low, piece, 0.0)
            groups = [zeros] * (kv_width // LANES)
            groups[kvh // 2] = piece
            blocks.append(jnp.concatenate(groups, axis=1))
        q_bd = jnp.concatenate(blocks, axis=0).astype(_BF16)
        kt = cache_ref[b, 0].reshape(kv_width, length).astype(_BF16)
        vt = cache_ref[b, 1].reshape(kv_width, length).astype(_BF16)
        s_c = jnp.dot(q_bd, kt, preferred_element_type=_F32)
        s_n = lax.dot_general(q_bd, kn_ref[b].astype(_BF16), _NT, preferred_element_type=_F32)
        s_c = jnp.where(valid_c, s_c, _NEG_INF)
        s_n = jnp.where(valid_n, s_n, _NEG_INF)
        m = jnp.maximum(jnp.max(s_c, axis=-1, keepdims=True), jnp.max(s_n, axis=-1, keepdims=True))
        m = jnp.maximum(m, sink)
        e_c = jnp.exp(s_c - m)
        e_n = jnp.exp(s_n - m)
        den = (jnp.sum(e_c, axis=-1, keepdims=True) + jnp.sum(e_n, axis=-1, keepdims=True)
               + jnp.exp(sink - m))
        o_all = lax.dot_general((e_c / den).astype(_BF16), vt, _NT, preferred_element_type=_F32)
        o_all = o_all + jnp.dot((e_n / den).astype(_BF16), vn_ref[b].astype(_BF16),
                                preferred_element_type=_F32)
        for grp in range(A_HEADS // 2):
            halves = []
            for half, h in enumerate((2 * grp, 2 * grp + 1)):
                kvh = h // A_GROUP
                blk = o_all[h * STEP_NEW_ROWS:(h + 1) * STEP_NEW_ROWS,
                            (kvh // 2) * LANES:(kvh // 2 + 1) * LANES]
                halves.append(to_half(blk, kvh % 2, half))
            o_ref[b, :, grp * LANES:(grp + 1) * LANES] = jnp.where(low, halves[0], halves[1])


def _step_b_kernel(q_ref, kn_ref, vn_ref, cache_ref, o_ref, lse_ref, *, dil):
    n_h, length = cache_ref.shape[2], cache_ref.shape[4]
    width = n_h * HEAD_DIM
    n_rows = n_h * STEP_NEW_ROWS
    seg = lax.broadcasted_iota(jnp.int32, (STEP_NEW_ROWS, width), 1) // HEAD_DIM

    def per_head(fn):
        return [fn(h) for h in range(n_h)]

    def own_head(blocks):
        return functools.reduce(jnp.add, [jnp.where(seg == h, blk, 0.0) for h, blk in enumerate(blocks)])

    def step_of(shape):
        return (lax.broadcasted_iota(jnp.int32, shape, 0) % STEP_NEW_ROWS) % DEC_SEQ

    col_c = lax.broadcasted_iota(jnp.int32, (n_rows, length), 1)
    col_n = lax.broadcasted_iota(jnp.int32, (n_rows, STEP_NEW_ROWS), 1)
    if dil == 1:
        valid_c = col_c >= step_of((n_rows, length))
        valid_n = col_n <= step_of((n_rows, STEP_NEW_ROWS))
    else:
        valid_c = (col_c % dil) == step_of((n_rows, length))
        valid_n = col_n == step_of((n_rows, STEP_NEW_ROWS))
    rows = lambda x, h: x[h * STEP_NEW_ROWS:(h + 1) * STEP_NEW_ROWS]

    for b in range(q_ref.shape[0]):
        q_tok = q_ref[b]
        q_bd = jnp.concatenate(per_head(lambda h: jnp.where(seg == h, q_tok, 0.0)), axis=0).astype(_BF16)
        kt = cache_ref[b, 0].reshape(width, length).astype(_BF16)
        vt = cache_ref[b, 1].reshape(width, length).astype(_BF16)
        s_c = jnp.dot(q_bd, kt, preferred_element_type=_F32)
        s_n = lax.dot_general(q_bd, kn_ref[b].astype(_BF16), _NT, preferred_element_type=_F32)
        s_c = jnp.where(valid_c, s_c, _NEG_INF)
        s_n = jnp.where(valid_n, s_n, _NEG_INF)
        m = jnp.maximum(jnp.max(s_c, axis=-1, keepdims=True), jnp.max(s_n, axis=-1, keepdims=True))
        e_c = jnp.exp(s_c - m)
        e_n = jnp.exp(s_n - m)
        den = jnp.sum(e_c, axis=-1, keepdims=True) + jnp.sum(e_n, axis=-1, keepdims=True)
        o_all = lax.dot_general((e_c / den).astype(_BF16), vt, _NT, preferred_element_type=_F32)
        o_all = o_all + jnp.dot((e_n / den).astype(_BF16), vn_ref[b].astype(_BF16),
                                preferred_element_type=_F32)
        o_ref[b] = own_head(per_head(lambda h: rows(o_all, h)))
        lse = m + jnp.log(den)
        lse_ref[b] = own_head(per_head(lambda h: jnp.broadcast_to(rows(lse, h), (STEP_NEW_ROWS, width))))


def _step_b(q_s, kv_s, cache, dil):
    db, length = cache.shape[:2]
    gw = B_GROUP_WIDTH
    cache_t = cache.transpose(0, 2, 3, 4, 1)
    head_bytes = 2 * HEAD_DIM * length * 4
    hb = min(B_HEADS, max(1, STEP_CACHE_BYTES // head_bytes))
    bb = min(db, max(1, STEP_CACHE_BYTES // (head_bytes * hb)))
    pad = lambda x: jnp.pad(x.reshape(db, DEC_SEQ, gw), [(0, 0), (0, STEP_NEW_ROWS - DEC_SEQ), (0, 0)])
    tok_spec = pl.BlockSpec((bb, STEP_NEW_ROWS, hb * HEAD_DIM), lambda b, j: (b, 0, j))
    tok_sds = jax.ShapeDtypeStruct((db, STEP_NEW_ROWS, gw), _F32)
    o, lse = pl.pallas_call(
        functools.partial(_step_b_kernel, dil=dil),
        grid=(db // bb, B_HEADS // hb),
        in_specs=[tok_spec, tok_spec, tok_spec,
                  pl.BlockSpec((bb, 2, hb, HEAD_DIM, length), lambda b, j: (b, 0, j, 0, 0))],
        out_specs=[tok_spec, tok_spec],
        out_shape=[tok_sds, tok_sds],
        compiler_params=_cparams("parallel", "parallel"),
        name="step_b_l%d" % length,
    )(pad(q_s.astype(_F32)), pad(kv_s[:, :gw]), pad(kv_s[:, gw:]), cache_t)
    back = lambda x: x[:, :DEC_SEQ].reshape(db * DEC_SEQ, gw)
    return back(o), back(lse)


def _step_a(qkv_s, cache, sinks, db):
    nq = A_HEADS * HEAD_DIM
    nk = A_KV_HEADS * HEAD_DIM
    pad = lambda x: jnp.pad(x.reshape(db, DEC_SEQ, -1), [(0, 0), (0, STEP_NEW_ROWS - DEC_SEQ), (0, 0)])
    sink_rows = jnp.repeat(sinks, STEP_NEW_ROWS).reshape(A_HEADS * STEP_NEW_ROWS, 1)
    bb = STEP_A_SAMPLES
    q_spec = pl.BlockSpec((bb, STEP_NEW_ROWS, nq), lambda i: (i, 0, 0))
    n_spec = pl.BlockSpec((bb, STEP_NEW_ROWS, nk), lambda i: (i, 0, 0))
    o = pl.pallas_call(
        _step_a_kernel,
        grid=(db // bb,),
        in_specs=[q_spec, n_spec, n_spec,
                  pl.BlockSpec((bb, 2, A_KV_HEADS, HEAD_DIM, A_WINDOW), lambda i: (i, 0, 0, 0, 0)),
                  pl.BlockSpec(sink_rows.shape, lambda i: (0, 0))],
        out_specs=q_spec,
        out_shape=jax.ShapeDtypeStruct((db, STEP_NEW_ROWS, nq), _F32),
        compiler_params=_cparams("parallel"),
        name="step_a",
    )(pad(qkv_s[:, :nq]), pad(qkv_s[:, nq:nq + nk]), pad(qkv_s[:, nq + nk:]),
      cache.transpose(0, 2, 3, 4, 1), sink_rows)
    return o[:, :DEC_SEQ].reshape(db * DEC_SEQ, nq)


def _layer_norm_rows(z, g, b):
    mu = jnp.mean(z, axis=-1, keepdims=True)
    zc = z - mu
    var = jnp.mean(zc * zc, axis=-1, keepdims=True)
    return zc * lax.rsqrt(var + LN_EPS) * g + b


def _proj_ln_kernel(*refs, dils, has_bias):
    n_merge = len(dils)
    n_in = 2 * n_merge if n_merge else 1
    w_ref = refs[n_in]
    pos = n_in + 1
    if has_bias:
        bias_ref = refs[pos]
        pos += 1
    x_ref, g_ref, b_ref, out_ref = refs[pos:pos + 4]
    scratch = refs[pos + 4:]
    tm = x_ref.shape[0]

    if n_merge:
        o_scr, order_scr = scratch[0], scratch[1:]

        def token_order(idx, ref, d, lanes):
            if d == 1:
                return ref[:, lanes]
            scr = order_scr[idx]
            for r in range(d):
                scr[pl.ds(r, tm // d, stride=d), :] = ref[r, :, lanes]
            return scr[...]

        reordered = [i for i, d in enumerate(dils) if d > 1]
        for g in range(w_ref.shape[0] // LANES):
            lanes = slice(g * LANES, (g + 1) * LANES)
            os = [token_order(2 * reordered.index(i) if d > 1 else 0, refs[i], d, lanes)
                  for i, d in enumerate(dils)]
            ls = [token_order(2 * reordered.index(i) + 1 if d > 1 else 0, refs[n_merge + i], d, lanes)
                  for i, d in enumerate(dils)]
            m = functools.reduce(jnp.maximum, ls)
            es = [jnp.exp(l - m) for l in ls]
            den = functools.reduce(jnp.add, es)
            o_scr[:, lanes] = functools.reduce(
                jnp.add, [(e / den) * o for e, o in zip(es, os)]).astype(_BF16)
        o = o_scr[...]
    else:
        o = refs[0][...].astype(_BF16)
    y = jnp.dot(o, w_ref[...], preferred_element_type=_F32)
    if has_bias:
        y = y + bias_ref[...]
    out_ref[...] = _layer_norm_rows(DN_ALPHA * x_ref[...] + y, g_ref[...], b_ref[...])


def _proj_ln(o_list, lse_list, dils, w_bf16, bias, x, ln_g, ln_b, tm=256):
    n = x.shape[0]
    k = w_bf16.shape[0]
    row = lambda i: (i, 0)
    fixed = lambda i: (0, 0)
    tiles = SEQ // tm

    def in_spec(d):
        if d == 1:
            return pl.BlockSpec((tm, k), row)
        return pl.BlockSpec((d, tm // d, k), lambda i: (i // tiles, i % tiles, 0))

    args = list(o_list) + list(lse_list) + [w_bf16]
    specs = [in_spec(d) for d in (dils or (1,))] * (2 if lse_list else 1)
    specs.append(pl.BlockSpec((k, D_MODEL), fixed))
    if bias is not None:
        args.append(bias)
        specs.append(pl.BlockSpec((1, D_MODEL), fixed))
    args += [x, ln_g, ln_b]
    specs += [pl.BlockSpec((tm, D_MODEL), row), pl.BlockSpec((1, D_MODEL), fixed),
              pl.BlockSpec((1, D_MODEL), fixed)]
    scratch = []
    if lse_list:
        scratch = [pltpu.VMEM((tm, k), _BF16)]
        scratch += [pltpu.VMEM((tm, LANES), _F32)] * (2 * sum(d > 1 for d in dils))
    return pl.pallas_call(
        functools.partial(_proj_ln_kernel, dils=tuple(dils), has_bias=bias is not None),
        grid=(n // tm,),
        in_specs=specs,
        out_specs=pl.BlockSpec((tm, D_MODEL), row),
        out_shape=jax.ShapeDtypeStruct((n, D_MODEL), _F32),
        scratch_shapes=scratch,
        compiler_params=_cparams("parallel"),
        name="proj_ln",
    )(*args)


N_KEY_BLOCKS = PEER_HEADS * 2
HALF_KEY_DIM = PEER_KEY_DIM // 2


def _peer_score_kernel(x_ref, wq_ref, keys_ref, sc_ref):
    q = jnp.dot(x_ref[...].astype(_BF16), wq_ref[...], preferred_element_type=_F32).astype(_BF16)
    for blk in range(N_KEY_BLOCKS):
        sc_ref[blk * PEER_N_KEYS:(blk + 1) * PEER_N_KEYS, :] = lax.dot_general(
            keys_ref[blk], q[:, blk * HALF_KEY_DIM:(blk + 1) * HALF_KEY_DIM], _NT,
            preferred_element_type=_F32)


def _peer_scores(x, wq_bf16, keys_bf16, layer):
    n = x.shape[0]
    tm = TOKEN_TILE
    return pl.pallas_call(
        _peer_score_kernel,
        grid=(n // tm,),
        in_specs=[
            pl.BlockSpec((tm, D_MODEL), lambda i: (i, 0)),
            pl.BlockSpec((None, D_MODEL, PEER_HEADS * PEER_KEY_DIM), lambda i: (layer, 0, 0)),
            pl.BlockSpec((None, N_KEY_BLOCKS, PEER_N_KEYS, HALF_KEY_DIM), lambda i: (layer, 0, 0, 0)),
        ],
        out_specs=pl.BlockSpec((N_KEY_BLOCKS * PEER_N_KEYS, tm), lambda i: (0, i)),
        out_shape=jax.ShapeDtypeStruct((N_KEY_BLOCKS * PEER_N_KEYS, n), _F32),
        compiler_params=_cparams("parallel"),
        name="peer_scores",
    )(x, wq_bf16, keys_bf16)


TOPK_TOKENS = 128


def _top1_rows(vals, ids, big):
    m = jnp.max(vals, axis=0, keepdims=True)
    sel = jnp.min(jnp.where(vals == m, ids, big), axis=0, keepdims=True)
    return m, sel, ids == sel


def _sort16_network():
    pairs = []

    def merge(lo, n, r):
        step = r * 2
        if step < n:
            merge(lo, n, step)
            merge(lo + r, n, step)
            pairs.extend((i, i + r) for i in range(lo + r, lo + n - r, step))
        else:
            pairs.append((lo, lo + r))

    def sort(lo, n):
        if n > 1:
            sort(lo, n // 2)
            sort(lo + n // 2, n // 2)
            merge(lo, n, 1)

    sort(0, PEER_TOPK)
    return pairs


def _peer_topk_kernel(sc_ref, e_ref, gate_ref, top_s, top_i, sel_e, sel_g):
    tt = sc_ref.shape[1]
    sub = lax.broadcasted_iota(jnp.int32, (SUBLANES, tt), 0).astype(_F32)
    neg_inf = jnp.full((SUBLANES, tt), _NEG_INF, _F32)

    def bcast(row):
        return jnp.broadcast_to(row, (SUBLANES, tt))

    def sub_max(x):
        return jnp.max(x, axis=0, keepdims=True)

    def head_lists(h):
        s0, s1 = top_s[2 * h], top_s[2 * h + 1]
        i0, i1 = top_i[2 * h] * PEER_N_KEYS, top_i[2 * h + 1]
        return s0, s1, i0, i1

    def write_gates(h, best):
        ex = [jnp.exp(b - best[0]) for b in best]
        den = functools.reduce(jnp.add, ex)
        for k in range(PEER_TOPK):
            sel_g[h, k:k + 1, :] = ex[k] / den

    network = _sort16_network()

    def pop_heads(vals, ids, hit, depth):
        for p in range(depth):
            last = p + 1 == len(vals)
            vals[p] = jnp.where(hit, neg_inf if last else vals[p + 1], vals[p])
            if not last:
                ids[p] = jnp.where(hit, ids[p + 1], ids[p])

    def fast_stage1(blk, tie):
        start = pl.multiple_of(blk * PEER_N_KEYS, PEER_N_KEYS)
        vals = [sc_ref[pl.ds(start + SUBLANES * k, SUBLANES), :] for k in range(PEER_TOPK)]
        ids = [sub + float(SUBLANES * k) for k in range(PEER_TOPK)]
        for i, j in network:
            keep = vals[i] >= vals[j]
            vals[i], vals[j] = jnp.maximum(vals[i], vals[j]), jnp.minimum(vals[i], vals[j])
            ids[i], ids[j] = jnp.where(keep, ids[i], ids[j]), jnp.where(keep, ids[j], ids[i])
        prev = None
        for k in range(PEER_TOPK + 1):
            m = sub_max(vals[0])
            hit = vals[0] == m
            n_hit = jnp.sum(jnp.where(hit, 1.0, 0.0), axis=0, keepdims=True)
            tie = jnp.maximum(tie, jnp.where(n_hit > 1.5, 1.0, 0.0))
            if prev is not None:
                tie = jnp.maximum(tie, jnp.where(m == prev, 1.0, 0.0))
            prev = m
            if k == PEER_TOPK:
                break
            top_s[blk, k:k + 1, :] = m
            top_i[blk, k:k + 1, :] = sub_max(jnp.where(hit, ids[0], -1.0))
            pop_heads(vals, ids, hit, PEER_TOPK - k)
        return tie

    def fast_stage2(h, tie):
        s0, s1, i0, i1 = head_lists(h)
        lo_s, hi_s, lo_i, hi_i = s1[:SUBLANES], s1[SUBLANES:], i1[:SUBLANES], i1[SUBLANES:]
        chain = [bcast(s0[a:a + 1]) + lo_s for a in range(SUBLANES)]
        chain_e = [bcast(i0[a:a + 1]) + lo_i for a in range(SUBLANES)]
        single = [bcast(s0[0:1]) + hi_s, s0[SUBLANES:] + bcast(s1[0:1])]
        single_e = [bcast(i0[0:1]) + hi_i, i0[SUBLANES:] + bcast(i1[0:1])]
        best, prev = [], None
        for k in range(PEER_TOPK + 1):
            m = sub_max(jnp.maximum(chain[0], jnp.maximum(single[0], single[1])))
            hits = [chain[0] == m, single[0] == m, single[1] == m]
            n_hit = jnp.sum(functools.reduce(jnp.add, [jnp.where(x, 1.0, 0.0) for x in hits]),
                            axis=0, keepdims=True)
            tie = jnp.maximum(tie, jnp.where(n_hit > 1.5, 1.0, 0.0))
            if prev is not None:
                tie = jnp.maximum(tie, jnp.where(m == prev, 1.0, 0.0))
            prev = m
            if k == PEER_TOPK:
                break
            best.append(m)
            picked = [jnp.where(x, e, -1.0) for x, e in zip(hits, [chain_e[0]] + single_e)]
            sel_e[h, k:k + 1, :] = sub_max(functools.reduce(jnp.maximum, picked))
            pop_heads(chain, chain_e, hits[0], min(SUBLANES, PEER_TOPK - k))
            single = [jnp.where(x, neg_inf, v) for x, v in zip(hits[1:], single)]
        write_gates(h, best)
        return tie

    no_tie = jnp.zeros((1, tt), _F32)
    tied_keys = jnp.max(lax.fori_loop(0, N_KEY_BLOCKS, fast_stage1, no_tie, unroll=2))
    tied_sums = jnp.max(lax.fori_loop(0, PEER_HEADS, fast_stage2, no_tie, unroll=2))

    @pl.when(tied_keys > 0.5)
    def _():
        key_id = lax.broadcasted_iota(jnp.int32, (PEER_N_KEYS, tt), 0).astype(_F32)

        def stage1(blk, carry):
            s = sc_ref[pl.ds(pl.multiple_of(blk * PEER_N_KEYS, PEER_N_KEYS), PEER_N_KEYS), :]
            for k in range(PEER_TOPK):
                m, sel, onehot = _top1_rows(s, key_id, float(PEER_N_KEYS))
                top_s[blk, k:k + 1, :] = m
                top_i[blk, k:k + 1, :] = sel
                s = jnp.where(onehot, _NEG_INF, s)
            return carry

        lax.fori_loop(0, N_KEY_BLOCKS, stage1, 0, unroll=2)

    @pl.when(jnp.maximum(tied_keys, tied_sums) > 0.5)
    def _():
        flat_id = jnp.concatenate(
            [sub, sub + SUBLANES] + [sub + a * PEER_TOPK for a in range(1, SUBLANES)]
            + [(sub + SUBLANES) * PEER_TOPK], axis=0)

        def stage2(h, carry):
            s0, s1, i0, i1 = head_lists(h)
            lo_s, hi_s, lo_i, hi_i = s1[:SUBLANES], s1[SUBLANES:], i1[:SUBLANES], i1[SUBLANES:]
            cand = jnp.concatenate(
                [bcast(s0[0:1]) + lo_s, bcast(s0[0:1]) + hi_s]
                + [bcast(s0[a:a + 1]) + lo_s for a in range(1, SUBLANES)]
                + [s0[SUBLANES:] + bcast(s1[0:1])], axis=0)
            cand_e = jnp.concatenate(
                [bcast(i0[0:1]) + lo_i, bcast(i0[0:1]) + hi_i]
                + [bcast(i0[a:a + 1]) + lo_i for a in range(1, SUBLANES)]
                + [i0[SUBLANES:] + bcast(i1[0:1])], axis=0)
            best = []
            for k in range(PEER_TOPK):
                m, _, onehot = _top1_rows(cand, flat_id, float(PEER_TOPK * PEER_TOPK))
                best.append(m)
                sel_e[h, k:k + 1, :] = sub_max(jnp.where(onehot, cand_e, -1.0))
                cand = jnp.where(onehot, _NEG_INF, cand)
            write_gates(h, best)
            return carry

        lax.fori_loop(0, PEER_HEADS, stage2, 0, unroll=2)

    n_sel = PEER_HEADS * PEER_TOPK
    e_ref[...] = sel_e[...].reshape(n_sel, tt).T.astype(jnp.int32)
    gate_ref[...] = sel_g[...].reshape(n_sel, tt).T


def _peer_topk(sc_t):
    n = sc_t.shape[1]
    tt = TOPK_TOKENS
    n_sel = PEER_HEADS * PEER_TOPK
    return pl.pallas_call(
        _peer_topk_kernel,
        grid=(n // tt,),
        in_specs=[pl.BlockSpec((N_KEY_BLOCKS * PEER_N_KEYS, tt), lambda i: (0, i))],
        out_specs=[pl.BlockSpec((tt, n_sel), lambda i: (i, 0)),
                   pl.BlockSpec((tt, n_sel), lambda i: (i, 0))],
        out_shape=[jax.ShapeDtypeStruct((n, n_sel), jnp.int32),
                   jax.ShapeDtypeStruct((n, n_sel), _F32)],
        scratch_shapes=[pltpu.VMEM((N_KEY_BLOCKS, PEER_TOPK, tt), _F32),
                        pltpu.VMEM((N_KEY_BLOCKS, PEER_TOPK, tt), _F32),
                        pltpu.VMEM((PEER_HEADS, PEER_TOPK, tt), _F32),
                        pltpu.VMEM((PEER_HEADS, PEER_TOPK, tt), _F32)],
        compiler_params=_cparams("parallel"),
        name="peer_topk",
    )(sc_t)


GATE_TOKENS = 64
GATE_GROUP = 16


def _peer_gate_kernel(e_ref, gate_ref, g_ref, scr):
    tb, _, n_sel = e_ref.shape
    row_id = lax.broadcasted_iota(jnp.int32, (PEER_N_KEYS, n_sel), 0)

    def token(n):
        e = e_ref[n]
        gate = gate_ref[n]
        hi = gate.astype(_BF16).astype(_F32)
        lo = gate - hi
        is_i = row_id == (e >> 7)
        lhs = jnp.concatenate([jnp.where(is_i, hi, 0.0), jnp.where(is_i, lo, 0.0)], axis=1)
        one = jnp.where(row_id == (e & (PEER_N_KEYS - 1)), 1.0, 0.0)
        rhs = jnp.concatenate([one, one], axis=1)
        res = lax.dot_general(lhs.astype(_BF16), rhs.astype(_BF16), _NT, preferred_element_type=_F32)
        for ib in range(PEER_N_KEYS // SUBLANES):
            start = (ib * tb + n) * SUBLANES
            scr[start:start + SUBLANES, :] = res[ib * SUBLANES:(ib + 1) * SUBLANES]

    for grp in range(tb // GATE_GROUP):
        first = grp * GATE_GROUP
        for n in range(first, first + GATE_GROUP):
            token(n)
        for i in range(PEER_N_KEYS):
            start = ((i // SUBLANES) * tb + first) * SUBLANES + i % SUBLANES
            g_ref[first:first + GATE_GROUP, i * PEER_N_KEYS:(i + 1) * PEER_N_KEYS] = (
                scr[pl.ds(start, GATE_GROUP, stride=SUBLANES), :])


def _peer_gates(expert, gate):
    n, n_sel = expert.shape
    tb = GATE_TOKENS
    expert = expert.reshape(n, 1, n_sel)
    gate = gate.reshape(n, 1, n_sel)
    return pl.pallas_call(
        _peer_gate_kernel,
        grid=(n // tb,),
        in_specs=[pl.BlockSpec((tb, 1, n_sel), lambda i: (i, 0, 0)),
                  pl.BlockSpec((tb, 1, n_sel), lambda i: (i, 0, 0))],
        out_specs=pl.BlockSpec((tb, PEER_N_EXPERTS), lambda i: (i, 0)),
        out_shape=jax.ShapeDtypeStruct((n, PEER_N_EXPERTS), _F32),
        scratch_shapes=[pltpu.VMEM((tb * PEER_N_KEYS, PEER_N_KEYS), _F32)],
        compiler_params=_cparams("parallel"),
        name="peer_gates",
    )(expert, gate)


EXPERT_CHUNK = 1024
_SQRT_HALF = math.sqrt(0.5)


def _peer_main_kernel(x_ref, ut_ref, g_ref, v_ref, lng_ref, lnb_ref, o_ref, xb_scr, acc_scr):
    c = pl.program_id(1)

    @pl.when(c == 0)
    def _():
        xb_scr[...] = x_ref[...].astype(_BF16)
        acc_scr[...] = jnp.zeros_like(acc_scr)

    h_parts = []
    for k in range(ut_ref.shape[1] // MXU_COLS):
        cols = slice(k * MXU_COLS, (k + 1) * MXU_COLS)
        a = jnp.dot(xb_scr[...], ut_ref[:, cols], preferred_element_type=_F32)
        gelu = 0.5 * a * (1.0 + lax.erf(a * _SQRT_HALF))
        h_parts.append((g_ref[:, cols] * gelu).astype(_BF16))
    h = jnp.concatenate(h_parts, axis=1)
    acc_scr[...] += jnp.dot(h, v_ref[...], preferred_element_type=_F32)

    @pl.when(c == pl.num_programs(1) - 1)
    def _():
        o_ref[...] = _layer_norm_rows(DN_ALPHA * x_ref[...] + acc_scr[...], lng_ref[...], lnb_ref[...])


def _peer_main(x, ut_bf16, gates, v_bf16, ln_g, ln_b, layer):
    n = x.shape[0]
    tm, ce = TOKEN_TILE, EXPERT_CHUNK
    return pl.pallas_call(
        _peer_main_kernel,
        grid=(n // tm, PEER_N_EXPERTS // ce),
        in_specs=[
            pl.BlockSpec((tm, D_MODEL), lambda i, c: (i, 0)),
            pl.BlockSpec((None, D_MODEL, ce), lambda i, c: (layer, 0, c)),
            pl.BlockSpec((tm, ce), lambda i, c: (i, c)),
            pl.BlockSpec((None, ce, D_MODEL), lambda i, c: (layer, c, 0)),
            pl.BlockSpec((1, D_MODEL), lambda i, c: (0, 0)),
            pl.BlockSpec((1, D_MODEL), lambda i, c: (0, 0)),
        ],
        out_specs=pl.BlockSpec((tm, D_MODEL), lambda i, c: (i, 0)),
        out_shape=jax.ShapeDtypeStruct((n, D_MODEL), _F32),
        scratch_shapes=[pltpu.VMEM((tm, D_MODEL), _BF16), pltpu.VMEM((tm, D_MODEL), _F32)],
        compiler_params=_cparams("parallel", "arbitrary"),
        name="peer_main",
    )(x, ut_bf16, gates, v_bf16, ln_g, ln_b)


def _peer_ffn_ln(x, wq_bf16, keys_bf16, ut_bf16, v_bf16, ln_g, ln_b, layer):
    sc_t = _peer_scores(x, wq_bf16, keys_bf16, layer)
    expert, gate = _peer_topk(sc_t)
    gates = _peer_gates(expert, gate)
    return _peer_main(x, ut_bf16, gates, v_bf16, ln_g, ln_b, layer)


def _rope_tables(pos):
    half = HEAD_DIM // 2
    inv = ROPE_THETA ** (-jnp.arange(half, dtype=_F32) / half)
    ang = pos.astype(_F32)[:, None] * inv[None, :]
    cos, sin = jnp.cos(ang), jnp.sin(ang)
    cos = jnp.tile(cos, (1, LANES // half))
    sin = jnp.tile(jnp.concatenate([-sin, sin], axis=1), (1, LANES // HEAD_DIM))
    return cos, sin


def kernel(x_prompt, x_sample, cache_a_kv, cache_b1_kv, cache_b2_kv, cache_b3_kv, w_qkv_a, b_qkv_a, sinks_a, w_o_a, b_o_a, w_qkv_b, w_o_b, ln_mix_g, ln_mix_b, ln_ffn_g, ln_ffn_b, peer_w_query, peer_sub_keys, peer_u, peer_v):
    n_batch, seq = x_prompt.shape[:2]
    db, t_len = x_sample.shape[:2]
    xp = x_prompt.reshape(n_batch * seq, D_MODEL)
    xs = x_sample.reshape(db * t_len, D_MODEL)
    cos_p, sin_p = _rope_tables(jnp.tile(jnp.arange(seq, dtype=jnp.int32), n_batch))
    cos_s, sin_s = _rope_tables(jnp.tile(PAST_LEN + jnp.arange(t_len, dtype=jnp.int32), db))
    scale = HEAD_DIM ** -0.5
    row = lambda v: v.reshape(1, -1)
    caches_b = (cache_b1_kv, cache_b2_kv, cache_b3_kv)
    outs_kv = {}
    wq_all = peer_w_query.astype(_BF16)
    keys_all = peer_sub_keys.reshape(DEPTH, N_KEY_BLOCKS, PEER_N_KEYS, HALF_KEY_DIM).astype(_BF16)
    ut_all = jnp.swapaxes(peer_u, 1, 2).astype(_BF16)
    v_all = peer_v.astype(_BF16)

    for layer in range(DEPTH):
        i = layer // 2
        ln_g, ln_b = row(ln_mix_g[layer]), row(ln_mix_b[layer])
        if layer % 2 == 0:
            nq = A_HEADS * HEAD_DIM
            nk = A_KV_HEADS * HEAD_DIM
            col_scale = jnp.concatenate([jnp.full((nq,), scale, _F32), jnp.ones((2 * nk,), _F32)])
            w = (w_qkv_a[i] * col_scale).astype(_BF16)
            bias = row(b_qkv_a[i] * col_scale)
            qkv_p = _qkv_rope(xp, w, bias, cos_p, sin_p, nq + nk)
            qkv_s = _qkv_rope(xs, w, bias, cos_s, sin_s, nq + nk)
            o_p = _band_attention_a(qkv_p, sinks_a[i], n_batch)
            o_s = _step_a(qkv_s, cache_a_kv[i], sinks_a[i], db)
            wo = w_o_a[i].astype(_BF16)
            xp = _proj_ln([o_p], [], (), wo, row(b_o_a[i]), xp, ln_g, ln_b)
            xs = _proj_ln([o_s], [], (), wo, row(b_o_a[i]), xs, ln_g, ln_b)
            keep = min(A_WINDOW, seq)
            kv_p = qkv_p.reshape(n_batch, seq, A_QKV)[:, seq - keep:, nq:]
            outs_kv.setdefault("a_p", []).append(kv_p.reshape(n_batch, keep, 2, A_KV_HEADS, HEAD_DIM))
            outs_kv.setdefault("a_s", []).append(qkv_s[:, nq:].reshape(db, t_len, 2, A_KV_HEADS, HEAD_DIM))
        else:
            gw = B_GROUP_WIDTH
            col_scale = jnp.concatenate([jnp.full((N_B_GROUPS * gw,), scale, _F32),
                                         jnp.ones((2 * N_B_GROUPS * gw,), _F32)])
            w = (w_qkv_b[i] * col_scale).astype(_BF16)
            o_ps, l_ps, o_ss, l_ss = [], [], [], []
            for g, (window, dil) in enumerate(B_GROUPS):
                outs = _qkv_group(xp, w, cos_p, sin_p, dil, g)
                q_p, kv_p, kvr_p = outs if dil > 1 else (outs[0], outs[1], outs[1])
                q_s, kv_s = _qkv_group(xs, w, cos_s, sin_s, 1, g)
                o, l = _band_attention_b(q_p, kvr_p, dil, n_batch)
                o_ps.append(o)
                l_ps.append(l)
                kv_s5 = kv_s.reshape(db, t_len, 2, B_HEADS, HEAD_DIM)
                o, l = _step_b(q_s, kv_s, caches_b[g][i], dil)
                o_ss.append(o)
                l_ss.append(l)
                keep = min(window, seq)
                kv_keep = kv_p.reshape(n_batch, seq, 2 * gw)[:, seq - keep:]
                outs_kv.setdefault("b%d_p" % g, []).append(
                    kv_keep.reshape(n_batch, keep, 2, B_HEADS, HEAD_DIM))
                outs_kv.setdefault("b%d_s" % g, []).append(kv_s5)
            wo = w_o_b[i].astype(_BF16)
            xp = _proj_ln(o_ps, l_ps, [dil for _, dil in B_GROUPS], wo, None, xp, ln_g, ln_b)
            xs = _proj_ln(o_ss, l_ss, [1] * N_B_GROUPS, wo, None, xs, ln_g, ln_b)

        ffn_g, ffn_b = row(ln_ffn_g[layer]), row(ln_ffn_b[layer])
        xp = _peer_ffn_ln(xp, wq_all, keys_all, ut_all, v_all, ffn_g, ffn_b, layer)
        xs = _peer_ffn_ln(xs, wq_all, keys_all, ut_all, v_all, ffn_g, ffn_b, layer)

    return (xp.reshape(n_batch, seq, D_MODEL), xs.reshape(db, t_len, D_MODEL),
            jnp.stack(outs_kv["a_p"]), jnp.stack(outs_kv["a_s"]),
            jnp.stack(outs_kv["b0_p"]), jnp.stack(outs_kv["b0_s"]),
            jnp.stack(outs_kv["b1_p"]), jnp.stack(outs_kv["b1_s"]),
            jnp.stack(outs_kv["b2_p"]), jnp.stack(outs_kv["b2_s"]))
```

```python
import functools
import math

import jax
import jax.numpy as jnp
from jax import lax
from jax.experimental import pallas as pl
from jax.experimental.pallas import tpu as pltpu

D_MODEL = 2048
SEQ = 2048
DEPTH = 2
DEC_SEQ = 4
PAST_LEN = 8192
HEAD_DIM = 64
ROPE_THETA = 10000.0
BLOCK = 128
A_HEADS = D_MODEL // HEAD_DIM
A_KV_HEADS = A_HEADS // 8
A_GROUP = A_HEADS // A_KV_HEADS
A_WINDOW = 128
A_QKV = (A_HEADS + 2 * A_KV_HEADS) * HEAD_DIM
B_GROUPS = ((128, 1), (512, 4), (2048, 16))
N_B_GROUPS = len(B_GROUPS)
B_HEADS = D_MODEL // (2 * HEAD_DIM)
B_GROUP_WIDTH = B_HEADS * HEAD_DIM
B_KEYS = 128
PEER_HEADS = 8
PEER_N_KEYS = 128
PEER_N_EXPERTS = PEER_N_KEYS * PEER_N_KEYS
PEER_TOPK = 16
PEER_KEY_DIM = 256
DN_ALPHA = (2.0 * DEPTH) ** 0.25
LN_EPS = 1e-5

LANES = 128
SUBLANES = 8
TOKEN_TILE = 512
VMEM_LIMIT = 56 * 1024 * 1024

_F32 = jnp.float32
_BF16 = jnp.bfloat16
_NEG_INF = float("-inf")
_NT = (((1,), (1,)), ((), ()))


def _cparams(*sem):
    return pltpu.CompilerParams(dimension_semantics=sem, vmem_limit_bytes=VMEM_LIMIT)


MXU_COLS = 256


def _rope(blk, cos, sin):
    lane = lax.broadcasted_iota(jnp.int32, blk.shape, 1)
    first_half = (lane % HEAD_DIM) < (HEAD_DIM // 2)
    partner = jnp.where(first_half, pltpu.roll(blk, LANES - HEAD_DIM // 2, 1),
                        pltpu.roll(blk, HEAD_DIM // 2, 1))
    return blk * cos + partner * sin


def _project_chunks(xb, w_ref, emit, bias_ref=None):
    for c in range(w_ref.shape[1] // MXU_COLS):
        cols = slice(c * MXU_COLS, (c + 1) * MXU_COLS)
        acc = jnp.dot(xb, w_ref[:, cols], preferred_element_type=_F32)
        if bias_ref is not None:
            acc = acc + bias_ref[:, cols]
        for h in range(MXU_COLS // LANES):
            emit(c * (MXU_COLS // LANES) + h, acc[:, h * LANES:(h + 1) * LANES])


def _qkv_rope_kernel(x_ref, w_ref, b_ref, cos_ref, sin_ref, o_ref, *, n_rope_groups):
    def emit(g, blk):
        if g < n_rope_groups:
            blk = _rope(blk, cos_ref[...], sin_ref[...])
        o_ref[:, g * LANES:(g + 1) * LANES] = blk

    _project_chunks(x_ref[...].astype(_BF16), w_ref, emit, b_ref)


def _qkv_rope(x, w_bf16, bias, cos, sin, n_rope_cols):
    n, k = x.shape
    ncols = w_bf16.shape[1]
    tm = TOKEN_TILE
    return pl.pallas_call(
        functools.partial(_qkv_rope_kernel, n_rope_groups=n_rope_cols // LANES),
        grid=(n // tm,),
        in_specs=[
            pl.BlockSpec((tm, k), lambda i: (i, 0)),
            pl.BlockSpec((k, ncols), lambda i: (0, 0)),
            pl.BlockSpec((1, ncols), lambda i: (0, 0)),
            pl.BlockSpec((tm, LANES), lambda i: (i, 0)),
            pl.BlockSpec((tm, LANES), lambda i: (i, 0)),
        ],
        out_specs=pl.BlockSpec((tm, ncols), lambda i: (i, 0)),
        out_shape=jax.ShapeDtypeStruct((n, ncols), _F32),
        compiler_params=_cparams("parallel"),
        name="qkv_rope",
    )(x, w_bf16, bias, cos, sin)


def _qkv_group_kernel(x_ref, wq_ref, wk_ref, wv_ref, cos_ref, sin_ref, *refs, dil):
    if dil == 1:
        q_ref, kv_ref = refs
    else:
        q_ref, kv_ref, kvr_ref, scr = refs
    xb = x_ref[...].astype(_BF16)
    per_residue = x_ref.shape[0] // dil
    width = wq_ref.shape[1]

    def emitter(token_ref, residue_ref, rotate, col0, slot):
        def emit(g, blk):
            lanes = slice(col0 + g * LANES, col0 + (g + 1) * LANES)
            if rotate:
                blk = _rope(blk, cos_ref[...], sin_ref[...])
            if token_ref is not None:
                token_ref[:, lanes] = blk.astype(token_ref.dtype)
            if residue_ref is not None:
                scr[slot, g] = blk
                for r in range(dil):
                    residue_ref[r, :, lanes] = (
                        scr[slot, g, pl.ds(r, per_residue, stride=dil), :].astype(residue_ref.dtype))
        return emit

    kvr = kvr_ref if dil > 1 else None
    _project_chunks(xb, wq_ref, emitter(q_ref if dil == 1 else None, q_ref if dil > 1 else None,
                                        True, 0, 0))
    _project_chunks(xb, wk_ref, emitter(kv_ref, kvr, True, 0, 1))
    _project_chunks(xb, wv_ref, emitter(kv_ref, kvr, False, width, 2))


def _qkv_group(x, w_bf16, cos, sin, dil, group):
    n, k = x.shape
    tm, tn = TOKEN_TILE, B_GROUP_WIDTH
    kv_spec = pl.BlockSpec((tm, 2 * tn), lambda i: (i, 0))
    kv_sds = jax.ShapeDtypeStruct((n, 2 * tn), _F32)
    if dil == 1:
        out_specs = [pl.BlockSpec((tm, tn), lambda i: (i, 0)), kv_spec]
        out_shape = [jax.ShapeDtypeStruct((n, tn), _BF16), kv_sds]
        scratch = []
    else:
        tiles = SEQ // tm
        res_rows = (n // SEQ * dil, SEQ // dil)
        res_spec = lambda w: pl.BlockSpec((dil, tm // dil, w), lambda i: (i // tiles, i % tiles, 0))
        out_specs = [res_spec(tn), kv_spec, res_spec(2 * tn)]
        out_shape = [jax.ShapeDtypeStruct(res_rows + (tn,), _BF16), kv_sds,
                     jax.ShapeDtypeStruct(res_rows + (2 * tn,), _BF16)]
        scratch = [pltpu.VMEM((3, tn // LANES, tm, LANES), _F32)]
    w_spec = lambda part: pl.BlockSpec((k, tn), lambda i: (0, part * N_B_GROUPS + group),
                                       pipeline_mode=pl.Buffered(1))
    return pl.pallas_call(
        functools.partial(_qkv_group_kernel, dil=dil),
        grid=(n // tm,),
        in_specs=[
            pl.BlockSpec((tm, k), lambda i: (i, 0)),
            w_spec(0), w_spec(1), w_spec(2),
            pl.BlockSpec((tm, LANES), lambda i: (i, 0)),
            pl.BlockSpec((tm, LANES), lambda i: (i, 0)),
        ],
        out_specs=out_specs,
        out_shape=out_shape,
        scratch_shapes=scratch,
        compiler_params=_cparams("parallel"),
        name="qkv_group_d%d" % dil,
    )(x, w_bf16, w_bf16, w_bf16, cos, sin)


def _band_attn_kernel(sink_ref, q_ref, kc_ref, vc_ref, kp_ref, vp_ref, *out_refs, kv_share,
                      has_sink, with_lse):
    o_ref = out_refs[0]
    lse_ref = out_refs[1] if with_lse else None
    j = pl.program_id(2)
    qi = lax.broadcasted_iota(jnp.int32, (BLOCK, 2 * BLOCK), 0)
    kj = lax.broadcasted_iota(jnp.int32, (BLOCK, 2 * BLOCK), 1)
    first_key = jnp.maximum(qi, jnp.where(j > 0, 0, BLOCK))
    valid = (kj >= first_key) & (kj <= qi + BLOCK)
    lane = lax.broadcasted_iota(jnp.int32, (BLOCK, LANES), 1)
    low = lane < HEAD_DIM

    def keys(cur_ref, prev_ref, kv_head):
        grp = kv_head // 2 if kv_share else kv_head
        lanes = slice(grp * LANES, (grp + 1) * LANES)
        parts = [prev_ref[:, lanes], cur_ref[:, lanes]]
        if kv_share:
            keep = low if kv_head % 2 == 0 else ~low
            parts = [jnp.where(keep, x, pltpu.roll(x, HEAD_DIM, 1)) for x in parts]
        return jnp.concatenate(parts, axis=0).astype(_BF16)

    for p in range(q_ref.shape[1] // LANES):
        if kv_share == 0 or p % kv_share == 0:
            kv_head = p // kv_share if kv_share else p
            kcat = keys(kc_ref, kp_ref, kv_head)
            vcat = keys(vc_ref, vp_ref, kv_head)
        q2 = q_ref[:, p * LANES:(p + 1) * LANES]
        outs, lses = [], []
        for h in range(2):
            qm = jnp.where(low if h == 0 else ~low, q2, 0.0).astype(_BF16)
            s = lax.dot_general(qm, kcat, _NT, preferred_element_type=_F32)
            s = jnp.where(valid, s, _NEG_INF)
            m = jnp.max(s, axis=-1, keepdims=True)
            if has_sink:
                sink = sink_ref[2 * p + h]
                m = jnp.maximum(m, sink)
            e = jnp.exp(s - m)
            den = jnp.sum(e, axis=-1, keepdims=True)
            if has_sink:
                den = den + jnp.exp(sink - m)
            pn = (e / den).astype(_BF16)
            outs.append(jnp.dot(pn, vcat, preferred_element_type=_F32))
            lses.append(m + jnp.log(den))
        o_ref[:, p * LANES:(p + 1) * LANES] = jnp.where(low, outs[0], outs[1])
        if with_lse:
            lse_ref[:, p * LANES:(p + 1) * LANES] = jnp.where(low, lses[0], lses[1])


def _band_attention(sinks, q, kv, *, q_col, k_col, v_col, q_width, kv_width, n_seq, nb, kv_share,
                    with_lse, name):
    cur = lambda b, r, j, s: b * nb + j
    prev = lambda b, r, j, s: b * nb + jnp.maximum(j - 1, 0)
    blk = lambda rows, width, col: pl.BlockSpec((BLOCK, width), lambda *a: (rows(*a), col))
    in_specs = [blk(cur, q_width, q_col), blk(cur, kv_width, k_col), blk(cur, kv_width, v_col),
                blk(prev, kv_width, k_col), blk(prev, kv_width, v_col)]
    args = [q, kv, kv, kv, kv]
    o_spec = blk(cur, q_width, 0)
    o_sds = jax.ShapeDtypeStruct((n_seq * nb * BLOCK, q_width), _F32)
    return pl.pallas_call(
        functools.partial(_band_attn_kernel, kv_share=kv_share,
                          has_sink=sinks is not None, with_lse=with_lse),
        grid_spec=pltpu.PrefetchScalarGridSpec(
            num_scalar_prefetch=1,
            grid=(n_seq, 1, nb),
            in_specs=in_specs,
            out_specs=[o_spec, o_spec] if with_lse else [o_spec],
        ),
        out_shape=[o_sds, o_sds] if with_lse else [o_sds],
        compiler_params=_cparams("parallel", "arbitrary", "arbitrary"),
        name=name,
    )(jnp.zeros((1,), _F32) if sinks is None else sinks, *args)


def _band_attention_a(qkv, sinks, n_batch):
    nq = A_HEADS * HEAD_DIM
    nk = A_KV_HEADS * HEAD_DIM
    (o,) = _band_attention(sinks, qkv, qkv, q_col=0, k_col=nq // nk, v_col=nq // nk + 1,
                           q_width=nq, kv_width=nk, n_seq=n_batch, nb=SEQ // BLOCK,
                           kv_share=A_GROUP // 2, with_lse=False, name="band_attn_a")
    return o


def _band_attention_b(q, kv, dil, n_batch):
    gw = B_GROUP_WIDTH
    o, lse = _band_attention(None, q.reshape(-1, gw), kv.reshape(-1, 2 * gw), q_col=0, k_col=0, v_col=1,
                             q_width=gw, kv_width=gw, n_seq=n_batch * dil, nb=SEQ // dil // BLOCK,
                             kv_share=0, with_lse=True, name="band_attn_d%d" % dil)
    return o.reshape(q.shape), lse.reshape(q.shape)


STEP_NEW_ROWS = 8
STEP_CACHE_BYTES = 8 * 1024 * 1024


STEP_A_SAMPLES = 8


def _step_a_kernel(q_ref, kn_ref, vn_ref, cache_ref, sink_ref, o_ref):
    length = cache_ref.shape[-1]
    kv_width = A_KV_HEADS * HEAD_DIM
    n_rows = A_HEADS * STEP_NEW_ROWS
    lane = lax.broadcasted_iota(jnp.int32, (STEP_NEW_ROWS, LANES), 1)
    low = lane < HEAD_DIM
    zeros = jnp.zeros((STEP_NEW_ROWS, LANES), _F32)

    def step_of(shape):
        return (lax.broadcasted_iota(jnp.int32, shape, 0) % STEP_NEW_ROWS) % DEC_SEQ

    valid_c = lax.broadcasted_iota(jnp.int32, (n_rows, length), 1) >= step_of((n_rows, length))
    valid_n = (lax.broadcasted_iota(jnp.int32, (n_rows, STEP_NEW_ROWS), 1)
               <= step_of((n_rows, STEP_NEW_ROWS)))
    sink = sink_ref[...]

    def to_half(x, src_half, dst_half):
        return x if src_half == dst_half else pltpu.roll(x, HEAD_DIM, 1)

    for b in range(q_ref.shape[0]):
        q_tok = q_ref[b]
        blocks = []
        for h in range(A_HEADS):
            kvh = h // A_GROUP
            piece = to_half(q_tok[:, (h // 2) * LANES:(h // 2 + 1) * LANES], h % 2, kvh % 2)
            piece = jnp.where(low if kvh % 2 == 0 else ~low, piece, 0.0)
            groups = [zeros] * (kv_width // LANES)
            groups[kvh // 2] = piece
            blocks.append(jnp.concatenate(groups, axis=1))
        q_bd = jnp.concatenate(blocks, axis=0).astype(_BF16)
        kt = cache_ref[b, 0].reshape(kv_width, length).astype(_BF16)
        vt = cache_ref[b, 1].reshape(kv_width, length).astype(_BF16)
        s_c = jnp.dot(q_bd, kt, preferred_element_type=_F32)
        s_n = lax.dot_general(q_bd, kn_ref[b].astype(_BF16), _NT, preferred_element_type=_F32)
        s_c = jnp.where(valid_c, s_c, _NEG_INF)
        s_n = jnp.where(valid_n, s_n, _NEG_INF)
        m = jnp.maximum(jnp.max(s_c, axis=-1, keepdims=True), jnp.max(s_n, axis=-1, keepdims=True))
        m = jnp.maximum(m, sink)
        e_c = jnp.exp(s_c - m)
        e_n = jnp.exp(s_n - m)
        den = (jnp.sum(e_c, axis=-1, keepdims=True) + jnp.sum(e_n, axis=-1, keepdims=True)
               + jnp.exp(sink - m))
        o_all = lax.dot_general((e_c / den).astype(_BF16), vt, _NT, preferred_element_type=_F32)
        o_all = o_all + jnp.dot((e_n / den).astype(_BF16), vn_ref[b].astype(_BF16),
                                preferred_element_type=_F32)
        for grp in range(A_HEADS // 2):
            halves = []
            for half, h in enumerate((2 * grp, 2 * grp + 1)):
                kvh = h // A_GROUP
                blk = o_all[h * STEP_NEW_ROWS:(h + 1) * STEP_NEW_ROWS,
                            (kvh // 2) * LANES:(kvh // 2 + 1) * LANES]
                halves.append(to_half(blk, kvh % 2, half))
            o_ref[b, :, grp * LANES:(grp + 1) * LANES] = jnp.where(low, halves[0], halves[1])


def _step_b_kernel(q_ref, kn_ref, vn_ref, cache_ref, o_ref, lse_ref, *, dil):
    n_h, length = cache_ref.shape[2], cache_ref.shape[4]
    width = n_h * HEAD_DIM
    n_rows = n_h * STEP_NEW_ROWS
    seg = lax.broadcasted_iota(jnp.int32, (STEP_NEW_ROWS, width), 1) // HEAD_DIM

    def per_head(fn):
        return [fn(h) for h in range(n_h)]

    def own_head(blocks):
        return functools.reduce(jnp.add, [jnp.where(seg == h, blk, 0.0) for h, blk in enumerate(blocks)])

    def step_of(shape):
        return (lax.broadcasted_iota(jnp.int32, shape, 0) % STEP_NEW_ROWS) % DEC_SEQ

    col_c = lax.broadcasted_iota(jnp.int32, (n_rows, length), 1)
    col_n = lax.broadcasted_iota(jnp.int32, (n_rows, STEP_NEW_ROWS), 1)
    if dil == 1:
        valid_c = col_c >= step_of((n_rows, length))
        valid_n = col_n <= step_of((n_rows, STEP_NEW_ROWS))
    else:
        valid_c = (col_c % dil) == step_of((n_rows, length))
        valid_n = col_n == step_of((n_rows, STEP_NEW_ROWS))
    rows = lambda x, h: x[h * STEP_NEW_ROWS:(h + 1) * STEP_NEW_ROWS]

    for b in range(q_ref.shape[0]):
        q_tok = q_ref[b]
        q_bd = jnp.concatenate(per_head(lambda h: jnp.where(seg == h, q_tok, 0.0)), axis=0).astype(_BF16)
        kt = cache_ref[b, 0].reshape(width, length).astype(_BF16)
        vt = cache_ref[b, 1].reshape(width, length).astype(_BF16)
        s_c = jnp.dot(q_bd, kt, preferred_element_type=_F32)
        s_n = lax.dot_general(q_bd, kn_ref[b].astype(_BF16), _NT, preferred_element_type=_F32)
        s_c = jnp.where(valid_c, s_c, _NEG_INF)
        s_n = jnp.where(valid_n, s_n, _NEG_INF)
        m = jnp.maximum(jnp.max(s_c, axis=-1, keepdims=True), jnp.max(s_n, axis=-1, keepdims=True))
        e_c = jnp.exp(s_c - m)
        e_n = jnp.exp(s_n - m)
        den = jnp.sum(e_c, axis=-1, keepdims=True) + jnp.sum(e_n, axis=-1, keepdims=True)
        o_all = lax.dot_general((e_c / den).astype(_BF16), vt, _NT, preferred_element_type=_F32)
        o_all = o_all + jnp.dot((e_n / den).astype(_BF16), vn_ref[b].astype(_BF16),
                                preferred_element_type=_F32)
        o_ref[b] = own_head(per_head(lambda h: rows(o_all, h)))
        lse = m + jnp.log(den)
        lse_ref[b] = own_head(per_head(lambda h: jnp.broadcast_to(rows(lse, h), (STEP_NEW_ROWS, width))))


def _step_b(q_s, kv_s, cache, dil):
    db, length = cache.shape[:2]
    gw = B_GROUP_WIDTH
    cache_t = cache.transpose(0, 2, 3, 4, 1)
    head_bytes = 2 * HEAD_DIM * length * 4
    hb = min(B_HEADS, max(1, STEP_CACHE_BYTES // head_bytes))
    bb = min(db, max(1, STEP_CACHE_BYTES // (head_bytes * hb)))
    pad = lambda x: jnp.pad(x.reshape(db, DEC_SEQ, gw), [(0, 0), (0, STEP_NEW_ROWS - DEC_SEQ), (0, 0)])
    tok_spec = pl.BlockSpec((bb, STEP_NEW_ROWS, hb * HEAD_DIM), lambda b, j: (b, 0, j))
    tok_sds = jax.ShapeDtypeStruct((db, STEP_NEW_ROWS, gw), _F32)
    o, lse = pl.pallas_call(
        functools.partial(_step_b_kernel, dil=dil),
        grid=(db // bb, B_HEADS // hb),
        in_specs=[tok_spec, tok_spec, tok_spec,
                  pl.BlockSpec((bb, 2, hb, HEAD_DIM, length), lambda b, j: (b, 0, j, 0, 0))],
        out_specs=[tok_spec, tok_spec],
        out_shape=[tok_sds, tok_sds],
        compiler_params=_cparams("parallel", "parallel"),
        name="step_b_l%d" % length,
    )(pad(q_s.astype(_F32)), pad(kv_s[:, :gw]), pad(kv_s[:, gw:]), cache_t)
    back = lambda x: x[:, :DEC_SEQ].reshape(db * DEC_SEQ, gw)
    return back(o), back(lse)


def _step_a(qkv_s, cache, sinks, db):
    nq = A_HEADS * HEAD_DIM
    nk = A_KV_HEADS * HEAD_DIM
    pad = lambda x: jnp.pad(x.reshape(db, DEC_SEQ, -1), [(0, 0), (0, STEP_NEW_ROWS - DEC_SEQ), (0, 0)])
    sink_rows = jnp.repeat(sinks, STEP_NEW_ROWS).reshape(A_HEADS * STEP_NEW_ROWS, 1)
    bb = STEP_A_SAMPLES
    q_spec = pl.BlockSpec((bb, STEP_NEW_ROWS, nq), lambda i: (i, 0, 0))
    n_spec = pl.BlockSpec((bb, STEP_NEW_ROWS, nk), lambda i: (i, 0, 0))
    o = pl.pallas_call(
        _step_a_kernel,
        grid=(db // bb,),
        in_specs=[q_spec, n_spec, n_spec,
                  pl.BlockSpec((bb, 2, A_KV_HEADS, HEAD_DIM, A_WINDOW), lambda i: (i, 0, 0, 0, 0)),
                  pl.BlockSpec(sink_rows.shape, lambda i: (0, 0))],
        out_specs=q_spec,
        out_shape=jax.ShapeDtypeStruct((db, STEP_NEW_ROWS, nq), _F32),
        compiler_params=_cparams("parallel"),
        name="step_a",
    )(pad(qkv_s[:, :nq]), pad(qkv_s[:, nq:nq + nk]), pad(qkv_s[:, nq + nk:]),
      cache.transpose(0, 2, 3, 4, 1), sink_rows)
    return o[:, :DEC_SEQ].reshape(db * DEC_SEQ, nq)


def _layer_norm_rows(z, g, b):
    mu = jnp.mean(z, axis=-1, keepdims=True)
    zc = z - mu
    var = jnp.mean(zc * zc, axis=-1, keepdims=True)
    return zc * lax.rsqrt(var + LN_EPS) * g + b


def _proj_ln_kernel(*refs, dils, has_bias):
    n_merge = len(dils)
    n_in = 2 * n_merge if n_merge else 1
    w_ref = refs[n_in]
    pos = n_in + 1
    if has_bias:
        bias_ref = refs[pos]
        pos += 1
    x_ref, g_ref, b_ref, out_ref = refs[pos:pos + 4]
    scratch = refs[pos + 4:]
    tm = x_ref.shape[0]

    if n_merge:
        o_scr, order_scr = scratch[0], scratch[1:]

        def token_order(idx, ref, d, lanes):
            if d == 1:
                return ref[:, lanes]
            scr = order_scr[idx]
            for r in range(d):
                scr[pl.ds(r, tm // d, stride=d), :] = ref[r, :, lanes]
            return scr[...]

        reordered = [i for i, d in enumerate(dils) if d > 1]
        for g in range(w_ref.shape[0] // LANES):
            lanes = slice(g * LANES, (g + 1) * LANES)
            os = [token_order(2 * reordered.index(i) if d > 1 else 0, refs[i], d, lanes)
                  for i, d in enumerate(dils)]
            ls = [token_order(2 * reordered.index(i) + 1 if d > 1 else 0, refs[n_merge + i], d, lanes)
                  for i, d in enumerate(dils)]
            m = functools.reduce(jnp.maximum, ls)
            es = [jnp.exp(l - m) for l in ls]
            den = functools.reduce(jnp.add, es)
            o_scr[:, lanes] = functools.reduce(
                jnp.add, [(e / den) * o for e, o in zip(es, os)]).astype(_BF16)
        o = o_scr[...]
    else:
        o = refs[0][...].astype(_BF16)
    y = jnp.dot(o, w_ref[...], preferred_element_type=_F32)
    if has_bias:
        y = y + bias_ref[...]
    out_ref[...] = _layer_norm_rows(DN_ALPHA * x_ref[...] + y, g_ref[...], b_ref[...])


def _proj_ln(o_list, lse_list, dils, w_bf16, bias, x, ln_g, ln_b, tm=256):
    n = x.shape[0]
    k = w_bf16.shape[0]
    row = lambda i: (i, 0)
    fixed = lambda i: (0, 0)
    tiles = SEQ // tm

    def in_spec(d):
        if d == 1:
            return pl.BlockSpec((tm, k), row)
        return pl.BlockSpec((d, tm // d, k), lambda i: (i // tiles, i % tiles, 0))

    args = list(o_list) + list(lse_list) + [w_bf16]
    specs = [in_spec(d) for d in (dils or (1,))] * (2 if lse_list else 1)
    specs.append(pl.BlockSpec((k, D_MODEL), fixed))
    if bias is not None:
        args.append(bias)
        specs.append(pl.BlockSpec((1, D_MODEL), fixed))
    args += [x, ln_g, ln_b]
    specs += [pl.BlockSpec((tm, D_MODEL), row), pl.BlockSpec((1, D_MODEL), fixed),
              pl.BlockSpec((1, D_MODEL), fixed)]
    scratch = []
    if lse_list:
        scratch = [pltpu.VMEM((tm, k), _BF16)]
        scratch += [pltpu.VMEM((tm, LANES), _F32)] * (2 * sum(d > 1 for d in dils))
    return pl.pallas_call(
        functools.partial(_proj_ln_kernel, dils=tuple(dils), has_bias=bias is not None),
        grid=(n // tm,),
        in_specs=specs,
        out_specs=pl.BlockSpec((tm, D_MODEL), row),
        out_shape=jax.ShapeDtypeStruct((n, D_MODEL), _F32),
        scratch_shapes=scratch,
        compiler_params=_cparams("parallel"),
        name="proj_ln",
    )(*args)


N_KEY_BLOCKS = PEER_HEADS * 2
HALF_KEY_DIM = PEER_KEY_DIM // 2


def _peer_score_kernel(x_ref, wq_ref, keys_ref, sc_ref):
    q = jnp.dot(x_ref[...].astype(_BF16), wq_ref[...], preferred_element_type=_F32).astype(_BF16)
    for blk in range(N_KEY_BLOCKS):
        sc_ref[blk * PEER_N_KEYS:(blk + 1) * PEER_N_KEYS, :] = lax.dot_general(
            keys_ref[blk], q[:, blk * HALF_KEY_DIM:(blk + 1) * HALF_KEY_DIM], _NT,
            preferred_element_type=_F32)


def _peer_scores(x, wq_bf16, keys_bf16, layer):
    n = x.shape[0]
    tm = TOKEN_TILE
    return pl.pallas_call(
        _peer_score_kernel,
        grid=(n // tm,),
        in_specs=[
            pl.BlockSpec((tm, D_MODEL), lambda i: (i, 0)),
            pl.BlockSpec((None, D_MODEL, PEER_HEADS * PEER_KEY_DIM), lambda i: (layer, 0, 0)),
            pl.BlockSpec((None, N_KEY_BLOCKS, PEER_N_KEYS, HALF_KEY_DIM), lambda i: (layer, 0, 0, 0)),
        ],
        out_specs=pl.BlockSpec((N_KEY_BLOCKS * PEER_N_KEYS, tm), lambda i: (0, i)),
        out_shape=jax.ShapeDtypeStruct((N_KEY_BLOCKS * PEER_N_KEYS, n), _F32),
        compiler_params=_cparams("parallel"),
        name="peer_scores",
    )(x, wq_bf16, keys_bf16)


TOPK_TOKENS = 128


def _top1_rows(vals, ids, big):
    m = jnp.max(vals, axis=0, keepdims=True)
    sel = jnp.min(jnp.where(vals == m, ids, big), axis=0, keepdims=True)
    return m, sel, ids == sel


def _sort16_network():
    pairs = []

    def merge(lo, n, r):
        step = r * 2
        if step < n:
            merge(lo, n, step)
            merge(lo + r, n, step)
            pairs.extend((i, i + r) for i in range(lo + r, lo + n - r, step))
        else:
            pairs.append((lo, lo + r))

    def sort(lo, n):
        if n > 1:
            sort(lo, n // 2)
            sort(lo + n // 2, n // 2)
            merge(lo, n, 1)

    sort(0, PEER_TOPK)
    return pairs


def _peer_topk_kernel(sc_ref, e_ref, gate_ref, top_s, top_i, sel_e, sel_g):
    tt = sc_ref.shape[1]
    sub = lax.broadcasted_iota(jnp.int32, (SUBLANES, tt), 0).astype(_F32)
    neg_inf = jnp.full((SUBLANES, tt), _NEG_INF, _F32)

    def bcast(row):
        return jnp.broadcast_to(row, (SUBLANES, tt))

    def sub_max(x):
        return jnp.max(x, axis=0, keepdims=True)

    def head_lists(h):
        s0, s1 = top_s[2 * h], top_s[2 * h + 1]
        i0, i1 = top_i[2 * h] * PEER_N_KEYS, top_i[2 * h + 1]
        return s0, s1, i0, i1

    def write_gates(h, best):
        ex = [jnp.exp(b - best[0]) for b in best]
        den = functools.reduce(jnp.add, ex)
        for k in range(PEER_TOPK):
            sel_g[h, k:k + 1, :] = ex[k] / den

    network = _sort16_network()

    def pop_heads(vals, ids, hit, depth):
        for p in range(depth):
            last = p + 1 == len(vals)
            vals[p] = jnp.where(hit, neg_inf if last else vals[p + 1], vals[p])
            if not last:
                ids[p] = jnp.where(hit, ids[p + 1], ids[p])

    def fast_stage1(blk, tie):
        start = pl.multiple_of(blk * PEER_N_KEYS, PEER_N_KEYS)
        vals = [sc_ref[pl.ds(start + SUBLANES * k, SUBLANES), :] for k in range(PEER_TOPK)]
        ids = [sub + float(SUBLANES * k) for k in range(PEER_TOPK)]
        for i, j in network:
            keep = vals[i] >= vals[j]
            vals[i], vals[j] = jnp.maximum(vals[i], vals[j]), jnp.minimum(vals[i], vals[j])
            ids[i], ids[j] = jnp.where(keep, ids[i], ids[j]), jnp.where(keep, ids[j], ids[i])
        prev = None
        for k in range(PEER_TOPK + 1):
            m = sub_max(vals[0])
            hit = vals[0] == m
            n_hit = jnp.sum(jnp.where(hit, 1.0, 0.0), axis=0, keepdims=True)
            tie = jnp.maximum(tie, jnp.where(n_hit > 1.5, 1.0, 0.0))
            if prev is not None:
                tie = jnp.maximum(tie, jnp.where(m == prev, 1.0, 0.0))
            prev = m
            if k == PEER_TOPK:
                break
            top_s[blk, k:k + 1, :] = m
            top_i[blk, k:k + 1, :] = sub_max(jnp.where(hit, ids[0], -1.0))
            pop_heads(vals, ids, hit, PEER_TOPK - k)
        return tie

    def fast_stage2(h, tie):
        s0, s1, i0, i1 = head_lists(h)
        lo_s, hi_s, lo_i, hi_i = s1[:SUBLANES], s1[SUBLANES:], i1[:SUBLANES], i1[SUBLANES:]
        chain = [bcast(s0[a:a + 1]) + lo_s for a in range(SUBLANES)]
        chain_e = [bcast(i0[a:a + 1]) + lo_i for a in range(SUBLANES)]
        single = [bcast(s0[0:1]) + hi_s, s0[SUBLANES:] + bcast(s1[0:1])]
        single_e = [bcast(i0[0:1]) + hi_i, i0[SUBLANES:] + bcast(i1[0:1])]
        best, prev = [], None
        for k in range(PEER_TOPK + 1):
            m = sub_max(jnp.maximum(chain[0], jnp.maximum(single[0], single[1])))
            hits = [chain[0] == m, single[0] == m, single[1] == m]
            n_hit = jnp.sum(functools.reduce(jnp.add, [jnp.where(x, 1.0, 0.0) for x in hits]),
                            axis=0, keepdims=True)
            tie = jnp.maximum(tie, jnp.where(n_hit > 1.5, 1.0, 0.0))
            if prev is not None:
                tie = jnp.maximum(tie, jnp.where(m == prev, 1.0, 0.0))
            prev = m
            if k == PEER_TOPK:
                break
            best.append(m)
            picked = [jnp.where(x, e, -1.0) for x, e in zip(hits, [chain_e[0]] + single_e)]
            sel_e[h, k:k + 1, :] = sub_max(functools.reduce(jnp.maximum, picked))
            pop_heads(chain, chain_e, hits[0], min(SUBLANES, PEER_TOPK - k))
            single = [jnp.where(x, neg_inf, v) for x, v in zip(hits[1:], single)]
        write_gates(h, best)
        return tie

    no_tie = jnp.zeros((1, tt), _F32)
    tied_keys = jnp.max(lax.fori_loop(0, N_KEY_BLOCKS, fast_stage1, no_tie, unroll=2))
    tied_sums = jnp.max(lax.fori_loop(0, PEER_HEADS, fast_stage2, no_tie, unroll=2))

    @pl.when(tied_keys > 0.5)
    def _():
        key_id = lax.broadcasted_iota(jnp.int32, (PEER_N_KEYS, tt), 0).astype(_F32)

        def stage1(blk, carry):
            s = sc_ref[pl.ds(pl.multiple_of(blk * PEER_N_KEYS, PEER_N_KEYS), PEER_N_KEYS), :]
            for k in range(PEER_TOPK):
                m, sel, onehot = _top1_rows(s, key_id, float(PEER_N_KEYS))
                top_s[blk, k:k + 1, :] = m
                top_i[blk, k:k + 1, :] = sel
                s = jnp.where(onehot, _NEG_INF, s)
            return carry

        lax.fori_loop(0, N_KEY_BLOCKS, stage1, 0, unroll=2)

    @pl.when(jnp.maximum(tied_keys, tied_sums) > 0.5)
    def _():
        flat_id = jnp.concatenate(
            [sub, sub + SUBLANES] + [sub + a * PEER_TOPK for a in range(1, SUBLANES)]
            + [(sub + SUBLANES) * PEER_TOPK], axis=0)

        def stage2(h, carry):
            s0, s1, i0, i1 = head_lists(h)
            lo_s, hi_s, lo_i, hi_i = s1[:SUBLANES], s1[SUBLANES:], i1[:SUBLANES], i1[SUBLANES:]
            cand = jnp.concatenate(
                [bcast(s0[0:1]) + lo_s, bcast(s0[0:1]) + hi_s]
                + [bcast(s0[a:a + 1]) + lo_s for a in range(1, SUBLANES)]
                + [s0[SUBLANES:] + bcast(s1[0:1])], axis=0)
            cand_e = jnp.concatenate(
                [bcast(i0[0:1]) + lo_i, bcast(i0[0:1]) + hi_i]
                + [bcast(i0[a:a + 1]) + lo_i for a in range(1, SUBLANES)]
                + [i0[SUBLANES:] + bcast(i1[0:1])], axis=0)
            best = []
            for k in range(PEER_TOPK):
                m, _, onehot = _top1_rows(cand, flat_id, float(PEER_TOPK * PEER_TOPK))
                best.append(m)
                sel_e[h, k:k + 1, :] = sub_max(jnp.where(onehot, cand_e, -1.0))
                cand = jnp.where(onehot, _NEG_INF, cand)
            write_gates(h, best)
            return carry

        lax.fori_loop(0, PEER_HEADS, stage2, 0, unroll=2)

    n_sel = PEER_HEADS * PEER_TOPK
    e_ref[...] = sel_e[...].reshape(n_sel, tt).T.astype(jnp.int32)
    gate_ref[...] = sel_g[...].reshape(n_sel, tt).T


def _peer_topk(sc_t):
    n = sc_t.shape[1]
    tt = TOPK_TOKENS
    n_sel = PEER_HEADS * PEER_TOPK
    return pl.pallas_call(
        _peer_topk_kernel,
        grid=(n // tt,),
        in_specs=[pl.BlockSpec((N_KEY_BLOCKS * PEER_N_KEYS, tt), lambda i: (0, i))],
        out_specs=[pl.BlockSpec((tt, n_sel), lambda i: (i, 0)),
                   pl.BlockSpec((tt, n_sel), lambda i: (i, 0))],
        out_shape=[jax.ShapeDtypeStruct((n, n_sel), jnp.int32),
                   jax.ShapeDtypeStruct((n, n_sel), _F32)],
        scratch_shapes=[pltpu.VMEM((N_KEY_BLOCKS, PEER_TOPK, tt), _F32),
                        pltpu.VMEM((N_KEY_BLOCKS, PEER_TOPK, tt), _F32),
                        pltpu.VMEM((PEER_HEADS, PEER_TOPK, tt), _F32),
                        pltpu.VMEM((PEER_HEADS, PEER_TOPK, tt), _F32)],
        compiler_params=_cparams("parallel"),
        name="peer_topk",
    )(sc_t)


GATE_TOKENS = 64
GATE_GROUP = 16


def _peer_gate_kernel(e_ref, gate_ref, g_ref, scr):
    tb, _, n_sel = e_ref.shape
    row_id = lax.broadcasted_iota(jnp.int32, (PEER_N_KEYS, n_sel), 0)

    def token(n):
        e = e_ref[n]
        gate = gate_ref[n]
        hi = gate.astype(_BF16).astype(_F32)
        lo = gate - hi
        is_i = row_id == (e >> 7)
        lhs = jnp.concatenate([jnp.where(is_i, hi, 0.0), jnp.where(is_i, lo, 0.0)], axis=1)
        one = jnp.where(row_id == (e & (PEER_N_KEYS - 1)), 1.0, 0.0)
        rhs = jnp.concatenate([one, one], axis=1)
        res = lax.dot_general(lhs.astype(_BF16), rhs.astype(_BF16), _NT, preferred_element_type=_F32)
        for ib in range(PEER_N_KEYS // SUBLANES):
            start = (ib * tb + n) * SUBLANES
            scr[start:start + SUBLANES, :] = res[ib * SUBLANES:(ib + 1) * SUBLANES]

    for grp in range(tb // GATE_GROUP):
        first = grp * GATE_GROUP
        for n in range(first, first + GATE_GROUP):
            token(n)
        for i in range(PEER_N_KEYS):
            start = ((i // SUBLANES) * tb + first) * SUBLANES + i % SUBLANES
            g_ref[first:first + GATE_GROUP, i * PEER_N_KEYS:(i + 1) * PEER_N_KEYS] = (
                scr[pl.ds(start, GATE_GROUP, stride=SUBLANES), :])


def _peer_gates(expert, gate):
    n, n_sel = expert.shape
    tb = GATE_TOKENS
    expert = expert.reshape(n, 1, n_sel)
    gate = gate.reshape(n, 1, n_sel)
    return pl.pallas_call(
        _peer_gate_kernel,
        grid=(n // tb,),
        in_specs=[pl.BlockSpec((tb, 1, n_sel), lambda i: (i, 0, 0)),
                  pl.BlockSpec((tb, 1, n_sel), lambda i: (i, 0, 0))],
        out_specs=pl.BlockSpec((tb, PEER_N_EXPERTS), lambda i: (i, 0)),
        out_shape=jax.ShapeDtypeStruct((n, PEER_N_EXPERTS), _F32),
        scratch_shapes=[pltpu.VMEM((tb * PEER_N_KEYS, PEER_N_KEYS), _F32)],
        compiler_params=_cparams("parallel"),
        name="peer_gates",
    )(expert, gate)


EXPERT_CHUNK = 1024
_SQRT_HALF = math.sqrt(0.5)


def _peer_main_kernel(x_ref, u_ref, g_ref, v_ref, lng_ref, lnb_ref, o_ref, xb_scr, acc_scr):
    c = pl.program_id(1)

    @pl.when(c == 0)
    def _():
        xb_scr[...] = x_ref[...].astype(_BF16)
        acc_scr[...] = jnp.zeros_like(acc_scr)

    h_parts = []
    for k in range(u_ref.shape[0] // MXU_COLS):
        cols = slice(k * MXU_COLS, (k + 1) * MXU_COLS)
        a = lax.dot_general(xb_scr[...], u_ref[cols, :], _NT, preferred_element_type=_F32)
        gelu = 0.5 * a * (1.0 + lax.erf(a * _SQRT_HALF))
        h_parts.append((g_ref[:, cols] * gelu).astype(_BF16))
    h = jnp.concatenate(h_parts, axis=1)
    acc_scr[...] += jnp.dot(h, v_ref[...], preferred_element_type=_F32)

    @pl.when(c == pl.num_programs(1) - 1)
    def _():
        o_ref[...] = _layer_norm_rows(DN_ALPHA * x_ref[...] + acc_scr[...], lng_ref[...], lnb_ref[...])


def _peer_main(x, u_bf16, gates, v_bf16, ln_g, ln_b, layer):
    n = x.shape[0]
    tm, ce = TOKEN_TILE, EXPERT_CHUNK
    return pl.pallas_call(
        _peer_main_kernel,
        grid=(n // tm, PEER_N_EXPERTS // ce),
        in_specs=[
            pl.BlockSpec((tm, D_MODEL), lambda i, c: (i, 0)),
            pl.BlockSpec((None, ce, D_MODEL), lambda i, c: (layer, c, 0)),
            pl.BlockSpec((tm, ce), lambda i, c: (i, c)),
            pl.BlockSpec((None, ce, D_MODEL), lambda i, c: (layer, c, 0)),
            pl.BlockSpec((1, D_MODEL), lambda i, c: (0, 0)),
            pl.BlockSpec((1, D_MODEL), lambda i, c: (0, 0)),
        ],
        out_specs=pl.BlockSpec((tm, D_MODEL), lambda i, c: (i, 0)),
        out_shape=jax.ShapeDtypeStruct((n, D_MODEL), _F32),
        scratch_shapes=[pltpu.VMEM((tm, D_MODEL), _BF16), pltpu.VMEM((tm, D_MODEL), _F32)],
        compiler_params=_cparams("parallel", "arbitrary"),
        name="peer_main",
    )(x, u_bf16, gates, v_bf16, ln_g, ln_b)


def _peer_ffn_ln(x, wq_bf16, keys_bf16, u_bf16, v_bf16, ln_g, ln_b, layer):
    sc_t = _peer_scores(x, wq_bf16, keys_bf16, layer)
    expert, gate = _peer_topk(sc_t)
    gates = _peer_gates(expert, gate)
    return _peer_main(x, u_bf16, gates, v_bf16, ln_g, ln_b, layer)


def _rope_tables(pos):
    half = HEAD_DIM // 2
    inv = ROPE_THETA ** (-jnp.arange(half, dtype=_F32) / half)
    ang = pos.astype(_F32)[:, None] * inv[None, :]
    cos, sin = jnp.cos(ang), jnp.sin(ang)
    cos = jnp.tile(cos, (1, LANES // half))
    sin = jnp.tile(jnp.concatenate([-sin, sin], axis=1), (1, LANES // HEAD_DIM))
    return cos, sin


def kernel(x_prompt, x_sample, cache_a_kv, cache_b1_kv, cache_b2_kv, cache_b3_kv, w_qkv_a, b_qkv_a, sinks_a, w_o_a, b_o_a, w_qkv_b, w_o_b, ln_mix_g, ln_mix_b, ln_ffn_g, ln_ffn_b, peer_w_query, peer_sub_keys, peer_u, peer_v):
    n_batch, seq = x_prompt.shape[:2]
    db, t_len = x_sample.shape[:2]
    xp = x_prompt.reshape(n_batch * seq, D_MODEL)
    xs = x_sample.reshape(db * t_len, D_MODEL)
    cos_p, sin_p = _rope_tables(jnp.tile(jnp.arange(seq, dtype=jnp.int32), n_batch))
    cos_s, sin_s = _rope_tables(jnp.tile(PAST_LEN + jnp.arange(t_len, dtype=jnp.int32), db))
    scale = HEAD_DIM ** -0.5
    row = lambda v: v.reshape(1, -1)
    caches_b = (cache_b1_kv, cache_b2_kv, cache_b3_kv)
    outs_kv = {}
    wq_all = peer_w_query.astype(_BF16)
    keys_all = peer_sub_keys.reshape(DEPTH, N_KEY_BLOCKS, PEER_N_KEYS, HALF_KEY_DIM).astype(_BF16)
    u_all = peer_u.astype(_BF16)
    v_all = peer_v.astype(_BF16)

    for layer in range(DEPTH):
        i = layer // 2
        ln_g, ln_b = row(ln_mix_g[layer]), row(ln_mix_b[layer])
        if layer % 2 == 0:
            nq = A_HEADS * HEAD_DIM
            nk = A_KV_HEADS * HEAD_DIM
            col_scale = jnp.concatenate([jnp.full((nq,), scale, _F32), jnp.ones((2 * nk,), _F32)])
            w = (w_qkv_a[i] * col_scale).astype(_BF16)
            bias = row(b_qkv_a[i] * col_scale)
            qkv_p = _qkv_rope(xp, w, bias, cos_p, sin_p, nq + nk)
            qkv_s = _qkv_rope(xs, w, bias, cos_s, sin_s, nq + nk)
            o_p = _band_attention_a(qkv_p, sinks_a[i], n_batch)
            o_s = _step_a(qkv_s, cache_a_kv[i], sinks_a[i], db)
            wo = w_o_a[i].astype(_BF16)
            xp = _proj_ln([o_p], [], (), wo, row(b_o_a[i]), xp, ln_g, ln_b)
            xs = _proj_ln([o_s], [], (), wo, row(b_o_a[i]), xs, ln_g, ln_b)
            keep = min(A_WINDOW, seq)
            kv_p = qkv_p.reshape(n_batch, seq, A_QKV)[:, seq - keep:, nq:]
            outs_kv.setdefault("a_p", []).append(kv_p.reshape(n_batch, keep, 2, A_KV_HEADS, HEAD_DIM))
            outs_kv.setdefault("a_s", []).append(qkv_s[:, nq:].reshape(db, t_len, 2, A_KV_HEADS, HEAD_DIM))
        else:
            gw = B_GROUP_WIDTH
            col_scale = jnp.concatenate([jnp.full((N_B_GROUPS * gw,), scale, _F32),
                                         jnp.ones((2 * N_B_GROUPS * gw,), _F32)])
            w = (w_qkv_b[i] * col_scale).astype(_BF16)
            o_ps, l_ps, o_ss, l_ss = [], [], [], []
            for g, (window, dil) in enumerate(B_GROUPS):
                outs = _qkv_group(xp, w, cos_p, sin_p, dil, g)
                q_p, kv_p, kvr_p = outs if dil > 1 else (outs[0], outs[1], outs[1])
                q_s, kv_s = _qkv_group(xs, w, cos_s, sin_s, 1, g)
                o, l = _band_attention_b(q_p, kvr_p, dil, n_batch)
                o_ps.append(o)
                l_ps.append(l)
                kv_s5 = kv_s.reshape(db, t_len, 2, B_HEADS, HEAD_DIM)
                o, l = _step_b(q_s, kv_s, caches_b[g][i], dil)
                o_ss.append(o)
                l_ss.append(l)
                keep = min(window, seq)
                kv_keep = kv_p.reshape(n_batch, seq, 2 * gw)[:, seq - keep:]
                outs_kv.setdefault("b%d_p" % g, []).append(
                    kv_keep.reshape(n_batch, keep, 2, B_HEADS, HEAD_DIM))
                outs_kv.setdefault("b%d_s" % g, []).append(kv_s5)
            wo = w_o_b[i].astype(_BF16)
            xp = _proj_ln(o_ps, l_ps, [dil for _, dil in B_GROUPS], wo, None, xp, ln_g, ln_b)
            xs = _proj_ln(o_ss, l_ss, [1] * N_B_GROUPS, wo, None, xs, ln_g, ln_b)

        ffn_g, ffn_b = row(ln_ffn_g[layer]), row(ln_ffn_b[layer])
        xp = _peer_ffn_ln(xp, wq_all, keys_all, u_all, v_all, ffn_g, ffn_b, layer)
        xs = _peer_ffn_ln(xs, wq_all, keys_all, u_all, v_all, ffn_g, ffn_b, layer)

    return (xp.reshape(n_batch, seq, D_MODEL), xs.reshape(db, t_len, D_MODEL),
            jnp.stack(outs_kv["a_p"]), jnp.stack(outs_kv["a_s"]),
            jnp.stack(outs_kv["b0_p"]), jnp.stack(outs_kv["b0_s"]),
            jnp.stack(outs_kv["b1_p"]), jnp.stack(outs_kv["b1_s"]),
            jnp.stack(outs_kv["b2_p"]), jnp.stack(outs_kv["b2_s"]))
```

```python
import functools
import math

import jax
import jax.numpy as jnp
from jax import lax
from jax.experimental import pallas as pl
from jax.experimental.pallas import tpu as pltpu

D_MODEL = 2048
SEQ = 2048
DEPTH = 2
DEC_SEQ = 4
PAST_LEN = 8192
HEAD_DIM = 64
ROPE_THETA = 10000.0
BLOCK = 128
A_HEADS = D_MODEL // HEAD_DIM
A_KV_HEADS = A_HEADS // 8
A_GROUP = A_HEADS // A_KV_HEADS
A_WINDOW = 128
A_QKV = (A_HEADS + 2 * A_KV_HEADS) * HEAD_DIM
B_GROUPS = ((128, 1), (512, 4), (2048, 16))
N_B_GROUPS = len(B_GROUPS)
B_HEADS = D_MODEL // (2 * HEAD_DIM)
B_GROUP_WIDTH = B_HEADS * HEAD_DIM
B_KEYS = 128
PEER_HEADS = 8
PEER_N_KEYS = 128
PEER_N_EXPERTS = PEER_N_KEYS * PEER_N_KEYS
PEER_TOPK = 16
PEER_KEY_DIM = 256
DN_ALPHA = (2.0 * DEPTH) ** 0.25
LN_EPS = 1e-5

LANES = 128
SUBLANES = 8
TOKEN_TILE = 512
VMEM_LIMIT = 56 * 1024 * 1024

_F32 = jnp.float32
_BF16 = jnp.bfloat16
_NEG_INF = float("-inf")
_NT = (((1,), (1,)), ((), ()))


def _cparams(*sem):
    return pltpu.CompilerParams(dimension_semantics=sem, vmem_limit_bytes=VMEM_LIMIT)


MXU_COLS = 256


def _rope(blk, cos, sin):
    lane = lax.broadcasted_iota(jnp.int32, blk.shape, 1)
    first_half = (lane % HEAD_DIM) < (HEAD_DIM // 2)
    partner = jnp.where(first_half, pltpu.roll(blk, LANES - HEAD_DIM // 2, 1),
                        pltpu.roll(blk, HEAD_DIM // 2, 1))
    return blk * cos + partner * sin


def _project_chunks(xb, w_ref, emit, bias_ref=None):
    for c in range(w_ref.shape[1] // MXU_COLS):
        cols = slice(c * MXU_COLS, (c + 1) * MXU_COLS)
        acc = jnp.dot(xb, w_ref[:, cols], preferred_element_type=_F32)
        if bias_ref is not None:
            acc = acc + bias_ref[:, cols]
        for h in range(MXU_COLS // LANES):
            emit(c * (MXU_COLS // LANES) + h, acc[:, h * LANES:(h + 1) * LANES])


def _qkv_rope_kernel(x_ref, w_ref, b_ref, cos_ref, sin_ref, o_ref, *, n_rope_groups):
    def emit(g, blk):
        if g < n_rope_groups:
            blk = _rope(blk, cos_ref[...], sin_ref[...])
        o_ref[:, g * LANES:(g + 1) * LANES] = blk

    _project_chunks(x_ref[...].astype(_BF16), w_ref, emit, b_ref)


def _qkv_rope(x, w_bf16, bias, cos, sin, n_rope_cols):
    n, k = x.shape
    ncols = w_bf16.shape[1]
    tm = TOKEN_TILE
    return pl.pallas_call(
        functools.partial(_qkv_rope_kernel, n_rope_groups=n_rope_cols // LANES),
        grid=(n // tm,),
        in_specs=[
            pl.BlockSpec((tm, k), lambda i: (i, 0)),
            pl.BlockSpec((k, ncols), lambda i: (0, 0)),
            pl.BlockSpec((1, ncols), lambda i: (0, 0)),
            pl.BlockSpec((tm, LANES), lambda i: (i, 0)),
            pl.BlockSpec((tm, LANES), lambda i: (i, 0)),
        ],
        out_specs=pl.BlockSpec((tm, ncols), lambda i: (i, 0)),
        out_shape=jax.ShapeDtypeStruct((n, ncols), _F32),
        compiler_params=_cparams("parallel"),
        name="qkv_rope",
    )(x, w_bf16, bias, cos, sin)


def _qkv_group_kernel(x_ref, wq_ref, wk_ref, wv_ref, cos_ref, sin_ref, *refs, dil):
    if dil == 1:
        q_ref, kv_ref = refs
    else:
        q_ref, kv_ref, kvr_ref, scr = refs
    xb = x_ref[...].astype(_BF16)
    per_residue = x_ref.shape[0] // dil
    width = wq_ref.shape[1]

    def emitter(token_ref, residue_ref, rotate, col0, slot):
        def emit(g, blk):
            lanes = slice(col0 + g * LANES, col0 + (g + 1) * LANES)
            if rotate:
                blk = _rope(blk, cos_ref[...], sin_ref[...])
            if token_ref is not None:
                token_ref[:, lanes] = blk.astype(token_ref.dtype)
            if residue_ref is not None:
                scr[slot, g] = blk
                for r in range(dil):
                    residue_ref[r, :, lanes] = (
                        scr[slot, g, pl.ds(r, per_residue, stride=dil), :].astype(residue_ref.dtype))
        return emit

    kvr = kvr_ref if dil > 1 else None
    _project_chunks(xb, wq_ref, emitter(q_ref if dil == 1 else None, q_ref if dil > 1 else None,
                                        True, 0, 0))
    _project_chunks(xb, wk_ref, emitter(kv_ref, kvr, True, 0, 1))
    _project_chunks(xb, wv_ref, emitter(kv_ref, kvr, False, width, 2))


def _qkv_group(x, w_bf16, cos, sin, dil, group):
    n, k = x.shape
    tm, tn = TOKEN_TILE, B_GROUP_WIDTH
    kv_spec = pl.BlockSpec((tm, 2 * tn), lambda i: (i, 0))
    kv_sds = jax.ShapeDtypeStruct((n, 2 * tn), _F32)
    if dil == 1:
        out_specs = [pl.BlockSpec((tm, tn), lambda i: (i, 0)), kv_spec]
        out_shape = [jax.ShapeDtypeStruct((n, tn), _BF16), kv_sds]
        scratch = []
    else:
        tiles = SEQ // tm
        res_rows = (n // SEQ * dil, SEQ // dil)
        res_spec = lambda w: pl.BlockSpec((dil, tm // dil, w), lambda i: (i // tiles, i % tiles, 0))
        out_specs = [res_spec(tn), kv_spec, res_spec(2 * tn)]
        out_shape = [jax.ShapeDtypeStruct(res_rows + (tn,), _BF16), kv_sds,
                     jax.ShapeDtypeStruct(res_rows + (2 * tn,), _BF16)]
        scratch = [pltpu.VMEM((3, tn // LANES, tm, LANES), _F32)]
    w_spec = lambda part: pl.BlockSpec((k, tn), lambda i: (0, part * N_B_GROUPS + group),
                                       pipeline_mode=pl.Buffered(1))
    return pl.pallas_call(
        functools.partial(_qkv_group_kernel, dil=dil),
        grid=(n // tm,),
        in_specs=[
            pl.BlockSpec((tm, k), lambda i: (i, 0)),
            w_spec(0), w_spec(1), w_spec(2),
            pl.BlockSpec((tm, LANES), lambda i: (i, 0)),
            pl.BlockSpec((tm, LANES), lambda i: (i, 0)),
        ],
        out_specs=out_specs,
        out_shape=out_shape,
        scratch_shapes=scratch,
        compiler_params=_cparams("parallel"),
        name="qkv_group_d%d" % dil,
    )(x, w_bf16, w_bf16, w_bf16, cos, sin)


def _band_attn_kernel(sink_ref, q_ref, kc_ref, vc_ref, kp_ref, vp_ref, *out_refs, kv_share,
                      has_sink, with_lse):
    o_ref = out_refs[0]
    lse_ref = out_refs[1] if with_lse else None
    j = pl.program_id(2)
    qi = lax.broadcasted_iota(jnp.int32, (BLOCK, 2 * BLOCK), 0)
    kj = lax.broadcasted_iota(jnp.int32, (BLOCK, 2 * BLOCK), 1)
    first_key = jnp.maximum(qi, jnp.where(j > 0, 0, BLOCK))
    valid = (kj >= first_key) & (kj <= qi + BLOCK)
    lane = lax.broadcasted_iota(jnp.int32, (BLOCK, LANES), 1)
    low = lane < HEAD_DIM

    def keys(cur_ref, prev_ref, kv_head):
        grp = kv_head // 2 if kv_share else kv_head
        lanes = slice(grp * LANES, (grp + 1) * LANES)
        parts = [prev_ref[:, lanes], cur_ref[:, lanes]]
        if kv_share:
            keep = low if kv_head % 2 == 0 else ~low
            parts = [jnp.where(keep, x, pltpu.roll(x, HEAD_DIM, 1)) for x in parts]
        return jnp.concatenate(parts, axis=0).astype(_BF16)

    for p in range(q_ref.shape[1] // LANES):
        if kv_share == 0 or p % kv_share == 0:
            kv_head = p // kv_share if kv_share else p
            kcat = keys(kc_ref, kp_ref, kv_head)
            vcat = keys(vc_ref, vp_ref, kv_head)
        q2 = q_ref[:, p * LANES:(p + 1) * LANES]
        outs, lses = [], []
        for h in range(2):
            qm = jnp.where(low if h == 0 else ~low, q2, 0.0).astype(_BF16)
            s = lax.dot_general(qm, kcat, _NT, preferred_element_type=_F32)
            s = jnp.where(valid, s, _NEG_INF)
            m = jnp.max(s, axis=-1, keepdims=True)
            if has_sink:
                sink = sink_ref[2 * p + h]
                m = jnp.maximum(m, sink)
            e = jnp.exp(s - m)
            den = jnp.sum(e, axis=-1, keepdims=True)
            if has_sink:
                den = den + jnp.exp(sink - m)
            pn = (e / den).astype(_BF16)
            outs.append(jnp.dot(pn, vcat, preferred_element_type=_F32))
            lses.append(m + jnp.log(den))
        o_ref[:, p * LANES:(p + 1) * LANES] = jnp.where(low, outs[0], outs[1])
        if with_lse:
            lse_ref[:, p * LANES:(p + 1) * LANES] = jnp.where(low, lses[0], lses[1])


def _band_attention(sinks, q, kv, *, q_col, k_col, v_col, q_width, kv_width, n_seq, nb, kv_share,
                    with_lse, name):
    cur = lambda b, r, j, s: b * nb + j
    prev = lambda b, r, j, s: b * nb + jnp.maximum(j - 1, 0)
    blk = lambda rows, width, col: pl.BlockSpec((BLOCK, width), lambda *a: (rows(*a), col))
    in_specs = [blk(cur, q_width, q_col), blk(cur, kv_width, k_col), blk(cur, kv_width, v_col),
                blk(prev, kv_width, k_col), blk(prev, kv_width, v_col)]
    args = [q, kv, kv, kv, kv]
    o_spec = blk(cur, q_width, 0)
    o_sds = jax.ShapeDtypeStruct((n_seq * nb * BLOCK, q_width), _F32)
    return pl.pallas_call(
        functools.partial(_band_attn_kernel, kv_share=kv_share,
                          has_sink=sinks is not None, with_lse=with_lse),
        grid_spec=pltpu.PrefetchScalarGridSpec(
            num_scalar_prefetch=1,
            grid=(n_seq, 1, nb),
            in_specs=in_specs,
            out_specs=[o_spec, o_spec] if with_lse else [o_spec],
        ),
        out_shape=[o_sds, o_sds] if with_lse else [o_sds],
        compiler_params=_cparams("parallel", "arbitrary", "arbitrary"),
        name=name,
    )(jnp.zeros((1,), _F32) if sinks is None else sinks, *args)


def _band_attention_a(qkv, sinks, n_batch):
    nq = A_HEADS * HEAD_DIM
    nk = A_KV_HEADS * HEAD_DIM
    (o,) = _band_attention(sinks, qkv, qkv, q_col=0, k_col=nq // nk, v_col=nq // nk + 1,
                           q_width=nq, kv_width=nk, n_seq=n_batch, nb=SEQ // BLOCK,
                           kv_share=A_GROUP // 2, with_lse=False, name="band_attn_a")
    return o


def _band_attention_b(q, kv, dil, n_batch):
    gw = B_GROUP_WIDTH
    o, lse = _band_attention(None, q.reshape(-1, gw), kv.reshape(-1, 2 * gw), q_col=0, k_col=0, v_col=1,
                             q_width=gw, kv_width=gw, n_seq=n_batch * dil, nb=SEQ // dil // BLOCK,
                             kv_share=0, with_lse=True, name="band_attn_d%d" % dil)
    return o.reshape(q.shape), lse.reshape(q.shape)


STEP_NEW_ROWS = 8
STEP_CACHE_BYTES = 8 * 1024 * 1024


STEP_A_SAMPLES = 8


def _step_a_kernel(q_ref, kn_ref, vn_ref, cache_ref, sink_ref, o_ref):
    length = cache_ref.shape[-1]
    kv_width = A_KV_HEADS * HEAD_DIM
    n_rows = A_HEADS * STEP_NEW_ROWS
    lane = lax.broadcasted_iota(jnp.int32, (STEP_NEW_ROWS, LANES), 1)
    low = lane < HEAD_DIM
    zeros = jnp.zeros((STEP_NEW_ROWS, LANES), _F32)

    def step_of(shape):
        return (lax.broadcasted_iota(jnp.int32, shape, 0) % STEP_NEW_ROWS) % DEC_SEQ

    valid_c = lax.broadcasted_iota(jnp.int32, (n_rows, length), 1) >= step_of((n_rows, length))
    valid_n = (lax.broadcasted_iota(jnp.int32, (n_rows, STEP_NEW_ROWS), 1)
               <= step_of((n_rows, STEP_NEW_ROWS)))
    sink = sink_ref[...]

    def to_half(x, src_half, dst_half):
        return x if src_half == dst_half else pltpu.roll(x, HEAD_DIM, 1)

    for b in range(q_ref.shape[0]):
        q_tok = q_ref[b]
        blocks = []
        for h in range(A_HEADS):
            kvh = h // A_GROUP
            piece = to_half(q_tok[:, (h // 2) * LANES:(h // 2 + 1) * LANES], h % 2, kvh % 2)
            piece = jnp.where(low if kvh % 2 == 0 else ~low, piece, 0.0)
            groups = [zeros] * (kv_width // LANES)
            groups[kvh // 2] = piece
            blocks.append(jnp.concatenate(groups, axis=1))
        q_bd = jnp.concatenate(blocks, axis=0).astype(_BF16)
        kt = cache_ref[b, 0].reshape(kv_width, length).astype(_BF16)
        vt = cache_ref[b, 1].reshape(kv_width, length).astype(_BF16)
        s_c = jnp.dot(q_bd, kt, preferred_element_type=_F32)
        s_n = lax.dot_general(q_bd, kn_ref[b].astype(_BF16), _NT, preferred_element_type=_F32)
        s_c = jnp.where(valid_c, s_c, _NEG_INF)
        s_n = jnp.where(valid_n, s_n, _NEG_INF)
        m = jnp.maximum(jnp.max(s_c, axis=-1, keepdims=True), jnp.max(s_n, axis=-1, keepdims=True))
        m = jnp.maximum(m, sink)
        e_c = jnp.exp(s_c - m)
        e_n = jnp.exp(s_n - m)
        den = (jnp.sum(e_c, axis=-1, keepdims=True) + jnp.sum(e_n, axis=-1, keepdims=True)
               + jnp.exp(sink - m))
        o_all = lax.dot_general((e_c / den).astype(_BF16), vt, _NT, preferred_element_type=_F32)
        o_all = o_all + jnp.dot((e_n / den).astype(_BF16), vn_ref[b].astype(_BF16),
                                preferred_element_type=_F32)
        for grp in range(A_HEADS // 2):
            halves = []
            for half, h in enumerate((2 * grp, 2 * grp + 1)):
                kvh = h // A_GROUP
                blk = o_all[h * STEP_NEW_ROWS:(h + 1) * STEP_NEW_ROWS,
                            (kvh // 2) * LANES:(kvh // 2 + 1) * LANES]
                halves.append(to_half(blk, kvh % 2, half))
            o_ref[b, :, grp * LANES:(grp + 1) * LANES] = jnp.where(low, halves[0], halves[1])


def _step_b_kernel(q_ref, kn_ref, vn_ref, cache_ref, o_ref, lse_ref, *, dil):
    n_h, length = cache_ref.shape[2], cache_ref.shape[4]
    width = n_h * HEAD_DIM
    n_rows = n_h * STEP_NEW_ROWS
    seg = lax.broadcasted_iota(jnp.int32, (STEP_NEW_ROWS, width), 1) // HEAD_DIM

    def per_head(fn):
        return [fn(h) for h in range(n_h)]

    def own_head(blocks):
        return functools.reduce(jnp.add, [jnp.where(seg == h, blk, 0.0) for h, blk in enumerate(blocks)])

    def step_of(shape):
        return (lax.broadcasted_iota(jnp.int32, shape, 0) % STEP_NEW_ROWS) % DEC_SEQ

    col_c = lax.broadcasted_iota(jnp.int32, (n_rows, length), 1)
    col_n = lax.broadcasted_iota(jnp.int32, (n_rows, STEP_NEW_ROWS), 1)
    if dil == 1:
        valid_c = col_c >= step_of((n_rows, length))
        valid_n = col_n <= step_of((n_rows, STEP_NEW_ROWS))
    else:
        valid_c = (col_c % dil) == step_of((n_rows, length))
        valid_n = col_n == step_of((n_rows, STEP_NEW_ROWS))
    rows = lambda x, h: x[h * STEP_NEW_ROWS:(h + 1) * STEP_NEW_ROWS]

    for b in range(q_ref.shape[0]):
        q_tok = q_ref[b]
        q_bd = jnp.concatenate(per_head(lambda h: jnp.where(seg == h, q_tok, 0.0)), axis=0).astype(_BF16)
        kt = cache_ref[b, 0].reshape(width, length).astype(_BF16)
        vt = cache_ref[b, 1].reshape(width, length).astype(_BF16)
        s_c = jnp.dot(q_bd, kt, preferred_element_type=_F32)
        s_n = lax.dot_general(q_bd, kn_ref[b].astype(_BF16), _NT, preferred_element_type=_F32)
        s_c = jnp.where(valid_c, s_c, _NEG_INF)
        s_n = jnp.where(valid_n, s_n, _NEG_INF)
        m = jnp.maximum(jnp.max(s_c, axis=-1, keepdims=True), jnp.max(s_n, axis=-1, keepdims=True))
        e_c = jnp.exp(s_c - m)
        e_n = jnp.exp(s_n - m)
        den = jnp.sum(e_c, axis=-1, keepdims=True) + jnp.sum(e_n, axis=-1, keepdims=True)
        o_all = lax.dot_general((e_c / den).astype(_BF16), vt, _NT, preferred_element_type=_F32)
        o_all = o_all + jnp.dot((e_n / den).astype(_BF16), vn_ref[b].astype(_BF16),
                                preferred_element_type=_F32)
        o_ref[b] = own_head(per_head(lambda h: rows(o_all, h)))
        lse = m + jnp.log(den)
        lse_ref[b] = own_head(per_head(lambda h: jnp.broadcast_to(rows(lse, h), (STEP_NEW_ROWS, width))))


def _step_b(q_s, kv_s, cache, dil):
    db, length = cache.shape[:2]
    gw = B_GROUP_WIDTH
    cache_t = cache.transpose(0, 2, 3, 4, 1)
    head_bytes = 2 * HEAD_DIM * length * 4
    hb = min(B_HEADS, max(1, STEP_CACHE_BYTES // head_bytes))
    bb = min(db, max(1, STEP_CACHE_BYTES // (head_bytes * hb)))
    pad = lambda x: jnp.pad(x.reshape(db, DEC_SEQ, gw), [(0, 0), (0, STEP_NEW_ROWS - DEC_SEQ), (0, 0)])
    tok_spec = pl.BlockSpec((bb, STEP_NEW_ROWS, hb * HEAD_DIM), lambda b, j: (b, 0, j))
    tok_sds = jax.ShapeDtypeStruct((db, STEP_NEW_ROWS, gw), _F32)
    o, lse = pl.pallas_call(
        functools.partial(_step_b_kernel, dil=dil),
        grid=(db // bb, B_HEADS // hb),
        in_specs=[tok_spec, tok_spec, tok_spec,
                  pl.BlockSpec((bb, 2, hb, HEAD_DIM, length), lambda b, j: (b, 0, j, 0, 0))],
        out_specs=[tok_spec, tok_spec],
        out_shape=[tok_sds, tok_sds],
        compiler_params=_cparams("parallel", "parallel"),
        name="step_b_l%d" % length,
    )(pad(q_s.astype(_F32)), pad(kv_s[:, :gw]), pad(kv_s[:, gw:]), cache_t)
    back = lambda x: x[:, :DEC_SEQ].reshape(db * DEC_SEQ, gw)
    return back(o), back(lse)


def _step_a(qkv_s, cache, sinks, db):
    nq = A_HEADS * HEAD_DIM
    nk = A_KV_HEADS * HEAD_DIM
    pad = lambda x: jnp.pad(x.reshape(db, DEC_SEQ, -1), [(0, 0), (0, STEP_NEW_ROWS - DEC_SEQ), (0, 0)])
    sink_rows = jnp.repeat(sinks, STEP_NEW_ROWS).reshape(A_HEADS * STEP_NEW_ROWS, 1)
    bb = STEP_A_SAMPLES
    q_spec = pl.BlockSpec((bb, STEP_NEW_ROWS, nq), lambda i: (i, 0, 0))
    n_spec = pl.BlockSpec((bb, STEP_NEW_ROWS, nk), lambda i: (i, 0, 0))
    o = pl.pallas_call(
        _step_a_kernel,
        grid=(db // bb,),
        in_specs=[q_spec, n_spec, n_spec,
                  pl.BlockSpec((bb, 2, A_KV_HEADS, HEAD_DIM, A_WINDOW), lambda i: (i, 0, 0, 0, 0)),
                  pl.BlockSpec(sink_rows.shape, lambda i: (0, 0))],
        out_specs=q_spec,
        out_shape=jax.ShapeDtypeStruct((db, STEP_NEW_ROWS, nq), _F32),
        compiler_params=_cparams("parallel"),
        name="step_a",
    )(pad(qkv_s[:, :nq]), pad(qkv_s[:, nq:nq + nk]), pad(qkv_s[:, nq + nk:]),
      cache.transpose(0, 2, 3, 4, 1), sink_rows)
    return o[:, :DEC_SEQ].reshape(db * DEC_SEQ, nq)


def _layer_norm_rows(z, g, b):
    mu = jnp.mean(z, axis=-1, keepdims=True)
    zc = z - mu
    var = jnp.mean(zc * zc, axis=-1, keepdims=True)
    return zc * lax.rsqrt(var + LN_EPS) * g + b


def _proj_ln_kernel(*refs, dils, has_bias):
    n_merge = len(dils)
    n_in = 2 * n_merge if n_merge else 1
    w_ref = refs[n_in]
    pos = n_in + 1
    if has_bias:
        bias_ref = refs[pos]
        pos += 1
    x_ref, g_ref, b_ref, out_ref = refs[pos:pos + 4]
    scratch = refs[pos + 4:]
    tm = x_ref.shape[0]

    if n_merge:
        o_scr, order_scr = scratch[0], scratch[1:]

        def token_order(idx, ref, d, lanes):
            if d == 1:
                return ref[:, lanes]
            scr = order_scr[idx]
            for r in range(d):
                scr[pl.ds(r, tm // d, stride=d), :] = ref[r, :, lanes]
            return scr[...]

        reordered = [i for i, d in enumerate(dils) if d > 1]
        for g in range(w_ref.shape[0] // LANES):
            lanes = slice(g * LANES, (g + 1) * LANES)
            os = [token_order(2 * reordered.index(i) if d > 1 else 0, refs[i], d, lanes)
                  for i, d in enumerate(dils)]
            ls = [token_order(2 * reordered.index(i) + 1 if d > 1 else 0, refs[n_merge + i], d, lanes)
                  for i, d in enumerate(dils)]
            m = functools.reduce(jnp.maximum, ls)
            es = [jnp.exp(l - m) for l in ls]
            den = functools.reduce(jnp.add, es)
            o_scr[:, lanes] = functools.reduce(
                jnp.add, [(e / den) * o for e, o in zip(es, os)]).astype(_BF16)
        o = o_scr[...]
    else:
        o = refs[0][...].astype(_BF16)
    y = jnp.dot(o, w_ref[...], preferred_element_type=_F32)
    if has_bias:
        y = y + bias_ref[...]
    out_ref[...] = _layer_norm_rows(DN_ALPHA * x_ref[...] + y, g_ref[...], b_ref[...])


def _proj_ln(o_list, lse_list, dils, w_bf16, bias, x, ln_g, ln_b, tm=256):
    n = x.shape[0]
    k = w_bf16.shape[0]
    row = lambda i: (i, 0)
    fixed = lambda i: (0, 0)
    tiles = SEQ // tm

    def in_spec(d):
        if d == 1:
            return pl.BlockSpec((tm, k), row)
        return pl.BlockSpec((d, tm // d, k), lambda i: (i // tiles, i % tiles, 0))

    args = list(o_list) + list(lse_list) + [w_bf16]
    specs = [in_spec(d) for d in (dils or (1,))] * (2 if lse_list else 1)
    specs.append(pl.BlockSpec((k, D_MODEL), fixed))
    if bias is not None:
        args.append(bias)
        specs.append(pl.BlockSpec((1, D_MODEL), fixed))
    args += [x, ln_g, ln_b]
    specs += [pl.BlockSpec((tm, D_MODEL), row), pl.BlockSpec((1, D_MODEL), fixed),
              pl.BlockSpec((1, D_MODEL), fixed)]
    scratch = []
    if lse_list:
        scratch = [pltpu.VMEM((tm, k), _BF16)]
        scratch += [pltpu.VMEM((tm, LANES), _F32)] * (2 * sum(d > 1 for d in dils))
    return pl.pallas_call(
        functools.partial(_proj_ln_kernel, dils=tuple(dils), has_bias=bias is not None),
        grid=(n // tm,),
        in_specs=specs,
        out_specs=pl.BlockSpec((tm, D_MODEL), row),
        out_shape=jax.ShapeDtypeStruct((n, D_MODEL), _F32),
        scratch_shapes=scratch,
        compiler_params=_cparams("parallel"),
        name="proj_ln",
    )(*args)


N_KEY_BLOCKS = PEER_HEADS * 2
HALF_KEY_DIM = PEER_KEY_DIM // 2


def _peer_score_kernel(x_ref, wq_ref, keys_ref, sc_ref):
    q = jnp.dot(x_ref[...].astype(_BF16), wq_ref[...], preferred_element_type=_F32).astype(_BF16)
    for blk in range(N_KEY_BLOCKS):
        sc_ref[blk * PEER_N_KEYS:(blk + 1) * PEER_N_KEYS, :] = lax.dot_general(
            keys_ref[blk], q[:, blk * HALF_KEY_DIM:(blk + 1) * HALF_KEY_DIM], _NT,
            preferred_element_type=_F32)


def _peer_scores(x, wq_bf16, keys_bf16, layer):
    n = x.shape[0]
    tm = TOKEN_TILE
    return pl.pallas_call(
        _peer_score_kernel,
        grid=(n // tm,),
        in_specs=[
            pl.BlockSpec((tm, D_MODEL), lambda i: (i, 0)),
            pl.BlockSpec((None, D_MODEL, PEER_HEADS * PEER_KEY_DIM), lambda i: (layer, 0, 0)),
            pl.BlockSpec((None, N_KEY_BLOCKS, PEER_N_KEYS, HALF_KEY_DIM), lambda i: (layer, 0, 0, 0)),
        ],
        out_specs=pl.BlockSpec((N_KEY_BLOCKS * PEER_N_KEYS, tm), lambda i: (0, i)),
        out_shape=jax.ShapeDtypeStruct((N_KEY_BLOCKS * PEER_N_KEYS, n), _F32),
        compiler_params=_cparams("parallel"),
        name="peer_scores",
    )(x, wq_bf16, keys_bf16)


TOPK_TOKENS = 128


def _top1_rows(vals, ids, big):
    m = jnp.max(vals, axis=0, keepdims=True)
    sel = jnp.min(jnp.where(vals == m, ids, big), axis=0, keepdims=True)
    return m, sel, ids == sel


def _sort16_network():
    pairs = []

    def merge(lo, n, r):
        step = r * 2
        if step < n:
            merge(lo, n, step)
            merge(lo + r, n, step)
            pairs.extend((i, i + r) for i in range(lo + r, lo + n - r, step))
        else:
            pairs.append((lo, lo + r))

    def sort(lo, n):
        if n > 1:
            sort(lo, n // 2)
            sort(lo + n // 2, n // 2)
            merge(lo, n, 1)

    sort(0, PEER_TOPK)
    return pairs


def _peer_topk_kernel(sc_ref, e_ref, gate_ref, top_s, top_i, sel_e, sel_g):
    tt = sc_ref.shape[1]
    sub = lax.broadcasted_iota(jnp.int32, (SUBLANES, tt), 0).astype(_F32)
    neg_inf = jnp.full((SUBLANES, tt), _NEG_INF, _F32)

    def bcast(row):
        return jnp.broadcast_to(row, (SUBLANES, tt))

    def sub_max(x):
        return jnp.max(x, axis=0, keepdims=True)

    def head_lists(h):
        s0, s1 = top_s[2 * h], top_s[2 * h + 1]
        i0, i1 = top_i[2 * h] * PEER_N_KEYS, top_i[2 * h + 1]
        return s0, s1, i0, i1

    def write_gates(h, best):
        ex = [jnp.exp(b - best[0]) for b in best]
        den = functools.reduce(jnp.add, ex)
        for k in range(PEER_TOPK):
            sel_g[h, k:k + 1, :] = ex[k] / den

    network = _sort16_network()

    def pop_heads(vals, ids, hit, depth):
        for p in range(depth):
            last = p + 1 == len(vals)
            vals[p] = jnp.where(hit, neg_inf if last else vals[p + 1], vals[p])
            if not last:
                ids[p] = jnp.where(hit, ids[p + 1], ids[p])

    def fast_stage1(blk, tie):
        start = pl.multiple_of(blk * PEER_N_KEYS, PEER_N_KEYS)
        vals = [sc_ref[pl.ds(start + SUBLANES * k, SUBLANES), :] for k in range(PEER_TOPK)]
        ids = [sub + float(SUBLANES * k) for k in range(PEER_TOPK)]
        for i, j in network:
            keep = vals[i] >= vals[j]
            vals[i], vals[j] = jnp.maximum(vals[i], vals[j]), jnp.minimum(vals[i], vals[j])
            ids[i], ids[j] = jnp.where(keep, ids[i], ids[j]), jnp.where(keep, ids[j], ids[i])
        prev = None
        for k in range(PEER_TOPK + 1):
            m = sub_max(vals[0])
            hit = vals[0] == m
            n_hit = jnp.sum(jnp.where(hit, 1.0, 0.0), axis=0, keepdims=True)
            tie = jnp.maximum(tie, jnp.where(n_hit > 1.5, 1.0, 0.0))
            if prev is not None:
                tie = jnp.maximum(tie, jnp.where(m == prev, 1.0, 0.0))
            prev = m
            if k == PEER_TOPK:
                break
            top_s[blk, k:k + 1, :] = m
            top_i[blk, k:k + 1, :] = sub_max(jnp.where(hit, ids[0], -1.0))
            pop_heads(vals, ids, hit, PEER_TOPK - k)
        return tie

    def fast_stage2(h, tie):
        s0, s1, i0, i1 = head_lists(h)
        lo_s, hi_s, lo_i, hi_i = s1[:SUBLANES], s1[SUBLANES:], i1[:SUBLANES], i1[SUBLANES:]
        chain = [bcast(s0[a:a + 1]) + lo_s for a in range(SUBLANES)]
        chain_e = [bcast(i0[a:a + 1]) + lo_i for a in range(SUBLANES)]
        single = [bcast(s0[0:1]) + hi_s, s0[SUBLANES:] + bcast(s1[0:1])]
        single_e = [bcast(i0[0:1]) + hi_i, i0[SUBLANES:] + bcast(i1[0:1])]
        best, prev = [], None
        for k in range(PEER_TOPK + 1):
            m = sub_max(jnp.maximum(chain[0], jnp.maximum(single[0], single[1])))
            hits = [chain[0] == m, single[0] == m, single[1] == m]
            n_hit = jnp.sum(functools.reduce(jnp.add, [jnp.where(x, 1.0, 0.0) for x in hits]),
                            axis=0, keepdims=True)
            tie = jnp.maximum(tie, jnp.where(n_hit > 1.5, 1.0, 0.0))
            if prev is not None:
                tie = jnp.maximum(tie, jnp.where(m == prev, 1.0, 0.0))
            prev = m
            if k == PEER_TOPK:
                break
            best.append(m)
            picked = [jnp.where(x, e, -1.0) for x, e in zip(hits, [chain_e[0]] + single_e)]
            sel_e[h, k:k + 1, :] = sub_max(functools.reduce(jnp.maximum, picked))
            pop_heads(chain, chain_e, hits[0], min(SUBLANES, PEER_TOPK - k))
            single = [jnp.where(x, neg_inf, v) for x, v in zip(hits[1:], single)]
        write_gates(h, best)
        return tie

    no_tie = jnp.zeros((1, tt), _F32)
    tied_keys = jnp.max(lax.fori_loop(0, N_KEY_BLOCKS, fast_stage1, no_tie, unroll=2))
    tied_sums = jnp.max(lax.fori_loop(0, PEER_HEADS, fast_stage2, no_tie, unroll=2))

    @pl.when(tied_keys > 0.5)
    def _():
        key_id = lax.broadcasted_iota(jnp.int32, (PEER_N_KEYS, tt), 0).astype(_F32)

        def stage1(blk, carry):
            s = sc_ref[pl.ds(pl.multiple_of(blk * PEER_N_KEYS, PEER_N_KEYS), PEER_N_KEYS), :]
            for k in range(PEER_TOPK):
                m, sel, onehot = _top1_rows(s, key_id, float(PEER_N_KEYS))
                top_s[blk, k:k + 1, :] = m
                top_i[blk, k:k + 1, :] = sel
                s = jnp.where(onehot, _NEG_INF, s)
            return carry

        lax.fori_loop(0, N_KEY_BLOCKS, stage1, 0, unroll=2)

    @pl.when(jnp.maximum(tied_keys, tied_sums) > 0.5)
    def _():
        flat_id = jnp.concatenate(
            [sub, sub + SUBLANES] + [sub + a * PEER_TOPK for a in range(1, SUBLANES)]
            + [(sub + SUBLANES) * PEER_TOPK], axis=0)

        def stage2(h, carry):
            s0, s1, i0, i1 = head_lists(h)
            lo_s, hi_s, lo_i, hi_i = s1[:SUBLANES], s1[SUBLANES:], i1[:SUBLANES], i1[SUBLANES:]
            cand = jnp.concatenate(
                [bcast(s0[0:1]) + lo_s, bcast(s0[0:1]) + hi_s]
                + [bcast(s0[a:a + 1]) + lo_s for a in range(1, SUBLANES)]
                + [s0[SUBLANES:] + bcast(s1[0:1])], axis=0)
            cand_e = jnp.concatenate(
                [bcast(i0[0:1]) + lo_i, bcast(i0[0:1]) + hi_i]
                + [bcast(i0[a:a + 1]) + lo_i for a in range(1, SUBLANES)]
                + [i0[SUBLANES:] + bcast(i1[0:1])], axis=0)
            best = []
            for k in range(PEER_TOPK):
                m, _, onehot = _top1_rows(cand, flat_id, float(PEER_TOPK * PEER_TOPK))
                best.append(m)
                sel_e[h, k:k + 1, :] = sub_max(jnp.where(onehot, cand_e, -1.0))
                cand = jnp.where(onehot, _NEG_INF, cand)
            write_gates(h, best)
            return carry

        lax.fori_loop(0, PEER_HEADS, stage2, 0, unroll=2)

    n_sel = PEER_HEADS * PEER_TOPK
    e_ref[...] = sel_e[...].reshape(n_sel, tt).T.astype(jnp.int32)
    gate_ref[...] = sel_g[...].reshape(n_sel, tt).T


def _peer_topk(sc_t):
    n = sc_t.shape[1]
    tt = TOPK_TOKENS
    n_sel = PEER_HEADS * PEER_TOPK
    return pl.pallas_call(
        _peer_topk_kernel,
        grid=(n // tt,),
        in_specs=[pl.BlockSpec((N_KEY_BLOCKS * PEER_N_KEYS, tt), lambda i: (0, i))],
        out_specs=[pl.BlockSpec((tt, n_sel), lambda i: (i, 0)),
                   pl.BlockSpec((tt, n_sel), lambda i: (i, 0))],
        out_shape=[jax.ShapeDtypeStruct((n, n_sel), jnp.int32),
                   jax.ShapeDtypeStruct((n, n_sel), _F32)],
        scratch_shapes=[pltpu.VMEM((N_KEY_BLOCKS, PEER_TOPK, tt), _F32),
                        pltpu.VMEM((N_KEY_BLOCKS, PEER_TOPK, tt), _F32),
                        pltpu.VMEM((PEER_HEADS, PEER_TOPK, tt), _F32),
                        pltpu.VMEM((PEER_HEADS, PEER_TOPK, tt), _F32)],
        compiler_params=_cparams("parallel"),
        name="peer_topk",
    )(sc_t)


GATE_TOKENS = 128
GATE_GROUP = 16


def _peer_gate_kernel(e_ref, gate_ref, g_ref, scr):
    tb, _, n_sel = e_ref.shape
    row_id = lax.broadcasted_iota(jnp.int32, (PEER_N_KEYS, n_sel), 0)

    def token(n):
        e = e_ref[n]
        gate = gate_ref[n]
        hi = gate.astype(_BF16).astype(_F32)
        lo = gate - hi
        is_i = row_id == (e >> 7)
        lhs = jnp.concatenate([jnp.where(is_i, hi, 0.0), jnp.where(is_i, lo, 0.0)], axis=1)
        one = jnp.where(row_id == (e & (PEER_N_KEYS - 1)), 1.0, 0.0)
        rhs = jnp.concatenate([one, one], axis=1)
        res = lax.dot_general(lhs.astype(_BF16), rhs.astype(_BF16), _NT, preferred_element_type=_F32)
        for ib in range(PEER_N_KEYS // SUBLANES):
            start = (ib * tb + n) * SUBLANES
            scr[start:start + SUBLANES, :] = res[ib * SUBLANES:(ib + 1) * SUBLANES]

    for grp in range(tb // GATE_GROUP):
        first = grp * GATE_GROUP
        for n in range(first, first + GATE_GROUP):
            token(n)
        for i in range(PEER_N_KEYS):
            start = ((i // SUBLANES) * tb + first) * SUBLANES + i % SUBLANES
            g_ref[first:first + GATE_GROUP, i * PEER_N_KEYS:(i + 1) * PEER_N_KEYS] = (
                scr[pl.ds(start, GATE_GROUP, stride=SUBLANES), :])


def _peer_gates(expert, gate):
    n, n_sel = expert.shape
    tb = GATE_TOKENS
    expert = expert.reshape(n, 1, n_sel)
    gate = gate.reshape(n, 1, n_sel)
    return pl.pallas_call(
        _peer_gate_kernel,
        grid=(n // tb,),
        in_specs=[pl.BlockSpec((tb, 1, n_sel), lambda i: (i, 0, 0)),
                  pl.BlockSpec((tb, 1, n_sel), lambda i: (i, 0, 0))],
        out_specs=pl.BlockSpec((tb, PEER_N_EXPERTS), lambda i: (i, 0)),
        out_shape=jax.ShapeDtypeStruct((n, PEER_N_EXPERTS), _F32),
        scratch_shapes=[pltpu.VMEM((tb * PEER_N_KEYS, PEER_N_KEYS), _F32)],
        compiler_params=_cparams("parallel"),
        name="peer_gates",
    )(expert, gate)


EXPERT_CHUNK = 1024
_SQRT_HALF = math.sqrt(0.5)


def _peer_main_kernel(x_ref, ut_ref, g_ref, v_ref, lng_ref, lnb_ref, o_ref, xb_scr, acc_scr):
    c = pl.program_id(1)

    @pl.when(c == 0)
    def _():
        xb_scr[...] = x_ref[...].astype(_BF16)
        acc_scr[...] = jnp.zeros_like(acc_scr)

    h_parts = []
    for k in range(ut_ref.shape[1] // MXU_COLS):
        cols = slice(k * MXU_COLS, (k + 1) * MXU_COLS)
        a = jnp.dot(xb_scr[...], ut_ref[:, cols], preferred_element_type=_F32)
        gelu = 0.5 * a * (1.0 + lax.erf(a * _SQRT_HALF))
        h_parts.append((g_ref[:, cols] * gelu).astype(_BF16))
    h = jnp.concatenate(h_parts, axis=1)
    acc_scr[...] += jnp.dot(h, v_ref[...], preferred_element_type=_F32)

    @pl.when(c == pl.num_programs(1) - 1)
    def _():
        o_ref[...] = _layer_norm_rows(DN_ALPHA * x_ref[...] + acc_scr[...], lng_ref[...], lnb_ref[...])


def _peer_main(x, ut_bf16, gates, v_bf16, ln_g, ln_b, layer):
    n = x.shape[0]
    tm, ce = TOKEN_TILE, EXPERT_CHUNK
    return pl.pallas_call(
        _peer_main_kernel,
        grid=(n // tm, PEER_N_EXPERTS // ce),
        in_specs=[
            pl.BlockSpec((tm, D_MODEL), lambda i, c: (i, 0)),
            pl.BlockSpec((None, D_MODEL, ce), lambda i, c: (layer, 0, c)),
            pl.BlockSpec((tm, ce), lambda i, c: (i, c)),
            pl.BlockSpec((None, ce, D_MODEL), lambda i, c: (layer, c, 0)),
            pl.BlockSpec((1, D_MODEL), lambda i, c: (0, 0)),
            pl.BlockSpec((1, D_MODEL), lambda i, c: (0, 0)),
        ],
        out_specs=pl.BlockSpec((tm, D_MODEL), lambda i, c: (i, 0)),
        out_shape=jax.ShapeDtypeStruct((n, D_MODEL), _F32),
        scratch_shapes=[pltpu.VMEM((tm, D_MODEL), _BF16), pltpu.VMEM((tm, D_MODEL), _F32)],
        compiler_params=_cparams("parallel", "arbitrary"),
        name="peer_main",
    )(x, ut_bf16, gates, v_bf16, ln_g, ln_b)


def _peer_ffn_ln(x, wq_bf16, keys_bf16, ut_bf16, v_bf16, ln_g, ln_b, layer):
    sc_t = _peer_scores(x, wq_bf16, keys_bf16, layer)
    expert, gate = _peer_topk(sc_t)
    gates = _peer_gates(expert, gate)
    return _peer_main(x, ut_bf16, gates, v_bf16, ln_g, ln_b, layer)


def _rope_tables(pos):
    half = HEAD_DIM // 2
    inv = ROPE_THETA ** (-jnp.arange(half, dtype=_F32) / half)
    ang = pos.astype(_F32)[:, None] * inv[None, :]
    cos, sin = jnp.cos(ang), jnp.sin(ang)
    cos = jnp.tile(cos, (1, LANES // half))
    sin = jnp.tile(jnp.concatenate([-sin, sin], axis=1), (1, LANES // HEAD_DIM))
    return cos, sin


def kernel(x_prompt, x_sample, cache_a_kv, cache_b1_kv, cache_b2_kv, cache_b3_kv, w_qkv_a, b_qkv_a, sinks_a, w_o_a, b_o_a, w_qkv_b, w_o_b, ln_mix_g, ln_mix_b, ln_ffn_g, ln_ffn_b, peer_w_query, peer_sub_keys, peer_u, peer_v):
    n_batch, seq = x_prompt.shape[:2]
    db, t_len = x_sample.shape[:2]
    xp = x_prompt.reshape(n_batch * seq, D_MODEL)
    xs = x_sample.reshape(db * t_len, D_MODEL)
    cos_p, sin_p = _rope_tables(jnp.tile(jnp.arange(seq, dtype=jnp.int32), n_batch))
    cos_s, sin_s = _rope_tables(jnp.tile(PAST_LEN + jnp.arange(t_len, dtype=jnp.int32), db))
    scale = HEAD_DIM ** -0.5
    row = lambda v: v.reshape(1, -1)
    caches_b = (cache_b1_kv, cache_b2_kv, cache_b3_kv)
    outs_kv = {}
    wq_all = peer_w_query.astype(_BF16)
    keys_all = peer_sub_keys.reshape(DEPTH, N_KEY_BLOCKS, PEER_N_KEYS, HALF_KEY_DIM).astype(_BF16)
    ut_all = jnp.swapaxes(peer_u, 1, 2).astype(_BF16)
    v_all = peer_v.astype(_BF16)

    for layer in range(DEPTH):
        i = layer // 2
        ln_g, ln_b = row(ln_mix_g[layer]), row(ln_mix_b[layer])
        if layer % 2 == 0:
            nq = A_HEADS * HEAD_DIM
            nk = A_KV_HEADS * HEAD_DIM
            col_scale = jnp.concatenate([jnp.full((nq,), scale, _F32), jnp.ones((2 * nk,), _F32)])
            w = (w_qkv_a[i] * col_scale).astype(_BF16)
            bias = row(b_qkv_a[i] * col_scale)
            qkv_p = _qkv_rope(xp, w, bias, cos_p, sin_p, nq + nk)
            qkv_s = _qkv_rope(xs, w, bias, cos_s, sin_s, nq + nk)
            o_p = _band_attention_a(qkv_p, sinks_a[i], n_batch)
            o_s = _step_a(qkv_s, cache_a_kv[i], sinks_a[i], db)
            wo = w_o_a[i].astype(_BF16)
            xp = _proj_ln([o_p], [], (), wo, row(b_o_a[i]), xp, ln_g, ln_b)
            xs = _proj_ln([o_s], [], (), wo, row(b_o_a[i]), xs, ln_g, ln_b)
            keep = min(A_WINDOW, seq)
            kv_p = qkv_p.reshape(n_batch, seq, A_QKV)[:, seq - keep:, nq:]
            outs_kv.setdefault("a_p", []).append(kv_p.reshape(n_batch, keep, 2, A_KV_HEADS, HEAD_DIM))
            outs_kv.setdefault("a_s", []).append(qkv_s[:, nq:].reshape(db, t_len, 2, A_KV_HEADS, HEAD_DIM))
        else:
            gw = B_GROUP_WIDTH
            col_scale = jnp.concatenate([jnp.full((N_B_GROUPS * gw,), scale, _F32),
                                         jnp.ones((2 * N_B_GROUPS * gw,), _F32)])
            w = (w_qkv_b[i] * col_scale).astype(_BF16)
            o_ps, l_ps, o_ss, l_ss = [], [], [], []
            for g, (window, dil) in enumerate(B_GROUPS):
                outs = _qkv_group(xp, w, cos_p, sin_p, dil, g)
                q_p, kv_p, kvr_p = outs if dil > 1 else (outs[0], outs[1], outs[1])
                q_s, kv_s = _qkv_group(xs, w, cos_s, sin_s, 1, g)
                o, l = _band_attention_b(q_p, kvr_p, dil, n_batch)
                o_ps.append(o)
                l_ps.append(l)
                kv_s5 = kv_s.reshape(db, t_len, 2, B_HEADS, HEAD_DIM)
                o, l = _step_b(q_s, kv_s, caches_b[g][i], dil)
                o_ss.append(o)
                l_ss.append(l)
                keep = min(window, seq)
                kv_keep = kv_p.reshape(n_batch, seq, 2 * gw)[:, seq - keep:]
                outs_kv.setdefault("b%d_p" % g, []).append(
                    kv_keep.reshape(n_batch, keep, 2, B_HEADS, HEAD_DIM))
                outs_kv.setdefault("b%d_s" % g, []).append(kv_s5)
            wo = w_o_b[i].astype(_BF16)
            xp = _proj_ln(o_ps, l_ps, [dil for _, dil in B_GROUPS], wo, None, xp, ln_g, ln_b)
            xs = _proj_ln(o_ss, l_ss, [1] * N_B_GROUPS, wo, None, xs, ln_g, ln_b)

        ffn_g, ffn_b = row(ln_ffn_g[layer]), row(ln_ffn_b[layer])
        xp = _peer_ffn_ln(xp, wq_all, keys_all, ut_all, v_all, ffn_g, ffn_b, layer)
        xs = _peer_ffn_ln(xs, wq_all, keys_all, ut_all, v_all, ffn_g, ffn_b, layer)

    return (xp.reshape(n_batch, seq, D_MODEL), xs.reshape(db, t_len, D_MODEL),
            jnp.stack(outs_kv["a_p"]), jnp.stack(outs_kv["a_s"]),
            jnp.stack(outs_kv["b0_p"]), jnp.stack(outs_kv["b0_s"]),
            jnp.stack(outs_kv["b1_p"]), jnp.stack(outs_kv["b1_s"]),
            jnp.stack(outs_kv["b2_p"]), jnp.stack(outs_kv["b2_s"]))
```
